```python
import jax, jax.numpy as jnp
from jax import lax
import numpy as np

D_MODEL = 2048
BATCH = 1
SEQ = 8192
DEPTH = 2
DEC_BATCH = 2
DEC_SEQ = 16384
PAST_LEN = 128

N_META = 16
EPS = 1e-6
SSD_D_INNER = D_MODEL
SSD_HEAD_DIM = 64
SSD_N_HEADS = SSD_D_INNER // SSD_HEAD_DIM
SSD_N_GROUPS = 8
SSD_HEADS_PER_GROUP = SSD_N_HEADS // SSD_N_GROUPS
SSD_D_STATE = 128
SSD_CONV = 3
SSD_CHUNK = 128
SSD_CONV_DIM = SSD_D_INNER + 2 * SSD_N_GROUPS * SSD_D_STATE
GLA_N_HEADS = 4
GLA_D_K = D_MODEL // 2
GLA_D_V = D_MODEL
GLA_HEAD_K = GLA_D_K // GLA_N_HEADS
GLA_HEAD_V = GLA_D_V // GLA_N_HEADS
GLA_GATE_RANK = 16
GLA_GATE_TAU = 16.0
GLA_CHUNK = 64
D_FF = 5632
FFN_CONV = 3
IN_SIZES = (SSD_D_INNER, SSD_CONV_DIM, SSD_N_HEADS, GLA_D_K, GLA_D_K, GLA_D_V, GLA_D_V,
            2 * GLA_GATE_RANK, 2 * D_MODEL)
D_IN_TOTAL = (SSD_D_INNER + SSD_CONV_DIM + SSD_N_HEADS + 2 * GLA_D_K + 2 * GLA_D_V
              + 2 * GLA_GATE_RANK + 2 * D_MODEL)

kernel_name = "hybrid_bidir_ssd_gla_convffn_meta"


def rmsnorm(x, w):
    xf = x.astype(jnp.float32)
    y = xf * lax.rsqrt(jnp.mean(xf * xf, axis=-1, keepdims=True) + EPS)
    return (y * w.astype(jnp.float32)).astype(x.dtype)


def dwconv_centred(x, w, b):
    k_width = w.shape[0]
    half = k_width // 2
    length = x.shape[1]
    xp = jnp.pad(x, ((0, 0), (half, half), (0, 0)))
    y = b
    for j in range(k_width):
        y = y + xp[:, j:j + length] * w[j]
    return y


def to_chunks(t, c):
    b, l = t.shape[:2]
    return jnp.moveaxis(t.reshape(b, l // c, c, *t.shape[2:]), 1, 0)


def from_chunks(t):
    nc, b, c = t.shape[:3]
    return jnp.moveaxis(t, 0, 1).reshape(b, nc * c, *t.shape[3:])


def bidirectional(scan_fn, chunk, fwd_inputs, bwd_inputs):
    pad = chunk - N_META
    def prep(t, flip):
        t = jnp.pad(t, [(0, 0), (pad, 0)] + [(0, 0)] * (t.ndim - 2))
        return jnp.flip(t, 1) if flip else t
    y_f = scan_fn(*[prep(t, False) for t in fwd_inputs])
    y_b = jnp.flip(scan_fn(*[prep(t, True) for t in bwd_inputs]), 1)
    return (y_f + y_b)[:, pad:]


def ssd_scan(x_dt, a, bm, cm):
    out_dtype = x_dt.dtype
    f32 = jnp.float32
    bsz = x_dt.shape[0]
    c = SSD_CHUNK
    causal = jnp.tril(jnp.ones((c, c), dtype=bool))[None, :, :, None, None]

    def step(state, xs):
        xc, ac, bc, cc = xs
        acs = jnp.cumsum(ac, axis=1)
        seg = acs[:, :, None] - acs[:, None, :]
        decay = jnp.exp(jnp.where(causal, seg, -jnp.inf))
        cb = jnp.einsum('blgn,bsgn->blsg', cc, bc)
        y = jnp.einsum('blsg,blsgr,bsgrp->blgrp', cb, decay, xc)
        y = y + jnp.einsum('blgn,bgrpn,blgr->blgrp', cc, state, jnp.exp(acs))
        last = acs[:, -1]
        w = jnp.exp(last[:, None] - acs)
        state = state * jnp.exp(last)[..., None, None] + jnp.einsum('bsgn,bsgr,bsgrp->bgrpn', bc, w, xc)
        return state, y

    state0 = jnp.zeros((bsz, SSD_N_GROUPS, SSD_HEADS_PER_GROUP, SSD_HEAD_DIM, SSD_D_STATE), f32)
    xs = tuple(to_chunks(t.astype(f32), c) for t in (x_dt, a, bm, cm))
    _, ys = lax.scan(step, state0, xs)
    return from_chunks(ys).astype(out_dtype)


def gla_scan(q, k, v, g):
    out_dtype = q.dtype
    f32 = jnp.float32
    bsz = q.shape[0]
    c = GLA_CHUNK
    causal = jnp.tril(jnp.ones((c, c), dtype=bool))[None, :, :, None, None]

    def step(state, xs):
        qc, kc, vc, gc = xs
        gcs = jnp.cumsum(gc, axis=1)
        rel = gcs[:, :, None] - gcs[:, None, :]
        decay = jnp.exp(jnp.where(causal, rel, -jnp.inf))
        att = jnp.einsum('blhd,bshd,blshd->bhls', qc, kc, decay)
        o = jnp.einsum('bhls,bshv->blhv', att, vc)
        o = o + jnp.einsum('blhd,bhdv->blhv', qc * jnp.exp(gcs), state)
        last = gcs[:, -1]
        k_dec = kc * jnp.exp(last[:, None] - gcs)
        state = state * jnp.exp(last)[..., None] + jnp.einsum('bshd,bshv->bhdv', k_dec, vc)
        return state, o

    state0 = jnp.zeros((bsz, GLA_N_HEADS, GLA_HEAD_K, GLA_HEAD_V), f32)
    xs = tuple(to_chunks(t.astype(f32), c) for t in (q, k, v, g))
    _, ys = lax.scan(step, state0, xs)
    return from_chunks(ys).astype(out_dtype)


def ssd_branch(z, xbc, dt_raw, conv_w, conv_b, dt_bias, a_log, d_skip, norm_w, w_out):
    g_, r_ = SSD_N_GROUPS, SSD_HEADS_PER_GROUP
    xbc = jax.nn.silu(dwconv_centred(xbc, conv_w, conv_b))
    xs, bm, cm = jnp.split(xbc, [SSD_D_INNER, SSD_D_INNER + g_ * SSD_D_STATE], axis=-1)
    bsz, l = xs.shape[:2]
    xs = xs.reshape(bsz, l, g_, r_, SSD_HEAD_DIM)
    bm = bm.reshape(bsz, l, g_, SSD_D_STATE)
    cm = cm.reshape(bsz, l, g_, SSD_D_STATE)
    dt_raw = dt_raw.reshape(bsz, l, g_, r_)

    def direction(i):
        dt = jax.nn.softplus(dt_raw + dt_bias[i].reshape(g_, r_))
        a = -jnp.exp(a_log[i].reshape(g_, r_)) * dt
        return (xs * dt[..., None], a, bm, cm)

    y = bidirectional(ssd_scan, SSD_CHUNK, direction(0), direction(1))
    y = y + xs * d_skip.reshape(g_, r_)[..., None]
    y = y.reshape(bsz, l, SSD_D_INNER)
    y = rmsnorm(y * jax.nn.silu(z), norm_w)
    return y @ w_out


def gla_branch(q, k, v, og, gate_lr, gate_up, gate_b, norm_w, w_out):
    bsz, l = q.shape[:2]
    q = q.reshape(bsz, l, GLA_N_HEADS, GLA_HEAD_K) * (GLA_HEAD_K ** -0.5)
    k = k.reshape(bsz, l, GLA_N_HEADS, GLA_HEAD_K)
    v = v.reshape(bsz, l, GLA_N_HEADS, GLA_HEAD_V)
    lr = gate_lr.reshape(bsz, l, 2, GLA_GATE_RANK)

    def direction(i):
        gl = jax.nn.log_sigmoid(lr[:, :, i] @ gate_up[i] + gate_b[i]) / GLA_GATE_TAU
        return (q, k, v, gl.reshape(bsz, l, GLA_N_HEADS, GLA_HEAD_K))

    o = bidirectional(gla_scan, GLA_CHUNK, direction(0), direction(1))
    o = rmsnorm(o, norm_w)
    o = o.reshape(bsz, l, GLA_D_V) * jax.nn.silu(og)
    return o @ w_out


def mixer(xn, w_in, ssd_conv_w, ssd_conv_b, ssd_dt_bias, ssd_a_log, ssd_d, ssd_norm_w, ssd_w_out,
          gla_gate_up, gla_gate_b, gla_norm_w, gla_w_out, w_mix_out):
    proj = xn @ w_in
    splits = [int(s) for s in np.cumsum(IN_SIZES)[:-1]]
    z, xbc, dt_raw, q, k, v, og, gate_lr, merge = jnp.split(proj, splits, axis=-1)
    y_ssd = ssd_branch(z, xbc, dt_raw, ssd_conv_w, ssd_conv_b, ssd_dt_bias, ssd_a_log, ssd_d,
                       ssd_norm_w, ssd_w_out)
    y_gla = gla_branch(q, k, v, og, gate_lr, gla_gate_up, gla_gate_b, gla_norm_w, gla_w_out)
    g_ssd, g_gla = jnp.split(jax.nn.sigmoid(merge), 2, axis=-1)
    return (g_ssd * y_ssd + g_gla * y_gla) @ w_mix_out


def conv_ffn(xn, w_up, conv_w, conv_b, w_down):
    h = dwconv_centred(xn @ w_up, conv_w, conv_b)
    gate, up = jnp.split(h, 2, axis=-1)
    return (jax.nn.silu(gate) * up) @ w_down


def trunk(x, meta_tokens, mix_norm_w, w_in, ssd_conv_w, ssd_conv_b, ssd_dt_bias, ssd_a_log, ssd_d,
          ssd_norm_w, ssd_w_out, gla_gate_up, gla_gate_b, gla_norm_w, gla_w_out, w_mix_out,
          ffn_norm_w, ffn_w_up, ffn_conv_w, ffn_conv_b, ffn_w_down, final_norm_w):
    bsz = x.shape[0]
    meta = jnp.broadcast_to(meta_tokens.astype(x.dtype)[None], (bsz, N_META, D_MODEL))
    h = jnp.concatenate([meta, x], axis=1)
    for i in range(DEPTH):
        h = h + mixer(rmsnorm(h, mix_norm_w[i]), w_in[i], ssd_conv_w[i], ssd_conv_b[i],
                      ssd_dt_bias[i], ssd_a_log[i], ssd_d[i], ssd_norm_w[i], ssd_w_out[i],
                      gla_gate_up[i], gla_gate_b[i], gla_norm_w[i], gla_w_out[i], w_mix_out[i])
        h = h + conv_ffn(rmsnorm(h, ffn_norm_w[i]), ffn_w_up[i], ffn_conv_w[i], ffn_conv_b[i],
                         ffn_w_down[i])
    return rmsnorm(h, final_norm_w)[:, N_META:]


def setup_inputs(seed: int = 0) -> dict:
    key = jax.random.key(seed)
    ks = jax.random.split(key, 24)
    nrm = jax.random.normal
    f32 = jnp.float32
    dt_min, dt_max = 1e-3, 1e-1
    dt = jnp.exp(jax.random.uniform(ks[7], (DEPTH, 2, SSD_N_HEADS), f32)
                 * (np.log(dt_max) - np.log(dt_min)) + np.log(dt_min))
    dt_bias = dt + jnp.log(-jnp.expm1(-dt))
    a_log = jnp.log(jax.random.uniform(ks[8], (DEPTH, 2, SSD_N_HEADS), f32, 1.0, 16.0))
    return {
        "x_prompt": nrm(ks[0], (BATCH, SEQ, D_MODEL), f32),
        "x_sample": nrm(ks[1], (DEC_BATCH, DEC_SEQ, D_MODEL), f32),
        "meta_tokens": nrm(ks[2], (N_META, D_MODEL), f32),
        "mix_norm_w": 1.0 + 0.02 * nrm(ks[3], (DEPTH, D_MODEL), f32),
        "w_in": nrm(ks[4], (DEPTH, D_MODEL, D_IN_TOTAL), f32) * D_MODEL ** -0.5,
        "ssd_conv_w": nrm(ks[5], (DEPTH, SSD_CONV, SSD_CONV_DIM), f32) * SSD_CONV ** -0.5,
        "ssd_conv_b": 0.02 * nrm(ks[6], (DEPTH, SSD_CONV_DIM), f32),
        "ssd_dt_bias": dt_bias,
        "ssd_a_log": a_log,
        "ssd_d": 1.0 + 0.1 * nrm(ks[9], (DEPTH, SSD_N_HEADS), f32),
        "ssd_norm_w": 1.0 + 0.02 * nrm(ks[10], (DEPTH, SSD_D_INNER), f32),
        "ssd_w_out": nrm(ks[11], (DEPTH, SSD_D_INNER, D_MODEL), f32) * SSD_D_INNER ** -0.5,
        "gla_gate_up": nrm(ks[12], (DEPTH, 2, GLA_GATE_RANK, GLA_D_K), f32) * GLA_GATE_RANK ** -0.5,
        "gla_gate_b": 0.1 * nrm(ks[13], (DEPTH, 2, GLA_D_K), f32),
        "gla_norm_w": 1.0 + 0.02 * nrm(ks[14], (DEPTH, GLA_HEAD_V), f32),
        "gla_w_out": nrm(ks[15], (DEPTH, GLA_D_V, D_MODEL), f32) * GLA_D_V ** -0.5,
        "w_mix_out": nrm(ks[16], (DEPTH, D_MODEL, D_MODEL), f32) * D_MODEL ** -0.5,
        "ffn_norm_w": 1.0 + 0.02 * nrm(ks[17], (DEPTH, D_MODEL), f32),
        "ffn_w_up": nrm(ks[18], (DEPTH, D_MODEL, 2 * D_FF), f32) * D_MODEL ** -0.5,
        "ffn_conv_w": nrm(ks[19], (DEPTH, FFN_CONV, 2 * D_FF), f32) * FFN_CONV ** -0.5,
        "ffn_conv_b": 0.02 * nrm(ks[20], (DEPTH, 2 * D_FF), f32),
        "ffn_w_down": nrm(ks[21], (DEPTH, D_FF, D_MODEL), f32) * D_FF ** -0.5,
        "final_norm_w": 1.0 + 0.02 * nrm(ks[22], (D_MODEL,), f32),
    }


def reference(x_prompt, x_sample, meta_tokens, mix_norm_w, w_in, ssd_conv_w, ssd_conv_b, ssd_dt_bias,
              ssd_a_log, ssd_d, ssd_norm_w, ssd_w_out, gla_gate_up, gla_gate_b, gla_norm_w, gla_w_out,
              w_mix_out, ffn_norm_w, ffn_w_up, ffn_conv_w, ffn_conv_b, ffn_w_down, final_norm_w):
    y_prompt = trunk(x_prompt, meta_tokens, mix_norm_w, w_in, ssd_conv_w, ssd_conv_b, ssd_dt_bias,
                     ssd_a_log, ssd_d, ssd_norm_w, ssd_w_out, gla_gate_up, gla_gate_b, gla_norm_w,
                     gla_w_out, w_mix_out, ffn_norm_w, ffn_w_up, ffn_conv_w, ffn_conv_b, ffn_w_down,
                     final_norm_w)
    y_sample = trunk(x_sample, meta_tokens, mix_norm_w, w_in, ssd_conv_w, ssd_conv_b, ssd_dt_bias,
                     ssd_a_log, ssd_d, ssd_norm_w, ssd_w_out, gla_gate_up, gla_gate_b, gla_norm_w,
                     gla_w_out, w_mix_out, ffn_norm_w, ffn_w_up, ffn_conv_w, ffn_conv_b, ffn_w_down,
                     final_norm_w)
    return (y_prompt, y_sample)
```

```python
import functools

import numpy as np
import jax
import jax.numpy as jnp
from jax import lax
from jax.experimental import pallas as pl
from jax.experimental.pallas import tpu as pltpu

F32 = jnp.float32
BF16 = jnp.bfloat16

N_META = 16
EPS = 1e-6
SSD_HEAD_DIM = 64
SSD_N_GROUPS = 8
SSD_HEADS_PER_GROUP = 4
SSD_D_STATE = 128
GLA_N_HEADS = 4
GLA_GATE_RANK = 16
GLA_GATE_TAU = 16.0

LANES = 128
ROW_ALIGN = 256
FRONT_PAD = ROW_ALIGN - N_META
SSD_CHUNK = 128
GLA_CHUNK = 256
ROW_TILE = 1024
PROLOGUE_ROWS = 128
HALO = 16
VMEM_LIMIT = 56 * 1024 * 1024
NEG_INF = float("-inf")


def _cparams(*sem):
    return pltpu.CompilerParams(dimension_semantics=sem, vmem_limit_bytes=VMEM_LIMIT)


def _softplus(x):
    return jnp.maximum(x, 0.0) + jnp.log1p(jnp.exp(-jnp.abs(x)))


def _silu(x):
    return x * jax.nn.sigmoid(x)


def _split3(x):
    x1 = x.astype(BF16)
    r1 = x - x1.astype(F32)
    x2 = r1.astype(BF16)
    x3 = (r1 - x2.astype(F32)).astype(BF16)
    return x1, x2, x3


def _dot(a, b):
    return jnp.dot(a, b, preferred_element_type=F32)


def _dot_nt(a, b):
    return lax.dot_general(a, b, (((1,), (1,)), ((), ())), preferred_element_type=F32)


def _dot_tn(a, b):
    return lax.dot_general(a, b, (((0,), (0,)), ((), ())), preferred_element_type=F32)


def _dot_exact_lhs(t, x):
    x1, x2, x3 = _split3(x)
    return _dot(t, x1) + _dot(t, x2) + _dot(t, x3)


def _dot_exact_rhs(x, t):
    x1, x2, x3 = _split3(x)
    return _dot(x1, t) + _dot(x2, t) + _dot(x3, t)


def _row_valid(row0, n, ranges):
    r = row0 + lax.broadcasted_iota(jnp.int32, (n, 1), 0)
    ok = None
    for lo, hi in ranges:
        m = (r >= lo) & (r < hi)
        ok = m if ok is None else (ok | m)
    return ok


def _is_any(c, values):
    ok = None
    for v in values:
        m = c == v
        ok = m if ok is None else (ok | m)
    return ok


def _inproj_kernel(h_ref, nw_ref, wb_ref, ws_ref, big_ref, small_ref, xn_scr):
    @pl.when(pl.program_id(1) == 0)
    def _():
        for r0 in range(0, h_ref.shape[0], PROLOGUE_ROWS):
            rs = slice(r0, r0 + PROLOGUE_ROWS)
            x = h_ref[rs, :]
            ms = jnp.mean(x * x, axis=-1, keepdims=True)
            xn_scr[rs, :] = (x * lax.rsqrt(ms + EPS) * nw_ref[...]).astype(BF16)
        small_ref[...] = _dot(xn_scr[...], ws_ref[...])

    big_ref[...] = _dot(xn_scr[...], wb_ref[...]).astype(BF16)


def _inproj(h, norm_w, w_big, w_small, tn=1024):
    rows, d = h.shape
    nb = w_big.shape[1]
    return pl.pallas_call(
        _inproj_kernel,
        grid=(rows // ROW_TILE, nb // tn),
        in_specs=[
            pl.BlockSpec((ROW_TILE, d), lambda i, j: (i, 0)),
            pl.BlockSpec((1, d), lambda i, j: (0, 0)),
            pl.BlockSpec((d, tn), lambda i, j: (0, j)),
            pl.BlockSpec((d, LANES), lambda i, j: (0, 0)),
        ],
        out_specs=[
            pl.BlockSpec((ROW_TILE, tn), lambda i, j: (i, j)),
            pl.BlockSpec((ROW_TILE, LANES), lambda i, j: (i, 0)),
        ],
        out_shape=[
            jax.ShapeDtypeStruct((rows, nb), BF16),
            jax.ShapeDtypeStruct((rows, LANES), F32),
        ],
        scratch_shapes=[pltpu.VMEM((ROW_TILE, d), BF16)],
        compiler_params=_cparams("parallel", "arbitrary"),
        name="inproj",
    )(h, norm_w.reshape(1, d), w_big, w_small)


def _conv3_rows(x, prev_row, next_row, w_ref, b_ref):
    n = x.shape[0]
    row = lax.broadcasted_iota(jnp.int32, (n, 1), 0)
    up = jnp.where(row == 0, prev_row, pltpu.roll(x, 1, 0))
    dn = jnp.where(row == n - 1, next_row, pltpu.roll(x, n - 1, 0))
    y = b_ref[...] + up * w_ref[0:1, :]
    y = y + x * w_ref[1:2, :]
    return y + dn * w_ref[2:3, :]


def _ssd_conv_kernel(x_ref, prev_ref, next_ref, w_ref, b_ref, o_ref):
    x = x_ref[...].astype(F32)
    prev_row = prev_ref[HALO - 1:HALO, :].astype(F32)
    next_row = next_ref[0:1, :].astype(F32)
    o_ref[...] = _silu(_conv3_rows(x, prev_row, next_row, w_ref, b_ref)).astype(BF16)


def _ssd_conv(big, conv_w, conv_b, col0, tr=512, tc=2048):
    rows = big.shape[0]
    c = conv_w.shape[1]
    cb0 = col0 // tc
    hb = tr // HALO
    last = rows // HALO - 1
    return pl.pallas_call(
        _ssd_conv_kernel,
        grid=(rows // tr, c // tc),
        in_specs=[
            pl.BlockSpec((tr, tc), lambda i, j: (i, cb0 + j)),
            pl.BlockSpec((HALO, tc), lambda i, j: (jnp.maximum(i * hb - 1, 0), cb0 + j)),
            pl.BlockSpec((HALO, tc), lambda i, j: (jnp.minimum((i + 1) * hb, last), cb0 + j)),
            pl.BlockSpec((3, tc), lambda i, j: (0, j)),
            pl.BlockSpec((1, tc), lambda i, j: (0, j)),
        ],
        out_specs=pl.BlockSpec((tr, tc), lambda i, j: (i, j)),
        out_shape=jax.ShapeDtypeStruct((rows, c), BF16),
        compiler_params=_cparams("parallel", "parallel"),
        name="ssd_conv",
    )(big, big, big, conv_w, conv_b.reshape(1, c))


def _ssd_scan_kernel(x_ref, b_ref, c_ref, small_ref, dtb_ref, alog_ref, tri_ref, trit_ref,
                     mask_ref, exp_ref, y_ref, state_scr, *, n_chunks, firsts, lasts, valid):
    d = pl.program_id(0)
    g = pl.program_id(1)
    c = g + d * (n_chunks - 1 - 2 * g)
    ch = SSD_CHUNK

    reset = jnp.where(d == 0, _is_any(c, firsts), _is_any(c, lasts))

    @pl.when(reset)
    def _():
        state_scr[...] = jnp.zeros_like(state_scr)

    ok = _row_valid(c * ch, ch, valid)
    dt = _softplus(small_ref[...] + dtb_ref[0])
    dt = jnp.where(ok, dt, 0.0)
    a = -jnp.exp(alog_ref[0]) * dt
    tri = tri_ref[0]
    acs = _dot_exact_lhs(tri, a)
    acs_t = _dot_exact_rhs(a.T, trit_ref[0])
    dt_t = dt.T
    total = jnp.sum(a, axis=0, keepdims=True)
    expand = exp_ref[...]
    w_in = jnp.exp(total - acs) * dt

    def expand_rows(v):
        v1 = v.astype(BF16)
        v2 = (v - v1.astype(F32)).astype(BF16)
        return _dot(v1, expand) + _dot(v2, expand)

    w_exp = expand_rows(w_in)
    ea_exp = expand_rows(jnp.exp(acs))
    et_exp = expand_rows(jnp.broadcast_to(jnp.exp(total), (8, LANES)))[0:1, :]
    mask = mask_ref[0]

    gw = SSD_HEADS_PER_GROUP * SSD_HEAD_DIM
    lane = lax.broadcasted_iota(jnp.int32, (1, gw), 1)
    for grp in range(SSD_N_GROUPS):
        bg = b_ref[:, grp * SSD_D_STATE:(grp + 1) * SSD_D_STATE]
        cg = c_ref[:, grp * SSD_D_STATE:(grp + 1) * SSD_D_STATE]
        xg = x_ref[:, grp * gw:(grp + 1) * gw]
        cb = _dot_nt(cg, bg)
        st = state_scr[grp]
        y = _dot(cg, st.astype(BF16)) * ea_exp[:, grp * gw:(grp + 1) * gw]
        for hh in range(SSD_HEADS_PER_GROUP):
            h = grp * SSD_HEADS_PER_GROUP + hh
            seg = acs[:, h:h + 1] - acs_t[h:h + 1, :] + mask
            m = (cb * jnp.exp(seg) * dt_t[h:h + 1, :]).astype(BF16)
            in_head = (lane >= hh * SSD_HEAD_DIM) & (lane < (hh + 1) * SSD_HEAD_DIM)
            y = y + _dot(m, jnp.where(in_head, xg, jnp.zeros_like(xg)))
        y_ref[0, :, grp * gw:(grp + 1) * gw] = y.astype(BF16)
        xw = (xg.astype(F32) * w_exp[:, grp * gw:(grp + 1) * gw]).astype(BF16)
        state_scr[grp] = st * et_exp[:, grp * gw:(grp + 1) * gw] + _dot_tn(bg, xw)


def _ssd_scan(xbc, small, dt_bias, a_log, layout):
    rows = xbc.shape[0]
    ch = SSD_CHUNK
    d_inner = SSD_N_GROUPS * SSD_HEADS_PER_GROUP * SSD_HEAD_DIM
    gn = SSD_N_GROUPS * SSD_D_STATE
    n_heads = SSD_N_GROUPS * SSD_HEADS_PER_GROUP
    n_chunks = rows // ch
    firsts = tuple(s // ch for s, _ in layout["spans"])
    lasts = tuple((s + n) // ch - 1 for s, n in layout["spans"])

    idx = np.arange(ch)
    lower = (idx[:, None] >= idx[None, :])
    tri = np.stack([lower, lower.T]).astype(np.float32)
    mask = np.where(tri > 0, 0.0, NEG_INF).astype(np.float32)
    expand = np.zeros((LANES, d_inner), np.float32)
    for h in range(n_heads):
        expand[h, h * SSD_HEAD_DIM:(h + 1) * SSD_HEAD_DIM] = 1.0
    pad = ((0, 0), (0, 0), (0, LANES - n_heads))
    dtb = jnp.pad(dt_bias.reshape(2, 1, n_heads), pad)
    alog = jnp.pad(a_log.reshape(2, 1, n_heads), pad)

    def cmap(d, g):
        return g + d * (n_chunks - 1 - 2 * g)

    kern = functools.partial(_ssd_scan_kernel, n_chunks=n_chunks, firsts=firsts, lasts=lasts,
                             valid=layout["valid"])
    return pl.pallas_call(
        kern,
        grid=(2, n_chunks),
        in_specs=[
            pl.BlockSpec((ch, d_inner), lambda d, g: (cmap(d, g), 0)),
            pl.BlockSpec((ch, gn), lambda d, g: (cmap(d, g), d_inner // gn)),
            pl.BlockSpec((ch, gn), lambda d, g: (cmap(d, g), d_inner // gn + 1)),
            pl.BlockSpec((ch, LANES), lambda d, g: (cmap(d, g), 0)),
            pl.BlockSpec((1, 1, LANES), lambda d, g: (d, 0, 0)),
            pl.BlockSpec((1, 1, LANES), lambda d, g: (d, 0, 0)),
            pl.BlockSpec((1, ch, ch), lambda d, g: (d, 0, 0)),
            pl.BlockSpec((1, ch, ch), lambda d, g: (1 - d, 0, 0)),
            pl.BlockSpec((1, ch, ch), lambda d, g: (d, 0, 0)),
            pl.BlockSpec((LANES, d_inner), lambda d, g: (0, 0)),
        ],
        out_specs=pl.BlockSpec((1, ch, d_inner), lambda d, g: (d, cmap(d, g), 0)),
        out_shape=jax.ShapeDtypeStruct((2, rows, d_inner), BF16),
        scratch_shapes=[pltpu.VMEM((SSD_N_GROUPS, SSD_D_STATE, SSD_HEADS_PER_GROUP * SSD_HEAD_DIM), F32)],
        compiler_params=_cparams("arbitrary", "arbitrary"),
        name="ssd_scan",
    )(xbc, xbc, xbc, small, dtb, alog, jnp.asarray(tri, BF16), jnp.asarray(tri, BF16),
      jnp.asarray(mask), jnp.asarray(expand, BF16))


def _gla_levels():
    lv, m = [], GLA_CHUNK // 2
    while m >= 1:
        lv.append(m)
        m //= 2
    return tuple(lv)


def _gla_ref_rows(cs_scr, m, d):
    ch, width = cs_scr.shape

    def row(r):
        return cs_scr[pl.ds(r - d, 1), :]

    pieces = []
    if 2 * m >= 8:
        for p in range(ch // (2 * m)):
            pieces.append(jnp.broadcast_to(row(2 * m * p + m), (2 * m, width)))
    else:
        sub = lax.broadcasted_iota(jnp.int32, (8, 1), 0)
        for p in range(ch // 8):
            acc = jnp.broadcast_to(row(8 * p + m), (8, width))
            for q in range(1, 8 // (2 * m)):
                cand = jnp.broadcast_to(row(8 * p + 2 * m * q + m), (8, width))
                acc = jnp.where(sub >= 2 * m * q, cand, acc)
            pieces.append(acc)
    return jnp.concatenate(pieces, axis=0)


def _gla_scan_kernel(q_ref, k_ref, v_ref, small_ref, gu_ref, gb_ref, tri_ref, lev_ref, o_ref,
                     state_scr, cs_scr, att_scr, *, n_chunks, firsts, lasts, valid, dk, dv):
    d = pl.program_id(0)
    g = pl.program_id(1)
    c = g + d * (n_chunks - 1 - 2 * g)
    ch = GLA_CHUNK
    levels = _gla_levels()

    reset = jnp.where(d == 0, _is_any(c, firsts), _is_any(c, lasts))

    @pl.when(reset)
    def _():
        state_scr[...] = jnp.zeros_like(state_scr)

    ok = _row_valid(c * ch, ch, valid)
    u = _dot(small_ref[...].astype(BF16), gu_ref[0]) + gb_ref[0]
    gate = jnp.where(ok, -_softplus(-u) / GLA_GATE_TAU, 0.0)
    cs = _dot_exact_lhs(tri_ref[0], gate)
    cs_scr[...] = cs
    total = jnp.sum(gate, axis=0, keepdims=True)
    lev = lev_ref[0]
    row = lax.broadcasted_iota(jnp.int32, (ch, 1), 0)
    scale = dk ** -0.5

    att_scr[...] = jnp.zeros_like(att_scr)
    for i, m in enumerate(levels):
        is_target = ((row // m) & 1) == (1 - d)
        e = jnp.exp(-jnp.abs(cs - _gla_ref_rows(cs_scr, m, d)))
        for h in range(GLA_N_HEADS):
            sl = slice(h * dk, (h + 1) * dk)
            qh = q_ref[:, sl].astype(F32) * scale
            kh = k_ref[:, sl].astype(F32)
            x = (jnp.where(is_target, qh, kh) * e[:, sl]).astype(BF16)
            att_scr[h] = jnp.where(lev == i, _dot_nt(x, x), att_scr[h])

    for h in range(GLA_N_HEADS):
        sl = slice(h * dk, (h + 1) * dk)
        vs = slice(h * dv, (h + 1) * dv)
        qh = q_ref[:, sl].astype(F32) * scale
        kh = k_ref[:, sl].astype(F32)
        vh = v_ref[:, vs]
        diag = _dot_nt(qh.astype(BF16), kh.astype(BF16))
        att = jnp.where(lev == len(levels), diag, att_scr[h]).astype(BF16)
        st = state_scr[h]
        qd = (qh * jnp.exp(cs[:, sl])).astype(BF16)
        o = _dot(att, vh) + _dot_nt(qd, st.astype(BF16))
        o_ref[0, :, vs] = o.astype(BF16)
        kd = (kh * jnp.exp(total[:, sl] - cs[:, sl])).astype(BF16)
        state_scr[h] = st * jnp.exp(total[:, sl]) + _dot_tn(vh, kd)


def _gla_scan(big, small, gate_up, gate_b, layout, col_q, col_k, col_v, dk, dv):
    rows = big.shape[0]
    ch = GLA_CHUNK
    n_chunks = rows // ch
    firsts = tuple(s // ch for s, _ in layout["spans"])
    lasts = tuple((s + n) // ch - 1 for s, n in layout["spans"])
    hk = GLA_N_HEADS * dk
    hv = GLA_N_HEADS * dv
    levels = _gla_levels()

    idx = np.arange(ch)
    l_, s_ = idx[:, None], idx[None, :]
    lower = l_ >= s_
    tri = np.stack([lower, lower.T]).astype(np.float32)
    x = l_ ^ s_
    top = np.floor(np.log2(np.maximum(x, 1))).astype(np.int64)
    lvl = np.where(x == 0, len(levels), (len(levels) - 1) - top)
    lev = np.stack([np.where(l_ >= s_, lvl, -1), np.where(l_ <= s_, lvl, -1)]).astype(np.int32)

    gu = jnp.zeros((2, LANES, hk), F32)
    for dr in range(2):
        lo = 32 + GLA_GATE_RANK * dr
        gu = gu.at[dr, lo:lo + GLA_GATE_RANK].set(gate_up[dr])
    gu = gu.astype(BF16)

    def cmap(d, g):
        return g + d * (n_chunks - 1 - 2 * g)

    kern = functools.partial(_gla_scan_kernel, n_chunks=n_chunks, firsts=firsts, lasts=lasts,
                             valid=layout["valid"], dk=dk, dv=dv)
    return pl.pallas_call(
        kern,
        grid=(2, n_chunks),
        in_specs=[
            pl.BlockSpec((ch, hk), lambda d, g: (cmap(d, g), col_q // hk)),
            pl.BlockSpec((ch, hk), lambda d, g: (cmap(d, g), col_k // hk)),
            pl.BlockSpec((ch, hv), lambda d, g: (cmap(d, g), col_v // hv)),
            pl.BlockSpec((ch, LANES), lambda d, g: (cmap(d, g), 0)),
            pl.BlockSpec((1, LANES, hk), lambda d, g: (d, 0, 0)),
            pl.BlockSpec((1, 1, hk), lambda d, g: (d, 0, 0)),
            pl.BlockSpec((1, ch, ch), lambda d, g: (d, 0, 0)),
            pl.BlockSpec((1, ch, ch), lambda d, g: (d, 0, 0)),
        ],
        out_specs=pl.BlockSpec((1, ch, hv), lambda d, g: (d, cmap(d, g), 0)),
        out_shape=jax.ShapeDtypeStruct((2, rows, hv), BF16),
        scratch_shapes=[
            pltpu.VMEM((GLA_N_HEADS, dv, dk), F32),
            pltpu.VMEM((ch, hk), F32),
            pltpu.VMEM((GLA_N_HEADS, ch, ch), F32),
        ],
        compiler_params=_cparams("arbitrary", "arbitrary"),
        name="gla_scan",
    )(big, big, big, small, gu, gate_b.reshape(2, 1, hk), jnp.asarray(tri, BF16), jnp.asarray(lev))


def _merge_kernel(ys_ref, xs_ref, z_ref, og_ref, ogate_ref, ms_ref, mg_ref, dskip_ref, snw_ref,
                  gnw_ref, ws_ref, wg_ref, o_ref, a_ssd, a_gla, *, dv):
    @pl.when(pl.program_id(1) == 0)
    def _():
        for r0 in range(0, a_ssd.shape[0], PROLOGUE_ROWS):
            rs = slice(r0, r0 + PROLOGUE_ROWS)
            y = ys_ref[0, rs, :].astype(F32) + ys_ref[1, rs, :].astype(F32)
            y = y + xs_ref[rs, :].astype(F32) * dskip_ref[...]
            y = y * _silu(z_ref[rs, :].astype(F32))
            ms = jnp.mean(y * y, axis=-1, keepdims=True)
            a_ssd[rs, :] = (y * lax.rsqrt(ms + EPS) * snw_ref[...]).astype(BF16)
            for h in range(GLA_N_HEADS):
                sl = slice(h * dv, (h + 1) * dv)
                o = og_ref[0, rs, sl].astype(F32) + og_ref[1, rs, sl].astype(F32)
                ms = jnp.mean(o * o, axis=-1, keepdims=True)
                o = o * lax.rsqrt(ms + EPS) * gnw_ref[...]
                a_gla[rs, sl] = (o * _silu(ogate_ref[rs, sl].astype(F32))).astype(BF16)

    y_ssd = _dot(a_ssd[...], ws_ref[...])
    y_gla = _dot(a_gla[...], wg_ref[...])
    merged = jax.nn.sigmoid(ms_ref[...].astype(F32)) * y_ssd
    merged = merged + jax.nn.sigmoid(mg_ref[...].astype(F32)) * y_gla
    o_ref[...] = merged.astype(BF16)


def _merge(y_ssd, xbc, big, o_gla, d_skip, ssd_norm_w, gla_norm_w, w_ssd, w_gla, col_og, col_merge,
           tm=512, tn=512):
    rows, d = xbc.shape[0], w_ssd.shape[1]
    di = w_ssd.shape[0]
    dvt = w_gla.shape[0]
    dv = dvt // GLA_N_HEADS
    kern = functools.partial(_merge_kernel, dv=dv)
    return pl.pallas_call(
        kern,
        grid=(rows // tm, d // tn),
        in_specs=[
            pl.BlockSpec((2, tm, di), lambda i, j: (0, i, 0)),
            pl.BlockSpec((tm, di), lambda i, j: (i, 0)),
            pl.BlockSpec((tm, di), lambda i, j: (i, 0)),
            pl.BlockSpec((2, tm, dvt), lambda i, j: (0, i, 0)),
            pl.BlockSpec((tm, dvt), lambda i, j: (i, col_og // dvt)),
            pl.BlockSpec((tm, tn), lambda i, j: (i, col_merge // tn + j)),
            pl.BlockSpec((tm, tn), lambda i, j: (i, (col_merge + d) // tn + j)),
            pl.BlockSpec((1, di), lambda i, j: (0, 0)),
            pl.BlockSpec((1, di), lambda i, j: (0, 0)),
            pl.BlockSpec((1, dv), lambda i, j: (0, 0)),
            pl.BlockSpec((di, tn), lambda i, j: (0, j)),
            pl.BlockSpec((dvt, tn), lambda i, j: (0, j)),
        ],
        out_specs=pl.BlockSpec((tm, tn), lambda i, j: (i, j)),
        out_shape=jax.ShapeDtypeStruct((rows, d), BF16),
        scratch_shapes=[pltpu.VMEM((tm, di), BF16), pltpu.VMEM((tm, dvt), BF16)],
        compiler_params=_cparams("parallel", "arbitrary"),
        name="merge",
    )(y_ssd, xbc, big, o_gla, big, big, big,
      jnp.repeat(d_skip, SSD_HEAD_DIM).reshape(1, di), ssd_norm_w.reshape(1, di),
      gla_norm_w.reshape(1, dv), w_ssd, w_gla)


def _mm_res_kernel(a_ref, w_ref, h_ref, o_ref):
    o_ref[...] = h_ref[...] + _dot(a_ref[...], w_ref[...])


def _mm_res(a, w, h, tm=512, tn=512):
    rows, k = a.shape
    n = w.shape[1]
    return pl.pallas_call(
        _mm_res_kernel,
        grid=(rows // tm, n // tn),
        in_specs=[
            pl.BlockSpec((tm, k), lambda i, j: (i, 0)),
            pl.BlockSpec((k, tn), lambda i, j: (0, j)),
            pl.BlockSpec((tm, tn), lambda i, j: (i, j)),
        ],
        out_specs=pl.BlockSpec((tm, tn), lambda i, j: (i, j)),
        out_shape=jax.ShapeDtypeStruct((rows, n), F32),
        compiler_params=_cparams("parallel", "arbitrary"),
        name="mm_res",
    )(a, w, h)


def _ffn_up_kernel(h_ref, prev_ref, next_ref, nw_ref, wg_ref, wu_ref, cwg_ref, cwu_ref, cbg_ref,
                   cbu_ref, o_ref, xn_scr, *, valid):
    tm = h_ref.shape[0]

    def norm(x):
        ms = jnp.mean(x * x, axis=-1, keepdims=True)
        return (x * lax.rsqrt(ms + EPS) * nw_ref[...]).astype(BF16)

    @pl.when(pl.program_id(1) == 0)
    def _():
        xn_scr[0:HALO, :] = norm(prev_ref[...])
        for r0 in range(0, tm, PROLOGUE_ROWS):
            xn_scr[HALO + r0:HALO + r0 + PROLOGUE_ROWS, :] = norm(h_ref[r0:r0 + PROLOGUE_ROWS, :])
        xn_scr[HALO + tm:2 * HALO + tm, :] = norm(next_ref[...])

    xn = xn_scr[...]
    n = tm + 2 * HALO

    def conv(u, w_ref, b_ref):
        y = b_ref[...] + pltpu.roll(u, 1, 0)[HALO:HALO + tm] * w_ref[0:1, :]
        y = y + u[HALO:HALO + tm] * w_ref[1:2, :]
        return y + pltpu.roll(u, n - 1, 0)[HALO:HALO + tm] * w_ref[2:3, :]

    gate = conv(_dot(xn, wg_ref[...]), cwg_ref, cbg_ref)
    up = conv(_dot(xn, wu_ref[...]), cwu_ref, cbu_ref)
    ok = _row_valid(pl.program_id(0) * tm, tm, valid)
    o_ref[...] = jnp.where(ok, _silu(gate) * up, 0.0).astype(BF16)


def _ffn_up(h, norm_w, w_up, conv_w, conv_b, layout, tf=512):
    rows, d = h.shape
    dff = w_up.shape[1] // 2
    tm = ROW_TILE
    hb = tm // HALO
    last = rows // HALO - 1
    nj = dff // tf
    kern = functools.partial(_ffn_up_kernel, valid=layout["valid"])
    return pl.pallas_call(
        kern,
        grid=(rows // tm, nj),
        in_specs=[
            pl.BlockSpec((tm, d), lambda i, j: (i, 0)),
            pl.BlockSpec((HALO, d), lambda i, j: (jnp.maximum(i * hb - 1, 0), 0)),
            pl.BlockSpec((HALO, d), lambda i, j: (jnp.minimum((i + 1) * hb, last), 0)),
            pl.BlockSpec((1, d), lambda i, j: (0, 0)),
            pl.BlockSpec((d, tf), lambda i, j: (0, j)),
            pl.BlockSpec((d, tf), lambda i, j: (0, nj + j)),
            pl.BlockSpec((3, tf), lambda i, j: (0, j)),
            pl.BlockSpec((3, tf), lambda i, j: (0, nj + j)),
            pl.BlockSpec((1, tf), lambda i, j: (0, j)),
            pl.BlockSpec((1, tf), lambda i, j: (0, nj + j)),
        ],
        out_specs=pl.BlockSpec((tm, tf), lambda i, j: (i, j)),
        out_shape=jax.ShapeDtypeStruct((rows, dff), BF16),
        scratch_shapes=[pltpu.VMEM((tm + 2 * HALO, d), BF16)],
        compiler_params=_cparams("parallel", "arbitrary"),
        name="ffn_up",
    )(h, h, h, norm_w.reshape(1, d), w_up, w_up, conv_w, conv_w,
      conv_b.reshape(1, 2 * dff), conv_b.reshape(1, 2 * dff))


def _final_norm_kernel(h_ref, nw_ref, o_ref):
    x = h_ref[...]
    ms = jnp.mean(x * x, axis=-1, keepdims=True)
    o_ref[0] = x * lax.rsqrt(ms + EPS) * nw_ref[...]


def _final_norm(h, norm_w, first_row, batch, length, tr=ROW_ALIGN):
    d = h.shape[1]
    per_seq = (ROW_ALIGN + length) // tr
    b0 = (first_row + ROW_ALIGN) // tr
    return pl.pallas_call(
        _final_norm_kernel,
        grid=(batch, length // tr),
        in_specs=[
            pl.BlockSpec((tr, d), lambda b, t: (b0 + b * per_seq + t, 0)),
            pl.BlockSpec((1, d), lambda b, t: (0, 0)),
        ],
        out_specs=pl.BlockSpec((1, tr, d), lambda b, t: (b, t, 0)),
        out_shape=jax.ShapeDtypeStruct((batch, length, d), F32),
        compiler_params=_cparams("parallel", "parallel"),
        name="final_norm",
    )(h, norm_w.reshape(1, d))


def _make_layout(groups):
    seqs, valid, row = [], [], 0
    for batch, length in groups:
        assert length % ROW_ALIGN == 0
        for _ in range(batch):
            n = ROW_ALIGN + length
            seqs.append((row, n))
            valid.append((row + FRONT_PAD, row + n))
            row += n
    rows = (row // ROW_TILE + 1) * ROW_TILE
    spans = tuple(seqs) + ((row, rows - row),)
    return dict(seqs=tuple(seqs), spans=spans, valid=tuple(valid), used_rows=row, rows=rows)


def kernel(x_prompt, x_sample, meta_tokens, mix_norm_w, w_in, ssd_conv_w, ssd_conv_b, ssd_dt_bias,
           ssd_a_log, ssd_d, ssd_norm_w, ssd_w_out, gla_gate_up, gla_gate_b, gla_norm_w, gla_w_out,
           w_mix_out, ffn_norm_w, ffn_w_up, ffn_conv_w, ffn_conv_b, ffn_w_down, final_norm_w):
    depth, d = mix_norm_w.shape
    groups = [(x_prompt.shape[0], x_prompt.shape[1]), (x_sample.shape[0], x_sample.shape[1])]
    layout = _make_layout(groups)
    rows = layout["rows"]

    n_heads = ssd_dt_bias.shape[-1]
    d_inner = n_heads * SSD_HEAD_DIM
    conv_dim = ssd_conv_w.shape[-1]
    d_k = gla_gate_up.shape[-1]
    d_v = gla_w_out.shape[1]
    dk, dv = d_k // GLA_N_HEADS, d_v // GLA_N_HEADS
    sizes = (d_inner, conv_dim, n_heads, d_k, d_k, d_v, d_v, 2 * GLA_GATE_RANK, 2 * d)
    off = np.concatenate([[0], np.cumsum(sizes)])
    assert off[-1] == w_in.shape[-1]
    wide = [0, 1, 3, 4, 5, 6, 8]
    col, pos = {}, 0
    for s in wide:
        col[s] = pos
        pos += sizes[s]

    zero_front = jnp.zeros((FRONT_PAD, d), F32)
    pieces = []
    for x in (x_prompt, x_sample):
        for b in range(x.shape[0]):
            pieces += [zero_front, meta_tokens.astype(F32), x[b]]
    if rows > layout["used_rows"]:
        pieces.append(jnp.zeros((rows - layout["used_rows"], d), F32))
    h = jnp.concatenate(pieces, axis=0)

    for i in range(depth):
        w = w_in[i]
        w_big = jnp.concatenate([w[:, off[s]:off[s + 1]] for s in wide], axis=1).astype(BF16)
        w_small = jnp.concatenate(
            [w[:, off[2]:off[3]], w[:, off[7]:off[8]],
             jnp.zeros((d, LANES - n_heads - 2 * GLA_GATE_RANK), F32)], axis=1).astype(BF16)

        big, small = _inproj(h, mix_norm_w[i], w_big, w_small)
        xbc = _ssd_conv(big, ssd_conv_w[i], ssd_conv_b[i], col[1])
        y_ssd = _ssd_scan(xbc, small, ssd_dt_bias[i], ssd_a_log[i], layout)
        o_gla = _gla_scan(big, small, gla_gate_up[i], gla_gate_b[i], layout,
                          col[3], col[4], col[5], dk, dv)
        merged = _merge(y_ssd, xbc, big, o_gla, ssd_d[i], ssd_norm_w[i], gla_norm_w[i],
                        ssd_w_out[i].astype(BF16), gla_w_out[i].astype(BF16), col[6], col[8])
        h = _mm_res(merged, w_mix_out[i].astype(BF16), h)
        act = _ffn_up(h, ffn_norm_w[i], ffn_w_up[i].astype(BF16), ffn_conv_w[i], ffn_conv_b[i],
                      layout)
        h = _mm_res(act, ffn_w_down[i].astype(BF16), h)

    outs, si = [], 0
    for batch, length in groups:
        outs.append(_final_norm(h, final_norm_w, layout["seqs"][si][0], batch, length))
        si += batch
    return tuple(outs)
```

```python
import functools

import numpy as np
import jax
import jax.numpy as jnp
from jax import lax
from jax.experimental import pallas as pl
from jax.experimental.pallas import tpu as pltpu

F32 = jnp.float32
BF16 = jnp.bfloat16

N_META = 16
EPS = 1e-6
SSD_HEAD_DIM = 64
SSD_N_GROUPS = 8
SSD_HEADS_PER_GROUP = 4
SSD_D_STATE = 128
GLA_N_HEADS = 4
GLA_GATE_RANK = 16
GLA_GATE_TAU = 16.0

LANES = 128
ROW_ALIGN = 256
FRONT_PAD = ROW_ALIGN - N_META
SSD_CHUNK = 128
GLA_CHUNK = 256
ROW_TILE = 1024
PROLOGUE_ROWS = 128
FFN_COL_CHUNK = 256
HALO = 16
VMEM_LIMIT = 56 * 1024 * 1024
NEG_INF = float("-inf")


def _cparams(*sem):
    return pltpu.CompilerParams(dimension_semantics=sem, vmem_limit_bytes=VMEM_LIMIT)


def _softplus(x):
    return jnp.maximum(x, 0.0) + jnp.log(1.0 + jnp.exp(-jnp.abs(x)))


def _sigmoid(x):
    return 0.5 + 0.5 * jnp.tanh(0.5 * x)


def _silu(x):
    hx = 0.5 * x
    return hx + hx * jnp.tanh(hx)


def _split3(x):
    x1 = x.astype(BF16)
    r1 = x - x1.astype(F32)
    x2 = r1.astype(BF16)
    x3 = (r1 - x2.astype(F32)).astype(BF16)
    return x1, x2, x3


def _dot(a, b):
    return jnp.dot(a, b, preferred_element_type=F32)


def _dot_nt(a, b):
    return lax.dot_general(a, b, (((1,), (1,)), ((), ())), preferred_element_type=F32)


def _dot_tn(a, b):
    return lax.dot_general(a, b, (((0,), (0,)), ((), ())), preferred_element_type=F32)


def _dot_exact_lhs(t, x):
    x1, x2, x3 = _split3(x)
    return _dot(t, x1) + _dot(t, x2) + _dot(t, x3)


def _dot_split2_lhs(t, x):
    x1 = x.astype(BF16)
    x2 = (x - x1.astype(F32)).astype(BF16)
    return _dot(t, x1) + _dot(t, x2)


def _dot_exact_rhs(x, t):
    x1, x2, x3 = _split3(x)
    return _dot(x1, t) + _dot(x2, t) + _dot(x3, t)


def _row_valid(row0, n, ranges):
    r = row0 + lax.broadcasted_iota(jnp.int32, (n, 1), 0)
    ok = None
    for lo, hi in ranges:
        m = (r >= lo) & (r < hi)
        ok = m if ok is None else (ok | m)
    return ok


def _is_any(c, values):
    ok = None
    for v in values:
        m = c == v
        ok = m if ok is None else (ok | m)
    return ok


def _inproj_kernel(h_ref, nw_ref, wb_ref, ws_ref, big_ref, small_ref, xn_scr):
    @pl.when(pl.program_id(1) == 0)
    def _():
        for r0 in range(0, h_ref.shape[0], PROLOGUE_ROWS):
            rs = slice(r0, r0 + PROLOGUE_ROWS)
            x = h_ref[rs, :]
            ms = jnp.mean(x * x, axis=-1, keepdims=True)
            xn_scr[rs, :] = (x * lax.rsqrt(ms + EPS) * nw_ref[...]).astype(BF16)
        small_ref[...] = _dot(xn_scr[...], ws_ref[...])

    big_ref[...] = _dot(xn_scr[...], wb_ref[...]).astype(BF16)


def _inproj(h, norm_w, w_big, w_small, tn=1024):
    rows, d = h.shape
    nb = w_big.shape[1]
    return pl.pallas_call(
        _inproj_kernel,
        grid=(rows // ROW_TILE, nb // tn),
        in_specs=[
            pl.BlockSpec((ROW_TILE, d), lambda i, j: (i, 0)),
            pl.BlockSpec((1, d), lambda i, j: (0, 0)),
            pl.BlockSpec((d, tn), lambda i, j: (0, j)),
            pl.BlockSpec((d, LANES), lambda i, j: (0, 0)),
        ],
        out_specs=[
            pl.BlockSpec((ROW_TILE, tn), lambda i, j: (i, j)),
            pl.BlockSpec((ROW_TILE, LANES), lambda i, j: (i, 0)),
        ],
        out_shape=[
            jax.ShapeDtypeStruct((rows, nb), BF16),
            jax.ShapeDtypeStruct((rows, LANES), F32),
        ],
        scratch_shapes=[pltpu.VMEM((ROW_TILE, d), BF16)],
        compiler_params=_cparams("parallel", "arbitrary"),
        name="inproj",
    )(h, norm_w.reshape(1, d), w_big, w_small)


def _conv3_rows(x, prev_row, next_row, w_ref, b_ref):
    n = x.shape[0]
    row = lax.broadcasted_iota(jnp.int32, (n, 1), 0)
    up = jnp.where(row == 0, prev_row, pltpu.roll(x, 1, 0))
    dn = jnp.where(row == n - 1, next_row, pltpu.roll(x, n - 1, 0))
    y = b_ref[...] + up * w_ref[0:1, :]
    y = y + x * w_ref[1:2, :]
    return y + dn * w_ref[2:3, :]


def _ssd_conv_kernel(x_ref, prev_ref, next_ref, w_ref, b_ref, o_ref):
    x = x_ref[...].astype(F32)
    prev_row = prev_ref[HALO - 1:HALO, :].astype(F32)
    next_row = next_ref[0:1, :].astype(F32)
    o_ref[...] = _silu(_conv3_rows(x, prev_row, next_row, w_ref, b_ref)).astype(BF16)


def _ssd_conv(big, conv_w, conv_b, col0, tr=512, tc=2048):
    rows = big.shape[0]
    c = conv_w.shape[1]
    cb0 = col0 // tc
    hb = tr // HALO
    last = rows // HALO - 1
    return pl.pallas_call(
        _ssd_conv_kernel,
        grid=(rows // tr, c // tc),
        in_specs=[
            pl.BlockSpec((tr, tc), lambda i, j: (i, cb0 + j)),
            pl.BlockSpec((HALO, tc), lambda i, j: (jnp.maximum(i * hb - 1, 0), cb0 + j)),
            pl.BlockSpec((HALO, tc), lambda i, j: (jnp.minimum((i + 1) * hb, last), cb0 + j)),
            pl.BlockSpec((3, tc), lambda i, j: (0, j)),
            pl.BlockSpec((1, tc), lambda i, j: (0, j)),
        ],
        out_specs=pl.BlockSpec((tr, tc), lambda i, j: (i, j)),
        out_shape=jax.ShapeDtypeStruct((rows, c), BF16),
        compiler_params=_cparams("parallel", "parallel"),
        name="ssd_conv",
    )(big, big, big, conv_w, conv_b.reshape(1, c))


def _ssd_scan_kernel(x_ref, b_ref, c_ref, small_ref, dtb_ref, alog_ref, tri_ref, trit_ref,
                     mask_ref, exp_ref, y_ref, state_scr, *, n_chunks, firsts, lasts, valid):
    d = pl.program_id(0)
    g = pl.program_id(1)
    c = g + d * (n_chunks - 1 - 2 * g)
    ch = SSD_CHUNK

    reset = jnp.where(d == 0, _is_any(c, firsts), _is_any(c, lasts))

    @pl.when(reset)
    def _():
        state_scr[...] = jnp.zeros_like(state_scr)

    ok = _row_valid(c * ch, ch, valid)
    dt = _softplus(small_ref[...] + dtb_ref[0])
    dt = jnp.where(ok, dt, 0.0)
    a = -jnp.exp(alog_ref[0]) * dt
    tri = tri_ref[0]
    acs = _dot_exact_lhs(tri, a)
    acs_t = _dot_exact_rhs(a.T, trit_ref[0])
    dt_t = dt.T
    total = jnp.sum(a, axis=0, keepdims=True)
    w_in = jnp.exp(total - acs) * dt
    per_head = jnp.concatenate(
        [w_in, jnp.exp(acs), jnp.broadcast_to(jnp.exp(total), (8, LANES))], axis=0)
    hi = per_head.astype(BF16)
    lo = (per_head - hi.astype(F32)).astype(BF16)
    per_lane = _dot(jnp.concatenate([hi, lo], axis=1), exp_ref[...])
    w_exp = per_lane[0:ch]
    ea_exp = per_lane[ch:2 * ch]
    et_exp = per_lane[2 * ch:2 * ch + 1]
    mask = mask_ref[0]

    gw = SSD_HEADS_PER_GROUP * SSD_HEAD_DIM
    lane = lax.broadcasted_iota(jnp.int32, (1, gw), 1)
    for grp in range(SSD_N_GROUPS):
        bg = b_ref[:, grp * SSD_D_STATE:(grp + 1) * SSD_D_STATE]
        cg = c_ref[:, grp * SSD_D_STATE:(grp + 1) * SSD_D_STATE]
        xg = x_ref[:, grp * gw:(grp + 1) * gw]
        cb = _dot_nt(cg, bg)
        st = state_scr[grp]
        y = _dot(cg, st.astype(BF16)) * ea_exp[:, grp * gw:(grp + 1) * gw]
        ms, xs = [], []
        for hh in range(SSD_HEADS_PER_GROUP):
            h = grp * SSD_HEADS_PER_GROUP + hh
            seg = acs[:, h:h + 1] - acs_t[h:h + 1, :] + mask
            ms.append((cb * jnp.exp(seg) * dt_t[h:h + 1, :]).astype(BF16))
            in_head = (lane >= hh * SSD_HEAD_DIM) & (lane < (hh + 1) * SSD_HEAD_DIM)
            xs.append(jnp.where(in_head, xg, jnp.zeros_like(xg)))
        y = y + _dot(jnp.concatenate(ms, axis=1), jnp.concatenate(xs, axis=0))
        y_ref[0, :, grp * gw:(grp + 1) * gw] = y.astype(BF16)
        xw = (xg.astype(F32) * w_exp[:, grp * gw:(grp + 1) * gw]).astype(BF16)
        state_scr[grp] = st * et_exp[:, grp * gw:(grp + 1) * gw] + _dot_tn(bg, xw)


def _ssd_scan(xbc, small, dt_bias, a_log, layout):
    rows = xbc.shape[0]
    ch = SSD_CHUNK
    d_inner = SSD_N_GROUPS * SSD_HEADS_PER_GROUP * SSD_HEAD_DIM
    gn = SSD_N_GROUPS * SSD_D_STATE
    n_heads = SSD_N_GROUPS * SSD_HEADS_PER_GROUP
    n_chunks = rows // ch
    firsts = tuple(s // ch for s, _ in layout["spans"])
    lasts = tuple((s + n) // ch - 1 for s, n in layout["spans"])

    idx = np.arange(ch)
    lower = (idx[:, None] >= idx[None, :])
    tri = np.stack([lower, lower.T]).astype(np.float32)
    mask = np.where(tri > 0, 0.0, NEG_INF).astype(np.float32)
    expand = np.zeros((2 * LANES, d_inner), np.float32)
    for h in range(n_heads):
        expand[h, h * SSD_HEAD_DIM:(h + 1) * SSD_HEAD_DIM] = 1.0
        expand[LANES + h, h * SSD_HEAD_DIM:(h + 1) * SSD_HEAD_DIM] = 1.0
    pad = ((0, 0), (0, 0), (0, LANES - n_heads))
    dtb = jnp.pad(dt_bias.reshape(2, 1, n_heads), pad)
    alog = jnp.pad(a_log.reshape(2, 1, n_heads), pad)

    def cmap(d, g):
        return g + d * (n_chunks - 1 - 2 * g)

    kern = functools.partial(_ssd_scan_kernel, n_chunks=n_chunks, firsts=firsts, lasts=lasts,
                             valid=layout["valid"])
    return pl.pallas_call(
        kern,
        grid=(2, n_chunks),
        in_specs=[
            pl.BlockSpec((ch, d_inner), lambda d, g: (cmap(d, g), 0)),
            pl.BlockSpec((ch, gn), lambda d, g: (cmap(d, g), d_inner // gn)),
            pl.BlockSpec((ch, gn), lambda d, g: (cmap(d, g), d_inner // gn + 1)),
            pl.BlockSpec((ch, LANES), lambda d, g: (cmap(d, g), 0)),
            pl.BlockSpec((1, 1, LANES), lambda d, g: (d, 0, 0)),
            pl.BlockSpec((1, 1, LANES), lambda d, g: (d, 0, 0)),
            pl.BlockSpec((1, ch, ch), lambda d, g: (d, 0, 0)),
            pl.BlockSpec((1, ch, ch), lambda d, g: (1 - d, 0, 0)),
            pl.BlockSpec((1, ch, ch), lambda d, g: (d, 0, 0)),
            pl.BlockSpec((2 * LANES, d_inner), lambda d, g: (0, 0)),
        ],
        out_specs=pl.BlockSpec((1, ch, d_inner), lambda d, g: (d, cmap(d, g), 0)),
        out_shape=jax.ShapeDtypeStruct((2, rows, d_inner), BF16),
        scratch_shapes=[pltpu.VMEM((SSD_N_GROUPS, SSD_D_STATE, SSD_HEADS_PER_GROUP * SSD_HEAD_DIM), F32)],
        compiler_params=_cparams("arbitrary", "arbitrary"),
        name="ssd_scan",
    )(xbc, xbc, xbc, small, dtb, alog, jnp.asarray(tri, BF16), jnp.asarray(tri, BF16),
      jnp.asarray(mask), jnp.asarray(expand, BF16))


def _gla_levels():
    lv, m = [], GLA_CHUNK // 2
    while m >= 1:
        lv.append(m)
        m //= 2
    return tuple(lv)


def _gla_level_exponent(cs_scr, cs_t, t, m, reverse):
    ch = cs_t.shape[0]
    off = m - 1 if reverse else m

    def ref_tile(r):
        return jnp.broadcast_to(cs_scr[t, pl.ds(r, 1), :], (8, LANES))

    if m >= 8:
        pieces = []
        for p in range(ch // (2 * m)):
            lo = 2 * m * p
            ref = jnp.concatenate([ref_tile(lo + off)] * (m // 8), axis=0)
            early, late = cs_t[lo:lo + m], cs_t[lo + m:lo + 2 * m]
            pieces += [early - ref, ref - late] if reverse else [ref - early, late - ref]
        return jnp.concatenate(pieces, axis=0)

    row = lax.broadcasted_iota(jnp.int32, (ch, 1), 0)
    in_late = ((row // m) & 1) == 1
    is_target = in_late != reverse
    if m == 1:
        x3 = cs_t.reshape(ch // 8, 8, LANES)
        nb = pltpu.roll(x3, 1 if reverse else 7, 1).reshape(ch, LANES)
        return jnp.where(is_target, 0.0, nb - cs_t)
    sub = lax.broadcasted_iota(jnp.int32, (8, 1), 0)
    pieces = []
    for p in range(ch // 8):
        ref = ref_tile(8 * p + off)
        for q in range(1, 8 // (2 * m)):
            ref = jnp.where(sub >= 2 * m * q, ref_tile(8 * p + 2 * m * q + off), ref)
        pieces.append(ref)
    delta = cs_t - jnp.concatenate(pieces, axis=0)
    return jnp.where(is_target, delta, -delta)


def _gla_level_operand(q_scr, k_ref, sl, m, reverse):
    ch = q_scr.shape[0]
    if m >= 16:
        pieces = []
        for p in range(ch // (2 * m)):
            lo = 2 * m * p
            early, late = slice(lo, lo + m), slice(lo + m, lo + 2 * m)
            if reverse:
                pieces += [q_scr[early, sl], k_ref[late, sl]]
            else:
                pieces += [k_ref[early, sl], q_scr[late, sl]]
        return jnp.concatenate(pieces, axis=0)
    row = lax.broadcasted_iota(jnp.int32, (ch, 1), 0)
    is_target = (((row // m) & 1) == 1) != reverse
    return jnp.where(is_target, q_scr[:, sl], k_ref[:, sl])


def _gla_scan_kernel(q_ref, k_ref, v_ref, small_ref, gu_ref, gb_ref, tri_ref, lev_ref, o_ref,
                     state_scr, cs_scr, att_scr, q_scr, *, reverse, n_chunks, firsts, lasts, valid,
                     dk, dv):
    g = pl.program_id(0)
    c = n_chunks - 1 - g if reverse else g
    ch = GLA_CHUNK
    levels = _gla_levels()
    tiles_per_head = dk // LANES

    @pl.when(_is_any(c, lasts if reverse else firsts))
    def _():
        state_scr[...] = jnp.zeros_like(state_scr)

    ok = _row_valid(c * ch, ch, valid)
    u = _dot(small_ref[...].astype(BF16), gu_ref[...]) + gb_ref[...]
    gate = jnp.where(ok, -_softplus(-u) / GLA_GATE_TAU, 0.0)
    cs = _dot_split2_lhs(tri_ref[...], gate)
    n_tiles = cs.shape[1] // LANES
    for t in range(n_tiles):
        cs_scr[t] = cs[:, t * LANES:(t + 1) * LANES]
    total = jnp.sum(gate, axis=0, keepdims=True)
    lev = lev_ref[...]
    q_scr[...] = q_ref[...] * jnp.asarray(dk ** -0.5, BF16)

    half = ch // 2
    tgt, src = (slice(0, half), slice(half, ch)) if reverse else (slice(half, ch), slice(0, half))
    order = (1, 0) + tuple(range(2, len(levels)))
    for i in order:
        m = levels[i]
        e = [jnp.exp(_gla_level_exponent(cs_scr, cs[:, t * LANES:(t + 1) * LANES], t, m, reverse)
                     ).astype(BF16) for t in range(n_tiles)]
        for h in range(GLA_N_HEADS):
            sl = slice(h * dk, (h + 1) * dk)
            e_h = jnp.concatenate(e[h * tiles_per_head:(h + 1) * tiles_per_head], axis=1)
            x = _gla_level_operand(q_scr, k_ref, sl, m, reverse) * e_h
            if i == 0:
                att_scr[h, tgt, src] = _dot_nt(x[tgt], x[src])
            elif i == 1:
                att_scr[h] = jnp.where(lev == i, _dot_nt(x, x), 0.0)
            else:
                att_scr[h] = jnp.where(lev == i, _dot_nt(x, x), att_scr[h])

    for h in range(GLA_N_HEADS):
        sl = slice(h * dk, (h + 1) * dk)
        vs = slice(h * dv, (h + 1) * dv)
        vh = v_ref[:, vs]
        diag = _dot_nt(q_scr[:, sl], k_ref[:, sl])
        att = jnp.where(lev == len(levels), diag, att_scr[h]).astype(BF16)
        st = state_scr[h]
        qd = q_scr[:, sl] * jnp.exp(cs[:, sl]).astype(BF16)
        o = _dot(att, vh) + _dot_nt(qd, st.astype(BF16))
        o_ref[:, vs] = o.astype(BF16)
        kd = k_ref[:, sl] * jnp.exp(total[:, sl] - cs[:, sl]).astype(BF16)
        state_scr[h] = st * jnp.exp(total[:, sl]) + _dot_tn(vh, kd)


def _gla_scan(big, small, gate_up, gate_b, layout, col_q, col_k, col_v, dk, dv, reverse):
    rows = big.shape[0]
    ch = GLA_CHUNK
    n_chunks = rows // ch
    firsts = tuple(s // ch for s, _ in layout["spans"])
    lasts = tuple((s + n) // ch - 1 for s, n in layout["spans"])
    hk = GLA_N_HEADS * dk
    hv = GLA_N_HEADS * dv
    levels = _gla_levels()

    idx = np.arange(ch)
    l_, s_ = idx[:, None], idx[None, :]
    seen = (l_ <= s_) if reverse else (l_ >= s_)
    x = l_ ^ s_
    top = np.floor(np.log2(np.maximum(x, 1))).astype(np.int64)
    lvl = np.where(x == 0, len(levels), (len(levels) - 1) - top)
    lev = np.where(seen, lvl, -1).astype(np.int32)

    lo = 32 + GLA_GATE_RANK * int(reverse)
    gu = jnp.zeros((LANES, hk), F32).at[lo:lo + GLA_GATE_RANK].set(gate_up).astype(BF16)

    def cmap(g):
        return n_chunks - 1 - g if reverse else g

    kern = functools.partial(_gla_scan_kernel, reverse=reverse, n_chunks=n_chunks, firsts=firsts,
                             lasts=lasts, valid=layout["valid"], dk=dk, dv=dv)
    return pl.pallas_call(
        kern,
        grid=(n_chunks,),
        in_specs=[
            pl.BlockSpec((ch, hk), lambda g: (cmap(g), col_q // hk)),
            pl.BlockSpec((ch, hk), lambda g: (cmap(g), col_k // hk)),
            pl.BlockSpec((ch, hv), lambda g: (cmap(g), col_v // hv)),
            pl.BlockSpec((ch, LANES), lambda g: (cmap(g), 0)),
            pl.BlockSpec((LANES, hk), lambda g: (0, 0)),
            pl.BlockSpec((1, hk), lambda g: (0, 0)),
            pl.BlockSpec((ch, ch), lambda g: (0, 0)),
            pl.BlockSpec((ch, ch), lambda g: (0, 0)),
        ],
        out_specs=pl.BlockSpec((ch, hv), lambda g: (cmap(g), 0)),
        out_shape=jax.ShapeDtypeStruct((rows, hv), BF16),
        scratch_shapes=[
            pltpu.VMEM((GLA_N_HEADS, dv, dk), F32),
            pltpu.VMEM((hk // LANES, ch, LANES), F32),
            pltpu.VMEM((GLA_N_HEADS, ch, ch), F32),
            pltpu.VMEM((ch, hk), BF16),
        ],
        compiler_params=_cparams("arbitrary"),
        name="gla_scan_bwd" if reverse else "gla_scan_fwd",
    )(big, big, big, small, gu, gate_b.reshape(1, hk), jnp.asarray(seen.astype(np.float32), BF16),
      jnp.asarray(lev))


def _merge_kernel(ys_ref, xs_ref, z_ref, ogf_ref, ogb_ref, ogate_ref, ms_ref, mg_ref, dskip_ref,
                  snw_ref, gnw_ref, ws_ref, wg_ref, o_ref, a_ssd, a_gla, *, dv):
    s = pl.program_id(0)
    slot = s % 2
    tm = a_ssd.shape[1]

    @pl.when(s == 0)
    def _():
        a_ssd[1] = jnp.zeros(a_ssd.shape[1:], BF16)
        a_gla[1] = jnp.zeros(a_gla.shape[1:], BF16)

    for r0 in range(0, tm, PROLOGUE_ROWS):
        rs = slice(r0, r0 + PROLOGUE_ROWS)
        y = ys_ref[0, rs, :].astype(F32) + ys_ref[1, rs, :].astype(F32)
        y = y + xs_ref[rs, :].astype(F32) * dskip_ref[...]
        y = y * _silu(z_ref[rs, :].astype(F32))
        ms = jnp.mean(y * y, axis=-1, keepdims=True)
        a_ssd[slot, rs, :] = (y * lax.rsqrt(ms + EPS) * snw_ref[...]).astype(BF16)
        for h in range(GLA_N_HEADS):
            sl = slice(h * dv, (h + 1) * dv)
            o = ogf_ref[rs, sl].astype(F32) + ogb_ref[rs, sl].astype(F32)
            ms = jnp.mean(o * o, axis=-1, keepdims=True)
            o = o * lax.rsqrt(ms + EPS) * gnw_ref[...]
            a_gla[slot, rs, sl] = (o * _silu(ogate_ref[rs, sl].astype(F32))).astype(BF16)

    y_ssd = _dot(a_ssd[1 - slot], ws_ref[...])
    y_gla = _dot(a_gla[1 - slot], wg_ref[...])
    merged = _sigmoid(ms_ref[...].astype(F32)) * y_ssd
    merged = merged + _sigmoid(mg_ref[...].astype(F32)) * y_gla
    o_ref[...] = merged.astype(BF16)


def _merge(y_ssd, xbc, big, o_fwd, o_bwd, d_skip, ssd_norm_w, gla_norm_w, w_ssd, w_gla, col_og, col_merge,
           tm=256):
    rows, d = xbc.shape[0], w_ssd.shape[1]
    di = w_ssd.shape[0]
    dvt = w_gla.shape[0]
    dv = dvt // GLA_N_HEADS
    n_tiles = rows // tm
    kern = functools.partial(_merge_kernel, dv=dv)

    def cur(s):
        return jnp.minimum(s, n_tiles - 1)

    def prev(s):
        return jnp.maximum(s - 1, 0)

    resident = pl.Buffered(1)
    return pl.pallas_call(
        kern,
        grid=(n_tiles + 1,),
        in_specs=[
            pl.BlockSpec((2, tm, di), lambda s: (0, cur(s), 0)),
            pl.BlockSpec((tm, di), lambda s: (cur(s), 0)),
            pl.BlockSpec((tm, di), lambda s: (cur(s), 0)),
            pl.BlockSpec((tm, dvt), lambda s: (cur(s), 0)),
            pl.BlockSpec((tm, dvt), lambda s: (cur(s), 0)),
            pl.BlockSpec((tm, dvt), lambda s: (cur(s), col_og // dvt)),
            pl.BlockSpec((tm, d), lambda s: (prev(s), col_merge // d)),
            pl.BlockSpec((tm, d), lambda s: (prev(s), col_merge // d + 1)),
            pl.BlockSpec((1, di), lambda s: (0, 0)),
            pl.BlockSpec((1, di), lambda s: (0, 0)),
            pl.BlockSpec((1, dv), lambda s: (0, 0)),
            pl.BlockSpec((di, d), lambda s: (0, 0), pipeline_mode=resident),
            pl.BlockSpec((dvt, d), lambda s: (0, 0), pipeline_mode=resident),
        ],
        out_specs=pl.BlockSpec((tm, d), lambda s: (prev(s), 0)),
        out_shape=jax.ShapeDtypeStruct((rows, d), BF16),
        scratch_shapes=[pltpu.VMEM((2, tm, di), BF16), pltpu.VMEM((2, tm, dvt), BF16)],
        compiler_params=_cparams("arbitrary"),
        name="merge",
    )(y_ssd, xbc, big, o_fwd, o_bwd, big, big, big,
      jnp.repeat(d_skip, SSD_HEAD_DIM).reshape(1, di), ssd_norm_w.reshape(1, di),
      gla_norm_w.reshape(1, dv), w_ssd, w_gla)


def _mm_res_kernel(a_ref, w_ref, h_ref, o_ref):
    o_ref[...] = h_ref[...] + _dot(a_ref[...], w_ref[...])


def _mm_res(a, w, h, n_split, tm=512):
    rows, k = a.shape
    n = w.shape[1]
    tn = n // n_split
    return pl.pallas_call(
        _mm_res_kernel,
        grid=(n_split, rows // tm),
        in_specs=[
            pl.BlockSpec((tm, k), lambda j, i: (i, 0)),
            pl.BlockSpec((k, tn), lambda j, i: (0, j)),
            pl.BlockSpec((tm, tn), lambda j, i: (i, j)),
        ],
        out_specs=pl.BlockSpec((tm, tn), lambda j, i: (i, j)),
        out_shape=jax.ShapeDtypeStruct((rows, n), F32),
        compiler_params=_cparams("arbitrary", "arbitrary"),
        name="mm_res",
    )(a, w, h)


def _ffn_up_kernel(h_ref, prev_ref, next_ref, nw_ref, wg_ref, wu_ref, cwg_ref, cwu_ref, cbg_ref,
                   cbu_ref, o_ref, xn_scr, *, valid):
    tm = h_ref.shape[0]

    def norm(x):
        ms = jnp.mean(x * x, axis=-1, keepdims=True)
        return (x * lax.rsqrt(ms + EPS) * nw_ref[...]).astype(BF16)

    @pl.when(pl.program_id(1) == 0)
    def _():
        xn_scr[0:HALO, :] = norm(prev_ref[...])
        for r0 in range(0, tm, PROLOGUE_ROWS):
            xn_scr[HALO + r0:HALO + r0 + PROLOGUE_ROWS, :] = norm(h_ref[r0:r0 + PROLOGUE_ROWS, :])
        xn_scr[HALO + tm:2 * HALO + tm, :] = norm(next_ref[...])

    xn = xn_scr[...]
    n = tm + 2 * HALO

    def conv(u, w_ref, b_ref, cs):
        y = b_ref[:, cs] + pltpu.roll(u, 1, 0)[HALO:HALO + tm] * w_ref[0:1, cs]
        y = y + u[HALO:HALO + tm] * w_ref[1:2, cs]
        return y + pltpu.roll(u, n - 1, 0)[HALO:HALO + tm] * w_ref[2:3, cs]

    ok = _row_valid(pl.program_id(0) * tm, tm, valid)
    for c0 in range(0, o_ref.shape[1], FFN_COL_CHUNK):
        cs = slice(c0, c0 + FFN_COL_CHUNK)
        gate = conv(_dot(xn, wg_ref[:, cs]), cwg_ref, cbg_ref, cs)
        up = conv(_dot(xn, wu_ref[:, cs]), cwu_ref, cbu_ref, cs)
        o_ref[:, cs] = jnp.where(ok, _silu(gate) * up, 0.0).astype(BF16)


def _ffn_up(h, norm_w, w_up, conv_w, conv_b, layout, tf=512):
    rows, d = h.shape
    dff = w_up.shape[1] // 2
    tm = ROW_TILE
    hb = tm // HALO
    last = rows // HALO - 1
    nj = dff // tf
    kern = functools.partial(_ffn_up_kernel, valid=layout["valid"])
    return pl.pallas_call(
        kern,
        grid=(rows // tm, nj),
        in_specs=[
            pl.BlockSpec((tm, d), lambda i, j: (i, 0)),
            pl.BlockSpec((HALO, d), lambda i, j: (jnp.maximum(i * hb - 1, 0), 0)),
            pl.BlockSpec((HALO, d), lambda i, j: (jnp.minimum((i + 1) * hb, last), 0)),
            pl.BlockSpec((1, d), lambda i, j: (0, 0)),
            pl.BlockSpec((d, tf), lambda i, j: (0, j)),
            pl.BlockSpec((d, tf), lambda i, j: (0, nj + j)),
            pl.BlockSpec((3, tf), lambda i, j: (0, j)),
            pl.BlockSpec((3, tf), lambda i, j: (0, nj + j)),
            pl.BlockSpec((1, tf), lambda i, j: (0, j)),
            pl.BlockSpec((1, tf), lambda i, j: (0, nj + j)),
        ],
        out_specs=pl.BlockSpec((tm, tf), lambda i, j: (i, j)),
        out_shape=jax.ShapeDtypeStruct((rows, dff), BF16),
        scratch_shapes=[pltpu.VMEM((tm + 2 * HALO, d), BF16)],
        compiler_params=_cparams("parallel", "arbitrary"),
        name="ffn_up",
    )(h, h, h, norm_w.reshape(1, d), w_up, w_up, conv_w, conv_w,
      conv_b.reshape(1, 2 * dff), conv_b.reshape(1, 2 * dff))


def _final_norm_kernel(h_ref, nw_ref, o_ref):
    x = h_ref[...]
    ms = jnp.mean(x * x, axis=-1, keepdims=True)
    o_ref[0] = x * lax.rsqrt(ms + EPS) * nw_ref[...]


def _final_norm(h, norm_w, first_row, batch, length, tr=ROW_ALIGN):
    d = h.shape[1]
    per_seq = (ROW_ALIGN + length) // tr
    b0 = (first_row + ROW_ALIGN) // tr
    return pl.pallas_call(
        _final_norm_kernel,
        grid=(batch, length // tr),
        in_specs=[
            pl.BlockSpec((tr, d), lambda b, t: (b0 + b * per_seq + t, 0)),
            pl.BlockSpec((1, d), lambda b, t: (0, 0)),
        ],
        out_specs=pl.BlockSpec((1, tr, d), lambda b, t: (b, t, 0)),
        out_shape=jax.ShapeDtypeStruct((batch, length, d), F32),
        compiler_params=_cparams("parallel", "parallel"),
        name="final_norm",
    )(h, norm_w.reshape(1, d))


def _make_layout(groups):
    seqs, valid, row = [], [], 0
    for batch, length in groups:
        assert length % ROW_ALIGN == 0
        for _ in range(batch):
            n = ROW_ALIGN + length
            seqs.append((row, n))
            valid.append((row + FRONT_PAD, row + n))
            row += n
    rows = (row // ROW_TILE + 1) * ROW_TILE
    spans = tuple(seqs) + ((row, rows - row),)
    return dict(seqs=tuple(seqs), spans=spans, valid=tuple(valid), used_rows=row, rows=rows)


def kernel(x_prompt, x_sample, meta_tokens, mix_norm_w, w_in, ssd_conv_w, ssd_conv_b, ssd_dt_bias,
           ssd_a_log, ssd_d, ssd_norm_w, ssd_w_out, gla_gate_up, gla_gate_b, gla_norm_w, gla_w_out,
           w_mix_out, ffn_norm_w, ffn_w_up, ffn_conv_w, ffn_conv_b, ffn_w_down, final_norm_w):
    depth, d = mix_norm_w.shape
    groups = [(x_prompt.shape[0], x_prompt.shape[1]), (x_sample.shape[0], x_sample.shape[1])]
    layout = _make_layout(groups)
    rows = layout["rows"]

    n_heads = ssd_dt_bias.shape[-1]
    d_inner = n_heads * SSD_HEAD_DIM
    conv_dim = ssd_conv_w.shape[-1]
    d_k = gla_gate_up.shape[-1]
    d_v = gla_w_out.shape[1]
    dk, dv = d_k // GLA_N_HEADS, d_v // GLA_N_HEADS
    sizes = (d_inner, conv_dim, n_heads, d_k, d_k, d_v, d_v, 2 * GLA_GATE_RANK, 2 * d)
    off = np.concatenate([[0], np.cumsum(sizes)])
    assert off[-1] == w_in.shape[-1]
    wide = [0, 1, 3, 4, 5, 6, 8]
    col, pos = {}, 0
    for s in wide:
        col[s] = pos
        pos += sizes[s]

    zero_front = jnp.zeros((FRONT_PAD, d), F32)
    pieces = []
    for x in (x_prompt, x_sample):
        for b in range(x.shape[0]):
            pieces += [zero_front, meta_tokens.astype(F32), x[b]]
    if rows > layout["used_rows"]:
        pieces.append(jnp.zeros((rows - layout["used_rows"], d), F32))
    h = jnp.concatenate(pieces, axis=0)

    for i in range(depth):
        w = w_in[i]
        w_big = jnp.concatenate([w[:, off[s]:off[s + 1]] for s in wide], axis=1).astype(BF16)
        w_small = jnp.concatenate(
            [w[:, off[2]:off[3]], w[:, off[7]:off[8]],
             jnp.zeros((d, LANES - n_heads - 2 * GLA_GATE_RANK), F32)], axis=1).astype(BF16)

        big, small = _inproj(h, mix_norm_w[i], w_big, w_small)
        xbc = _ssd_conv(big, ssd_conv_w[i], ssd_conv_b[i], col[1])
        y_ssd = _ssd_scan(xbc, small, ssd_dt_bias[i], ssd_a_log[i], layout)
        o_fwd, o_bwd = (_gla_scan(big, small, gla_gate_up[i, r], gla_gate_b[i, r], layout,
                                  col[3], col[4], col[5], dk, dv, reverse=bool(r)) for r in (0, 1))
        merged = _merge(y_ssd, xbc, big, o_fwd, o_bwd, ssd_d[i], ssd_norm_w[i], gla_norm_w[i],
                        ssd_w_out[i].astype(BF16), gla_w_out[i].astype(BF16), col[6], col[8])
        h = _mm_res(merged, w_mix_out[i].astype(BF16), h, n_split=1)
        act = _ffn_up(h, ffn_norm_w[i], ffn_w_up[i].astype(BF16), ffn_conv_w[i], ffn_conv_b[i],
                      layout)
        h = _mm_res(act, ffn_w_down[i].astype(BF16), h, n_split=2)

    outs, si = [], 0
    for batch, length in groups:
        outs.append(_final_norm(h, final_norm_w, layout["seqs"][si][0], batch, length))
        si += batch
    return tuple(outs)
```

```python
import functools

import numpy as np
import jax
import jax.numpy as jnp
from jax import lax
from jax.experimental import pallas as pl
from jax.experimental.pallas import tpu as pltpu

F32 = jnp.float32
BF16 = jnp.bfloat16

N_META = 16
EPS = 1e-6
SSD_HEAD_DIM = 64
SSD_N_GROUPS = 8
SSD_HEADS_PER_GROUP = 4
SSD_D_STATE = 128
GLA_N_HEADS = 4
GLA_GATE_RANK = 16
GLA_GATE_TAU = 16.0

LANES = 128
ROW_ALIGN = 256
FRONT_PAD = ROW_ALIGN - N_META
SSD_CHUNK = 128
GLA_CHUNK = 256
ROW_TILE = 1024
PROLOGUE_ROWS = 128
FFN_COL_CHUNK = 256
HALO = 16
VMEM_LIMIT = 56 * 1024 * 1024
NEG_INF = float("-inf")


def _cparams(*sem):
    return pltpu.CompilerParams(dimension_semantics=sem, vmem_limit_bytes=VMEM_LIMIT)


def _softplus(x):
    return jnp.maximum(x, 0.0) + jnp.log(1.0 + jnp.exp(-jnp.abs(x)))


def _sigmoid(x):
    return 0.5 + 0.5 * jnp.tanh(0.5 * x)


def _silu(x):
    hx = 0.5 * x
    return hx + hx * jnp.tanh(hx)


def _split3(x):
    x1 = x.astype(BF16)
    r1 = x - x1.astype(F32)
    x2 = r1.astype(BF16)
    x3 = (r1 - x2.astype(F32)).astype(BF16)
    return x1, x2, x3


def _dot(a, b):
    return jnp.dot(a, b, preferred_element_type=F32)


def _dot_nt(a, b):
    return lax.dot_general(a, b, (((1,), (1,)), ((), ())), preferred_element_type=F32)


def _dot_tn(a, b):
    return lax.dot_general(a, b, (((0,), (0,)), ((), ())), preferred_element_type=F32)


def _dot_exact_lhs(t, x):
    x1, x2, x3 = _split3(x)
    return _dot(t, x1) + _dot(t, x2) + _dot(t, x3)


def _dot_split2_lhs(t, x):
    x1 = x.astype(BF16)
    x2 = (x - x1.astype(F32)).astype(BF16)
    return _dot(t, x1) + _dot(t, x2)


def _dot_exact_rhs(x, t):
    x1, x2, x3 = _split3(x)
    return _dot(x1, t) + _dot(x2, t) + _dot(x3, t)


def _row_valid(row0, n, ranges):
    r = row0 + lax.broadcasted_iota(jnp.int32, (n, 1), 0)
    ok = None
    for lo, hi in ranges:
        m = (r >= lo) & (r < hi)
        ok = m if ok is None else (ok | m)
    return ok


def _is_any(c, values):
    ok = None
    for v in values:
        m = c == v
        ok = m if ok is None else (ok | m)
    return ok


def _inproj_kernel(h_ref, nw_ref, wb_ref, ws_ref, big_ref, small_ref, xn_scr):
    @pl.when(pl.program_id(1) == 0)
    def _():
        for r0 in range(0, h_ref.shape[0], PROLOGUE_ROWS):
            rs = slice(r0, r0 + PROLOGUE_ROWS)
            x = h_ref[rs, :]
            ms = jnp.mean(x * x, axis=-1, keepdims=True)
            xn_scr[rs, :] = (x * lax.rsqrt(ms + EPS) * nw_ref[...]).astype(BF16)
        small_ref[...] = _dot(xn_scr[...], ws_ref[...])

    big_ref[...] = _dot(xn_scr[...], wb_ref[...]).astype(BF16)


def _inproj(h, norm_w, w_big, w_small, tn=2048):
    rows, d = h.shape
    nb = w_big.shape[1]
    return pl.pallas_call(
        _inproj_kernel,
        grid=(rows // ROW_TILE, nb // tn),
        in_specs=[
            pl.BlockSpec((ROW_TILE, d), lambda i, j: (i, 0)),
            pl.BlockSpec((1, d), lambda i, j: (0, 0)),
            pl.BlockSpec((d, tn), lambda i, j: (0, j)),
            pl.BlockSpec((d, LANES), lambda i, j: (0, 0)),
        ],
        out_specs=[
            pl.BlockSpec((ROW_TILE, tn), lambda i, j: (i, j)),
            pl.BlockSpec((ROW_TILE, LANES), lambda i, j: (i, 0)),
        ],
        out_shape=[
            jax.ShapeDtypeStruct((rows, nb), BF16),
            jax.ShapeDtypeStruct((rows, LANES), F32),
        ],
        scratch_shapes=[pltpu.VMEM((ROW_TILE, d), BF16)],
        compiler_params=_cparams("parallel", "arbitrary"),
        name="inproj",
    )(h, norm_w.reshape(1, d), w_big, w_small)


def _conv3_rows(x, prev_row, next_row, w_ref, b_ref):
    n = x.shape[0]
    row = lax.broadcasted_iota(jnp.int32, (n, 1), 0)
    up = jnp.where(row == 0, prev_row, pltpu.roll(x, 1, 0))
    dn = jnp.where(row == n - 1, next_row, pltpu.roll(x, n - 1, 0))
    y = b_ref[...] + up * w_ref[0:1, :]
    y = y + x * w_ref[1:2, :]
    return y + dn * w_ref[2:3, :]


def _ssd_conv_kernel(x_ref, prev_ref, next_ref, w_ref, b_ref, o_ref):
    x = x_ref[...].astype(F32)
    prev_row = prev_ref[HALO - 1:HALO, :].astype(F32)
    next_row = next_ref[0:1, :].astype(F32)
    o_ref[...] = _silu(_conv3_rows(x, prev_row, next_row, w_ref, b_ref)).astype(BF16)


def _ssd_conv(big, conv_w, conv_b, col0, tr=512, tc=2048):
    rows = big.shape[0]
    c = conv_w.shape[1]
    cb0 = col0 // tc
    hb = tr // HALO
    last = rows // HALO - 1
    return pl.pallas_call(
        _ssd_conv_kernel,
        grid=(rows // tr, c // tc),
        in_specs=[
            pl.BlockSpec((tr, tc), lambda i, j: (i, cb0 + j)),
            pl.BlockSpec((HALO, tc), lambda i, j: (jnp.maximum(i * hb - 1, 0), cb0 + j)),
            pl.BlockSpec((HALO, tc), lambda i, j: (jnp.minimum((i + 1) * hb, last), cb0 + j)),
            pl.BlockSpec((3, tc), lambda i, j: (0, j)),
            pl.BlockSpec((1, tc), lambda i, j: (0, j)),
        ],
        out_specs=pl.BlockSpec((tr, tc), lambda i, j: (i, j)),
        out_shape=jax.ShapeDtypeStruct((rows, c), BF16),
        compiler_params=_cparams("parallel", "parallel"),
        name="ssd_conv",
    )(big, big, big, conv_w, conv_b.reshape(1, c))


def _ssd_chunk(d, c, x_ref, b_ref, c_ref, small_ref, dtb_ref, alog_ref, tri_ref, mask_ref, exp_ref,
               y_ref, state_scr, valid):
    ch = SSD_CHUNK
    ok = _row_valid(c * ch, ch, valid)
    dt = _softplus(small_ref[...] + dtb_ref[d])
    dt = jnp.where(ok, dt, 0.0)
    a = -jnp.exp(alog_ref[d]) * dt
    acs = _dot_exact_lhs(tri_ref[d], a)
    acs_t = _dot_exact_rhs(a.T, tri_ref[1 - d])
    dt_t = dt.T
    total = jnp.sum(a, axis=0, keepdims=True)
    w_in = jnp.exp(total - acs) * dt
    per_head = jnp.concatenate(
        [w_in, jnp.exp(acs), jnp.broadcast_to(jnp.exp(total), (8, LANES))], axis=0)
    hi = per_head.astype(BF16)
    lo = (per_head - hi.astype(F32)).astype(BF16)
    per_lane = _dot(jnp.concatenate([hi, lo], axis=1), exp_ref[...])
    w_exp = per_lane[0:ch]
    ea_exp = per_lane[ch:2 * ch]
    et_exp = per_lane[2 * ch:2 * ch + 1]
    mask = mask_ref[d]

    gw = SSD_HEADS_PER_GROUP * SSD_HEAD_DIM
    lane = lax.broadcasted_iota(jnp.int32, (1, gw), 1)
    for grp in range(SSD_N_GROUPS):
        bg = b_ref[:, grp * SSD_D_STATE:(grp + 1) * SSD_D_STATE]
        cg = c_ref[:, grp * SSD_D_STATE:(grp + 1) * SSD_D_STATE]
        xg = x_ref[:, grp * gw:(grp + 1) * gw]
        cb = _dot_nt(cg, bg)
        st = state_scr[d, grp]
        y = _dot(cg, st.astype(BF16)) * ea_exp[:, grp * gw:(grp + 1) * gw]
        ms, xs = [], []
        for hh in range(SSD_HEADS_PER_GROUP):
            h = grp * SSD_HEADS_PER_GROUP + hh
            seg = acs[:, h:h + 1] - acs_t[h:h + 1, :] + mask
            ms.append((cb * jnp.exp(seg) * dt_t[h:h + 1, :]).astype(BF16))
            in_head = (lane >= hh * SSD_HEAD_DIM) & (lane < (hh + 1) * SSD_HEAD_DIM)
            xs.append(jnp.where(in_head, xg, jnp.zeros_like(xg)))
        y = y + _dot(jnp.concatenate(ms, axis=1), jnp.concatenate(xs, axis=0))
        y_ref[:, grp * gw:(grp + 1) * gw] = y.astype(BF16)
        xw = (xg.astype(F32) * w_exp[:, grp * gw:(grp + 1) * gw]).astype(BF16)
        state_scr[d, grp] = st * et_exp[:, grp * gw:(grp + 1) * gw] + _dot_tn(bg, xw)


def _ssd_scan_kernel(xf_ref, bf_ref, cf_ref, sf_ref, xb_ref, bb_ref, cb_ref, sb_ref, dtb_ref,
                     alog_ref, tri_ref, mask_ref, exp_ref, yf_ref, yb_ref, state_scr, *, n_chunks,
                     firsts, lasts, valid):
    g = pl.program_id(0)
    r = n_chunks - 1 - g

    @pl.when(_is_any(g, firsts))
    def _():
        state_scr[0] = jnp.zeros(state_scr.shape[1:], F32)

    @pl.when(_is_any(r, lasts))
    def _():
        state_scr[1] = jnp.zeros(state_scr.shape[1:], F32)

    shared = (dtb_ref, alog_ref, tri_ref, mask_ref, exp_ref)
    _ssd_chunk(0, g, xf_ref, bf_ref, cf_ref, sf_ref, *shared, yf_ref, state_scr, valid)
    _ssd_chunk(1, r, xb_ref, bb_ref, cb_ref, sb_ref, *shared, yb_ref, state_scr, valid)


def _ssd_scan(xbc, small, dt_bias, a_log, layout):
    rows = xbc.shape[0]
    ch = SSD_CHUNK
    d_inner = SSD_N_GROUPS * SSD_HEADS_PER_GROUP * SSD_HEAD_DIM
    gn = SSD_N_GROUPS * SSD_D_STATE
    n_heads = SSD_N_GROUPS * SSD_HEADS_PER_GROUP
    n_chunks = rows // ch
    firsts = tuple(s // ch for s, _ in layout["spans"])
    lasts = tuple((s + n) // ch - 1 for s, n in layout["spans"])

    idx = np.arange(ch)
    lower = (idx[:, None] >= idx[None, :])
    tri = np.stack([lower, lower.T]).astype(np.float32)
    mask = np.where(tri > 0, 0.0, NEG_INF).astype(np.float32)
    expand = np.zeros((2 * LANES, d_inner), np.float32)
    for h in range(n_heads):
        expand[h, h * SSD_HEAD_DIM:(h + 1) * SSD_HEAD_DIM] = 1.0
        expand[LANES + h, h * SSD_HEAD_DIM:(h + 1) * SSD_HEAD_DIM] = 1.0
    pad = ((0, 0), (0, 0), (0, LANES - n_heads))
    dtb = jnp.pad(dt_bias.reshape(2, 1, n_heads), pad)
    alog = jnp.pad(a_log.reshape(2, 1, n_heads), pad)

    def chunk_specs(cmap):
        return [
            pl.BlockSpec((ch, d_inner), lambda g: (cmap(g), 0)),
            pl.BlockSpec((ch, gn), lambda g: (cmap(g), d_inner // gn)),
            pl.BlockSpec((ch, gn), lambda g: (cmap(g), d_inner // gn + 1)),
            pl.BlockSpec((ch, LANES), lambda g: (cmap(g), 0)),
        ]

    def fwd(g):
        return g

    def rev(g):
        return n_chunks - 1 - g

    kern = functools.partial(_ssd_scan_kernel, n_chunks=n_chunks, firsts=firsts, lasts=lasts,
                             valid=layout["valid"])
    return pl.pallas_call(
        kern,
        grid=(n_chunks,),
        in_specs=chunk_specs(fwd) + chunk_specs(rev) + [
            pl.BlockSpec((2, 1, LANES), lambda g: (0, 0, 0)),
            pl.BlockSpec((2, 1, LANES), lambda g: (0, 0, 0)),
            pl.BlockSpec((2, ch, ch), lambda g: (0, 0, 0)),
            pl.BlockSpec((2, ch, ch), lambda g: (0, 0, 0)),
            pl.BlockSpec((2 * LANES, d_inner), lambda g: (0, 0)),
        ],
        out_specs=[pl.BlockSpec((ch, d_inner), lambda g: (fwd(g), 0)),
                   pl.BlockSpec((ch, d_inner), lambda g: (rev(g), 0))],
        out_shape=[jax.ShapeDtypeStruct((rows, d_inner), BF16)] * 2,
        scratch_shapes=[
            pltpu.VMEM((2, SSD_N_GROUPS, SSD_D_STATE, SSD_HEADS_PER_GROUP * SSD_HEAD_DIM), F32)],
        compiler_params=_cparams("arbitrary"),
        name="ssd_scan",
    )(xbc, xbc, xbc, small, xbc, xbc, xbc, small, dtb, alog, jnp.asarray(tri, BF16),
      jnp.asarray(mask), jnp.asarray(expand, BF16))


def _gla_levels():
    lv, m = [], GLA_CHUNK // 2
    while m >= 1:
        lv.append(m)
        m //= 2
    return tuple(lv)


def _gla_level_exponent(cs_scr, cs_t, t, m, reverse):
    ch = cs_t.shape[0]
    off = m - 1 if reverse else m

    def ref_tile(r):
        return jnp.broadcast_to(cs_scr[t, pl.ds(r, 1), :], (8, LANES))

    if m >= 8:
        pieces = []
        for p in range(ch // (2 * m)):
            lo = 2 * m * p
            ref = jnp.concatenate([ref_tile(lo + off)] * (m // 8), axis=0)
            early, late = cs_t[lo:lo + m], cs_t[lo + m:lo + 2 * m]
            pieces += [early - ref, ref - late] if reverse else [ref - early, late - ref]
        return jnp.concatenate(pieces, axis=0)

    row = lax.broadcasted_iota(jnp.int32, (ch, 1), 0)
    in_late = ((row // m) & 1) == 1
    is_target = in_late != reverse
    if m == 1:
        x3 = cs_t.reshape(ch // 8, 8, LANES)
        nb = pltpu.roll(x3, 1 if reverse else 7, 1).reshape(ch, LANES)
        return jnp.where(is_target, 0.0, nb - cs_t)
    sub = lax.broadcasted_iota(jnp.int32, (8, 1), 0)
    pieces = []
    for p in range(ch // 8):
        ref = ref_tile(8 * p + off)
        for q in range(1, 8 // (2 * m)):
            ref = jnp.where(sub >= 2 * m * q, ref_tile(8 * p + 2 * m * q + off), ref)
        pieces.append(ref)
    delta = cs_t - jnp.concatenate(pieces, axis=0)
    return jnp.where(is_target, delta, -delta)


def _gla_level_operand(q_scr, k_ref, sl, m, reverse):
    ch = q_scr.shape[0]
    if m >= 16:
        pieces = []
        for p in range(ch // (2 * m)):
            lo = 2 * m * p
            early, late = slice(lo, lo + m), slice(lo + m, lo + 2 * m)
            if reverse:
                pieces += [q_scr[early, sl], k_ref[late, sl]]
            else:
                pieces += [k_ref[early, sl], q_scr[late, sl]]
        return jnp.concatenate(pieces, axis=0)
    row = lax.broadcasted_iota(jnp.int32, (ch, 1), 0)
    is_target = (((row // m) & 1) == 1) != reverse
    return jnp.where(is_target, q_scr[:, sl], k_ref[:, sl])


def _gla_scan_kernel(q_ref, k_ref, v_ref, small_ref, gu_ref, gb_ref, tri_ref, lev_ref, o_ref,
                     state_scr, cs_scr, att_scr, diag_scr, q_scr, *, reverse, n_chunks, firsts, lasts,
                     valid, dk, dv):
    g = pl.program_id(0)
    c = n_chunks - 1 - g if reverse else g
    ch = GLA_CHUNK
    levels = _gla_levels()
    tiles_per_head = dk // LANES

    @pl.when(_is_any(c, lasts if reverse else firsts))
    def _():
        state_scr[...] = jnp.zeros_like(state_scr)

    ok = _row_valid(c * ch, ch, valid)
    u = _dot(small_ref[...].astype(BF16), gu_ref[...]) + gb_ref[...]
    gate = jnp.where(ok, -_softplus(-u) / GLA_GATE_TAU, 0.0)
    cs = _dot_split2_lhs(tri_ref[...], gate)
    n_tiles = cs.shape[1] // LANES
    for t in range(n_tiles):
        cs_scr[t] = cs[:, t * LANES:(t + 1) * LANES]
    total = jnp.sum(gate, axis=0, keepdims=True)
    lev = lev_ref[...]
    q_scr[...] = q_ref[...] * jnp.asarray(dk ** -0.5, BF16)

    half = ch // 2
    tgt, src = (slice(0, half), slice(half, ch)) if reverse else (slice(half, ch), slice(0, half))
    quads = (slice(0, half), slice(half, ch))

    @pl.when(g == 0)
    def _():
        att_scr[:, src, tgt] = jnp.zeros((GLA_N_HEADS, half, half), BF16)

    for i, m in enumerate(levels):
        e = [jnp.exp(_gla_level_exponent(cs_scr, cs[:, t * LANES:(t + 1) * LANES], t, m, reverse)
                     ).astype(BF16) for t in range(n_tiles)]
        for h in range(GLA_N_HEADS):
            sl = slice(h * dk, (h + 1) * dk)
            e_h = jnp.concatenate(e[h * tiles_per_head:(h + 1) * tiles_per_head], axis=1)
            x = _gla_level_operand(q_scr, k_ref, sl, m, reverse) * e_h
            if i == 0:
                att_scr[h, tgt, src] = _dot_nt(x[tgt], x[src]).astype(BF16)
                continue
            for qi, qs in enumerate(quads):
                p = _dot_nt(x[qs], x[qs])
                diag_scr[h, qi] = jnp.where(lev == i, p, 0.0 if i == 1 else diag_scr[h, qi])

    for h in range(GLA_N_HEADS):
        sl = slice(h * dk, (h + 1) * dk)
        vs = slice(h * dv, (h + 1) * dv)
        vh = v_ref[:, vs]
        for qi, qs in enumerate(quads):
            p = _dot_nt(q_scr[qs, sl], k_ref[qs, sl])
            att_scr[h, qs, qs] = jnp.where(lev == len(levels), p, diag_scr[h, qi]).astype(BF16)
        st = state_scr[h]
        qd = q_scr[:, sl] * jnp.exp(cs[:, sl]).astype(BF16)
        o = _dot(att_scr[h], vh) + _dot_nt(qd, st.astype(BF16))
        o_ref[:, vs] = o.astype(BF16)
        kd = k_ref[:, sl] * jnp.exp(total[:, sl] - cs[:, sl]).astype(BF16)
        state_scr[h] = st * jnp.exp(total[:, sl]) + _dot_tn(vh, kd)


def _gla_scan(big, small, gate_up, gate_b, layout, col_q, col_k, col_v, dk, dv, reverse):
    rows = big.shape[0]
    ch = GLA_CHUNK
    n_chunks = rows // ch
    firsts = tuple(s // ch for s, _ in layout["spans"])
    lasts = tuple((s + n) // ch - 1 for s, n in layout["spans"])
    hk = GLA_N_HEADS * dk
    hv = GLA_N_HEADS * dv
    levels = _gla_levels()

    idx = np.arange(ch)
    l_, s_ = idx[:, None], idx[None, :]
    seen = (l_ <= s_) if reverse else (l_ >= s_)
    x = l_ ^ s_
    top = np.floor(np.log2(np.maximum(x, 1))).astype(np.int64)
    lvl = np.where(x == 0, len(levels), (len(levels) - 1) - top)
    lev = np.where(seen, lvl, -1).astype(np.int32)[:ch // 2, :ch // 2]

    lo = 32 + GLA_GATE_RANK * int(reverse)
    gu = jnp.zeros((LANES, hk), F32).at[lo:lo + GLA_GATE_RANK].set(gate_up).astype(BF16)

    def cmap(g):
        return n_chunks - 1 - g if reverse else g

    kern = functools.partial(_gla_scan_kernel, reverse=reverse, n_chunks=n_chunks, firsts=firsts,
                             lasts=lasts, valid=layout["valid"], dk=dk, dv=dv)
    return pl.pallas_call(
        kern,
        grid=(n_chunks,),
        in_specs=[
            pl.BlockSpec((ch, hk), lambda g: (cmap(g), col_q // hk)),
            pl.BlockSpec((ch, hk), lambda g: (cmap(g), col_k // hk)),
            pl.BlockSpec((ch, hv), lambda g: (cmap(g), col_v // hv)),
            pl.BlockSpec((ch, LANES), lambda g: (cmap(g), 0)),
            pl.BlockSpec((LANES, hk), lambda g: (0, 0)),
            pl.BlockSpec((1, hk), lambda g: (0, 0)),
            pl.BlockSpec((ch, ch), lambda g: (0, 0)),
            pl.BlockSpec((ch // 2, ch // 2), lambda g: (0, 0)),
        ],
        out_specs=pl.BlockSpec((ch, hv), lambda g: (cmap(g), 0)),
        out_shape=jax.ShapeDtypeStruct((rows, hv), BF16),
        scratch_shapes=[
            pltpu.VMEM((GLA_N_HEADS, dv, dk), F32),
            pltpu.VMEM((hk // LANES, ch, LANES), F32),
            pltpu.VMEM((GLA_N_HEADS, ch, ch), BF16),
            pltpu.VMEM((GLA_N_HEADS, 2, ch // 2, ch // 2), F32),
            pltpu.VMEM((ch, hk), BF16),
        ],
        compiler_params=_cparams("arbitrary"),
        name="gla_scan_bwd" if reverse else "gla_scan_fwd",
    )(big, big, big, small, gu, gate_b.reshape(1, hk), jnp.asarray(seen.astype(np.float32), BF16),
      jnp.asarray(lev))


def _merge_kernel(ysf_ref, ysb_ref, xs_ref, z_ref, ogf_ref, ogb_ref, ogate_ref, ms_ref, mg_ref, dskip_ref,
                  snw_ref, gnw_ref, ws_ref, wg_ref, o_ref, a_ssd, a_gla, *, dv):
    s = pl.program_id(0)
    tm = a_ssd.shape[1]

    @pl.when(s == 0)
    def _():
        a_ssd[1] = jnp.zeros(a_ssd.shape[1:], BF16)
        a_gla[1] = jnp.zeros(a_gla.shape[1:], BF16)

    slot = s % 2
    for r0 in range(0, tm, PROLOGUE_ROWS):
        rs = slice(r0, r0 + PROLOGUE_ROWS)
        y = ysf_ref[rs, :].astype(F32) + ysb_ref[rs, :].astype(F32)
        y = y + xs_ref[rs, :].astype(F32) * dskip_ref[...]
        y = y * _silu(z_ref[rs, :].astype(F32))
        ms = jnp.mean(y * y, axis=-1, keepdims=True)
        a_ssd[slot, rs, :] = (y * lax.rsqrt(ms + EPS) * snw_ref[...]).astype(BF16)
        for h in range(GLA_N_HEADS):
            sl = slice(h * dv, (h + 1) * dv)
            o = ogf_ref[rs, sl].astype(F32) + ogb_ref[rs, sl].astype(F32)
            ms = jnp.mean(o * o, axis=-1, keepdims=True)
            o = o * lax.rsqrt(ms + EPS) * gnw_ref[...]
            a_gla[slot, rs, sl] = (o * _silu(ogate_ref[rs, sl].astype(F32))).astype(BF16)

    y_ssd = _dot(a_ssd[1 - slot], ws_ref[...])
    y_gla = _dot(a_gla[1 - slot], wg_ref[...])
    merged = _sigmoid(ms_ref[...].astype(F32)) * y_ssd
    merged = merged + _sigmoid(mg_ref[...].astype(F32)) * y_gla
    o_ref[...] = merged.astype(BF16)


def _merge(y_fwd, y_bwd, xbc, big, o_fwd, o_bwd, d_skip, ssd_norm_w, gla_norm_w, w_ssd, w_gla, col_og, col_merge,
           tm=256):
    rows, d = xbc.shape[0], w_ssd.shape[1]
    di = w_ssd.shape[0]
    dvt = w_gla.shape[0]
    dv = dvt // GLA_N_HEADS
    n_tiles = rows // tm
    kern = functools.partial(_merge_kernel, dv=dv)

    def cur(s):
        return jnp.minimum(s, n_tiles - 1)

    def prev(s):
        return jnp.maximum(s - 1, 0)

    resident = pl.Buffered(1)
    return pl.pallas_call(
        kern,
        grid=(n_tiles + 1,),
        in_specs=[
            pl.BlockSpec((tm, di), lambda s: (cur(s), 0)),
            pl.BlockSpec((tm, di), lambda s: (cur(s), 0)),
            pl.BlockSpec((tm, di), lambda s: (cur(s), 0)),
            pl.BlockSpec((tm, di), lambda s: (cur(s), 0)),
            pl.BlockSpec((tm, dvt), lambda s: (cur(s), 0)),
            pl.BlockSpec((tm, dvt), lambda s: (cur(s), 0)),
            pl.BlockSpec((tm, dvt), lambda s: (cur(s), col_og // dvt)),
            pl.BlockSpec((tm, d), lambda s: (prev(s), col_merge // d)),
            pl.BlockSpec((tm, d), lambda s: (prev(s), col_merge // d + 1)),
            pl.BlockSpec((1, di), lambda s: (0, 0)),
            pl.BlockSpec((1, di), lambda s: (0, 0)),
            pl.BlockSpec((1, dv), lambda s: (0, 0)),
            pl.BlockSpec((di, d), lambda s: (0, 0), pipeline_mode=resident),
            pl.BlockSpec((dvt, d), lambda s: (0, 0), pipeline_mode=resident),
        ],
        out_specs=pl.BlockSpec((tm, d), lambda s: (prev(s), 0)),
        out_shape=jax.ShapeDtypeStruct((rows, d), BF16),
        scratch_shapes=[pltpu.VMEM((2, tm, di), BF16), pltpu.VMEM((2, tm, dvt), BF16)],
        compiler_params=_cparams("arbitrary"),
        name="merge",
    )(y_fwd, y_bwd, xbc, big, o_fwd, o_bwd, big, big, big,
      jnp.repeat(d_skip, SSD_HEAD_DIM).reshape(1, di), ssd_norm_w.reshape(1, di),
      gla_norm_w.reshape(1, dv), w_ssd, w_gla)


def _mm_res_kernel(*refs):
    *aw_refs, h_ref, o_ref = refs
    n_terms = len(aw_refs) // 2
    acc = h_ref[...]
    for a_ref, w_ref in zip(aw_refs[:n_terms], aw_refs[n_terms:]):
        acc = acc + _dot(a_ref[...], w_ref[...])
    o_ref[...] = acc


def _mm_res(a_list, w_list, h, n_split, tm=512):
    rows = h.shape[0]
    n = w_list[0].shape[1]
    tn = n // n_split
    a_specs = [pl.BlockSpec((tm, a.shape[1]), lambda j, i: (i, 0)) for a in a_list]
    w_specs = [pl.BlockSpec((w.shape[0], tn), lambda j, i: (0, j)) for w in w_list]
    return pl.pallas_call(
        _mm_res_kernel,
        grid=(n_split, rows // tm),
        in_specs=a_specs + w_specs + [pl.BlockSpec((tm, tn), lambda j, i: (i, j))],
        out_specs=pl.BlockSpec((tm, tn), lambda j, i: (i, j)),
        out_shape=jax.ShapeDtypeStruct((rows, n), F32),
        compiler_params=_cparams("arbitrary", "arbitrary"),
        name="mm_res",
    )(*a_list, *w_list, h)


def _ffn_up_kernel(h_ref, prev_ref, next_ref, nw_ref, wg_ref, wu_ref, cwg_ref, cwu_ref, cbg_ref,
                   cbu_ref, o_ref, xn_scr, *, valid):
    tm = h_ref.shape[0]

    def norm(x):
        ms = jnp.mean(x * x, axis=-1, keepdims=True)
        return (x * lax.rsqrt(ms + EPS) * nw_ref[...]).astype(BF16)

    @pl.when(pl.program_id(1) == 0)
    def _():
        xn_scr[0:HALO, :] = norm(prev_ref[...])
        for r0 in range(0, tm, PROLOGUE_ROWS):
            xn_scr[HALO + r0:HALO + r0 + PROLOGUE_ROWS, :] = norm(h_ref[r0:r0 + PROLOGUE_ROWS, :])
        xn_scr[HALO + tm:2 * HALO + tm, :] = norm(next_ref[...])

    xn = xn_scr[...]
    n = tm + 2 * HALO

    def conv(u, w_ref, b_ref, cs):
        y = b_ref[:, cs] + pltpu.roll(u, 1, 0)[HALO:HALO + tm] * w_ref[0:1, cs]
        y = y + u[HALO:HALO + tm] * w_ref[1:2, cs]
        return y + pltpu.roll(u, n - 1, 0)[HALO:HALO + tm] * w_ref[2:3, cs]

    ok = _row_valid(pl.program_id(0) * tm, tm, valid)
    for c0 in range(0, o_ref.shape[1], FFN_COL_CHUNK):
        cs = slice(c0, c0 + FFN_COL_CHUNK)
        gate = conv(_dot(xn, wg_ref[:, cs]), cwg_ref, cbg_ref, cs)
        up = conv(_dot(xn, wu_ref[:, cs]), cwu_ref, cbu_ref, cs)
        o_ref[:, cs] = jnp.where(ok, _silu(gate) * up, 0.0).astype(BF16)


def _ffn_up(h, norm_w, w_up, conv_w, conv_b, layout, tf=512):
    rows, d = h.shape
    dff = w_up.shape[1] // 2
    tm = ROW_TILE
    hb = tm // HALO
    last = rows // HALO - 1
    nj = dff // tf
    kern = functools.partial(_ffn_up_kernel, valid=layout["valid"])
    return pl.pallas_call(
        kern,
        grid=(rows // tm, nj),
        in_specs=[
            pl.BlockSpec((tm, d), lambda i, j: (i, 0)),
            pl.BlockSpec((HALO, d), lambda i, j: (jnp.maximum(i * hb - 1, 0), 0)),
            pl.BlockSpec((HALO, d), lambda i, j: (jnp.minimum((i + 1) * hb, last), 0)),
            pl.BlockSpec((1, d), lambda i, j: (0, 0)),
            pl.BlockSpec((d, tf), lambda i, j: (0, j)),
            pl.BlockSpec((d, tf), lambda i, j: (0, nj + j)),
            pl.BlockSpec((3, tf), lambda i, j: (0, j)),
            pl.BlockSpec((3, tf), lambda i, j: (0, nj + j)),
            pl.BlockSpec((1, tf), lambda i, j: (0, j)),
            pl.BlockSpec((1, tf), lambda i, j: (0, nj + j)),
        ],
        out_specs=pl.BlockSpec((tm, tf), lambda i, j: (i, j)),
        out_shape=jax.ShapeDtypeStruct((rows, dff), BF16),
        scratch_shapes=[pltpu.VMEM((tm + 2 * HALO, d), BF16)],
        compiler_params=_cparams("parallel", "arbitrary"),
        name="ffn_up",
    )(h, h, h, norm_w.reshape(1, d), w_up, w_up, conv_w, conv_w,
      conv_b.reshape(1, 2 * dff), conv_b.reshape(1, 2 * dff))


def _final_norm_kernel(h_ref, nw_ref, o_ref):
    x = h_ref[...]
    ms = jnp.mean(x * x, axis=-1, keepdims=True)
    o_ref[0] = x * lax.rsqrt(ms + EPS) * nw_ref[...]


def _final_norm(h, norm_w, first_row, batch, length, tr=ROW_ALIGN):
    d = h.shape[1]
    per_seq = (ROW_ALIGN + length) // tr
    b0 = (first_row + ROW_ALIGN) // tr
    return pl.pallas_call(
        _final_norm_kernel,
        grid=(batch, length // tr),
        in_specs=[
            pl.BlockSpec((tr, d), lambda b, t: (b0 + b * per_seq + t, 0)),
            pl.BlockSpec((1, d), lambda b, t: (0, 0)),
        ],
        out_specs=pl.BlockSpec((1, tr, d), lambda b, t: (b, t, 0)),
        out_shape=jax.ShapeDtypeStruct((batch, length, d), F32),
        compiler_params=_cparams("parallel", "parallel"),
        name="final_norm",
    )(h, norm_w.reshape(1, d))


def _make_layout(groups):
    seqs, valid, row = [], [], 0
    for batch, length in groups:
        assert length % ROW_ALIGN == 0
        for _ in range(batch):
            n = ROW_ALIGN + length
            seqs.append((row, n))
            valid.append((row + FRONT_PAD, row + n))
            row += n
    rows = (row // ROW_TILE + 1) * ROW_TILE
    spans = tuple(seqs) + ((row, rows - row),)
    return dict(seqs=tuple(seqs), spans=spans, valid=tuple(valid), used_rows=row, rows=rows)


def kernel(x_prompt, x_sample, meta_tokens, mix_norm_w, w_in, ssd_conv_w, ssd_conv_b, ssd_dt_bias,
           ssd_a_log, ssd_d, ssd_norm_w, ssd_w_out, gla_gate_up, gla_gate_b, gla_norm_w, gla_w_out,
           w_mix_out, ffn_norm_w, ffn_w_up, ffn_conv_w, ffn_conv_b, ffn_w_down, final_norm_w):
    depth, d = mix_norm_w.shape
    groups = [(x_prompt.shape[0], x_prompt.shape[1]), (x_sample.shape[0], x_sample.shape[1])]
    layout = _make_layout(groups)
    rows = layout["rows"]

    n_heads = ssd_dt_bias.shape[-1]
    d_inner = n_heads * SSD_HEAD_DIM
    conv_dim = ssd_conv_w.shape[-1]
    d_k = gla_gate_up.shape[-1]
    d_v = gla_w_out.shape[1]
    dk, dv = d_k // GLA_N_HEADS, d_v // GLA_N_HEADS
    sizes = (d_inner, conv_dim, n_heads, d_k, d_k, d_v, d_v, 2 * GLA_GATE_RANK, 2 * d)
    off = np.concatenate([[0], np.cumsum(sizes)])
    assert off[-1] == w_in.shape[-1]
    wide = [0, 1, 3, 4, 5, 6, 8]
    col, pos = {}, 0
    for s in wide:
        col[s] = pos
        pos += sizes[s]

    zero_front = jnp.zeros((FRONT_PAD, d), F32)
    pieces = []
    for x in (x_prompt, x_sample):
        for b in range(x.shape[0]):
            pieces += [zero_front, meta_tokens.astype(F32), x[b]]
    if rows > layout["used_rows"]:
        pieces.append(jnp.zeros((rows - layout["used_rows"], d), F32))
    h = jnp.concatenate(pieces, axis=0)

    for i in range(depth):
        w = w_in[i]
        w_big = jnp.concatenate([w[:, off[s]:off[s + 1]] for s in wide], axis=1).astype(BF16)
        w_small = jnp.concatenate(
            [w[:, off[2]:off[3]], w[:, off[7]:off[8]],
             jnp.zeros((d, LANES - n_heads - 2 * GLA_GATE_RANK), F32)], axis=1).astype(BF16)

        big, small = _inproj(h, mix_norm_w[i], w_big, w_small)
        xbc = _ssd_conv(big, ssd_conv_w[i], ssd_conv_b[i], col[1])
        y_fwd, y_bwd = _ssd_scan(xbc, small, ssd_dt_bias[i], ssd_a_log[i], layout)
        o_fwd, o_bwd = (_gla_scan(big, small, gla_gate_up[i, r], gla_gate_b[i, r], layout,
                                  col[3], col[4], col[5], dk, dv, reverse=bool(r)) for r in (0, 1))
        merged = _merge(y_fwd, y_bwd, xbc, big, o_fwd, o_bwd, ssd_d[i], ssd_norm_w[i], gla_norm_w[i],
                        ssd_w_out[i].astype(BF16), gla_w_out[i].astype(BF16), col[6], col[8])
        h = _mm_res([merged], [w_mix_out[i].astype(BF16)], h, n_split=1)
        act = _ffn_up(h, ffn_norm_w[i], ffn_w_up[i].astype(BF16), ffn_conv_w[i], ffn_conv_b[i],
                      layout)
        h = _mm_res([act], [ffn_w_down[i].astype(BF16)], h, n_split=2)

    outs, si = [], 0
    for batch, length in groups:
        outs.append(_final_norm(h, final_norm_w, layout["seqs"][si][0], batch, length))
        si += batch
    return tuple(outs)
```

```python
import functools

import numpy as np
import jax
import jax.numpy as jnp
from jax import lax
from jax.experimental import pallas as pl
from jax.experimental.pallas import tpu as pltpu

F32 = jnp.float32
BF16 = jnp.bfloat16

N_META = 16
EPS = 1e-6
SSD_HEAD_DIM = 64
SSD_N_GROUPS = 8
SSD_HEADS_PER_GROUP = 4
SSD_D_STATE = 128
GLA_N_HEADS = 4
GLA_GATE_RANK = 16
GLA_GATE_TAU = 16.0

LANES = 128
ROW_ALIGN = 256
FRONT_PAD = ROW_ALIGN - N_META
SSD_CHUNK = 128
GLA_CHUNK = 256
ROW_TILE = 1024
PROLOGUE_ROWS = 128
FFN_COL_CHUNK = 256
FFN_ROW_TILE = 512
HALO = 16
VMEM_LIMIT = 56 * 1024 * 1024
NEG_INF = float("-inf")


def _cparams(*sem):
    return pltpu.CompilerParams(dimension_semantics=sem, vmem_limit_bytes=VMEM_LIMIT)


def _softplus(x):
    return jnp.maximum(x, 0.0) + jnp.log(1.0 + jnp.exp(-jnp.abs(x)))


def _sigmoid(x):
    return 0.5 + 0.5 * jnp.tanh(0.5 * x)


def _silu(x):
    hx = 0.5 * x
    return hx + hx * jnp.tanh(hx)


def _split3(x):
    x1 = x.astype(BF16)
    r1 = x - x1.astype(F32)
    x2 = r1.astype(BF16)
    x3 = (r1 - x2.astype(F32)).astype(BF16)
    return x1, x2, x3


def _dot(a, b):
    return jnp.dot(a, b, preferred_element_type=F32)


def _dot_nt(a, b):
    return lax.dot_general(a, b, (((1,), (1,)), ((), ())), preferred_element_type=F32)


def _dot_tn(a, b):
    return lax.dot_general(a, b, (((0,), (0,)), ((), ())), preferred_element_type=F32)


def _dot_exact_lhs(t, x):
    x1, x2, x3 = _split3(x)
    return _dot(t, x1) + _dot(t, x2) + _dot(t, x3)


def _dot_split2_lhs(t, x):
    x1 = x.astype(BF16)
    x2 = (x - x1.astype(F32)).astype(BF16)
    return _dot(t, x1) + _dot(t, x2)


def _dot_exact_rhs(x, t):
    x1, x2, x3 = _split3(x)
    return _dot(x1, t) + _dot(x2, t) + _dot(x3, t)


def _row_valid(row0, n, ranges):
    r = row0 + lax.broadcasted_iota(jnp.int32, (n, 1), 0)
    ok = None
    for lo, hi in ranges:
        m = (r >= lo) & (r < hi)
        ok = m if ok is None else (ok | m)
    return ok


def _is_any(c, values):
    ok = None
    for v in values:
        m = c == v
        ok = m if ok is None else (ok | m)
    return ok


def _inproj_kernel(h_ref, nw_ref, wb_ref, ws_ref, big_ref, small_ref, xn_scr):
    @pl.when(pl.program_id(1) == 0)
    def _():
        for r0 in range(0, h_ref.shape[0], PROLOGUE_ROWS):
            rs = slice(r0, r0 + PROLOGUE_ROWS)
            x = h_ref[rs, :]
            ms = jnp.mean(x * x, axis=-1, keepdims=True)
            xn_scr[rs, :] = (x * lax.rsqrt(ms + EPS) * nw_ref[...]).astype(BF16)
        small_ref[...] = _dot(xn_scr[...], ws_ref[...])

    big_ref[...] = _dot(xn_scr[...], wb_ref[...]).astype(BF16)


def _inproj(h, norm_w, w_big, w_small, tn=2048):
    rows, d = h.shape
    nb = w_big.shape[1]
    return pl.pallas_call(
        _inproj_kernel,
        grid=(rows // ROW_TILE, nb // tn),
        in_specs=[
            pl.BlockSpec((ROW_TILE, d), lambda i, j: (i, 0)),
            pl.BlockSpec((1, d), lambda i, j: (0, 0)),
            pl.BlockSpec((d, tn), lambda i, j: (0, j)),
            pl.BlockSpec((d, LANES), lambda i, j: (0, 0)),
        ],
        out_specs=[
            pl.BlockSpec((ROW_TILE, tn), lambda i, j: (i, j)),
            pl.BlockSpec((ROW_TILE, LANES), lambda i, j: (i, 0)),
        ],
        out_shape=[
            jax.ShapeDtypeStruct((rows, nb), BF16),
            jax.ShapeDtypeStruct((rows, LANES), F32),
        ],
        scratch_shapes=[pltpu.VMEM((ROW_TILE, d), BF16)],
        compiler_params=_cparams("parallel", "arbitrary"),
        name="inproj",
    )(h, norm_w.reshape(1, d), w_big, w_small)


def _conv3_rows(x, prev_row, next_row, w_ref, b_ref):
    n = x.shape[0]
    row = lax.broadcasted_iota(jnp.int32, (n, 1), 0)
    up = jnp.where(row == 0, prev_row, pltpu.roll(x, 1, 0))
    dn = jnp.where(row == n - 1, next_row, pltpu.roll(x, n - 1, 0))
    y = b_ref[...] + up * w_ref[0:1, :]
    y = y + x * w_ref[1:2, :]
    return y + dn * w_ref[2:3, :]


def _ssd_conv_kernel(x_ref, prev_ref, next_ref, w_ref, b_ref, o_ref):
    x = x_ref[...].astype(F32)
    prev_row = prev_ref[HALO - 1:HALO, :].astype(F32)
    next_row = next_ref[0:1, :].astype(F32)
    o_ref[...] = _silu(_conv3_rows(x, prev_row, next_row, w_ref, b_ref)).astype(BF16)


def _ssd_conv(big, conv_w, conv_b, col0, tr=512, tc=2048):
    rows = big.shape[0]
    c = conv_w.shape[1]
    cb0 = col0 // tc
    hb = tr // HALO
    last = rows // HALO - 1
    return pl.pallas_call(
        _ssd_conv_kernel,
        grid=(rows // tr, c // tc),
        in_specs=[
            pl.BlockSpec((tr, tc), lambda i, j: (i, cb0 + j)),
            pl.BlockSpec((HALO, tc), lambda i, j: (jnp.maximum(i * hb - 1, 0), cb0 + j)),
            pl.BlockSpec((HALO, tc), lambda i, j: (jnp.minimum((i + 1) * hb, last), cb0 + j)),
            pl.BlockSpec((3, tc), lambda i, j: (0, j)),
            pl.BlockSpec((1, tc), lambda i, j: (0, j)),
        ],
        out_specs=pl.BlockSpec((tr, tc), lambda i, j: (i, j)),
        out_shape=jax.ShapeDtypeStruct((rows, c), BF16),
        compiler_params=_cparams("parallel", "parallel"),
        name="ssd_conv",
    )(big, big, big, conv_w, conv_b.reshape(1, c))


def _ssd_chunk(d, c, x_ref, b_ref, c_ref, small_ref, dtb_ref, alog_ref, tri_ref, mask_ref, exp_ref,
               y_ref, state_scr, valid):
    ch = SSD_CHUNK
    ok = _row_valid(c * ch, ch, valid)
    dt = _softplus(small_ref[...] + dtb_ref[d])
    dt = jnp.where(ok, dt, 0.0)
    a = -jnp.exp(alog_ref[d]) * dt
    acs = _dot_exact_lhs(tri_ref[d], a)
    acs_t = _dot_exact_rhs(a.T, tri_ref[1 - d])
    dt_t = dt.T
    total = jnp.sum(a, axis=0, keepdims=True)
    w_in = jnp.exp(total - acs) * dt
    per_head = jnp.concatenate(
        [w_in, jnp.exp(acs), jnp.broadcast_to(jnp.exp(total), (8, LANES))], axis=0)
    hi = per_head.astype(BF16)
    lo = (per_head - hi.astype(F32)).astype(BF16)
    per_lane = _dot(jnp.concatenate([hi, lo], axis=1), exp_ref[...])
    w_exp = per_lane[0:ch]
    ea_exp = per_lane[ch:2 * ch]
    et_exp = per_lane[2 * ch:2 * ch + 1]
    mask = mask_ref[d]

    gw = SSD_HEADS_PER_GROUP * SSD_HEAD_DIM
    lane = lax.broadcasted_iota(jnp.int32, (1, gw), 1)
    for grp in range(SSD_N_GROUPS):
        bg = b_ref[:, grp * SSD_D_STATE:(grp + 1) * SSD_D_STATE]
        cg = c_ref[:, grp * SSD_D_STATE:(grp + 1) * SSD_D_STATE]
        xg = x_ref[:, grp * gw:(grp + 1) * gw]
        cb = _dot_nt(cg, bg)
        st = state_scr[d, grp]
        y = _dot(cg, st.astype(BF16)) * ea_exp[:, grp * gw:(grp + 1) * gw]
        ms, xs = [], []
        for hh in range(SSD_HEADS_PER_GROUP):
            h = grp * SSD_HEADS_PER_GROUP + hh
            seg = acs[:, h:h + 1] - acs_t[h:h + 1, :] + mask
            ms.append((cb * jnp.exp(seg) * dt_t[h:h + 1, :]).astype(BF16))
            in_head = (lane >= hh * SSD_HEAD_DIM) & (lane < (hh + 1) * SSD_HEAD_DIM)
            xs.append(jnp.where(in_head, xg, jnp.zeros_like(xg)))
        y = y + _dot(jnp.concatenate(ms, axis=1), jnp.concatenate(xs, axis=0))
        y_ref[:, grp * gw:(grp + 1) * gw] = y.astype(BF16)
        xw = (xg.astype(F32) * w_exp[:, grp * gw:(grp + 1) * gw]).astype(BF16)
        state_scr[d, grp] = st * et_exp[:, grp * gw:(grp + 1) * gw] + _dot_tn(bg, xw)


def _ssd_scan_kernel(xf_ref, bf_ref, cf_ref, sf_ref, xb_ref, bb_ref, cb_ref, sb_ref, dtb_ref,
                     alog_ref, tri_ref, mask_ref, exp_ref, yf_ref, yb_ref, state_scr, *, n_chunks,
                     firsts, lasts, valid):
    g = pl.program_id(0)
    r = n_chunks - 1 - g

    @pl.when(_is_any(g, firsts))
    def _():
        state_scr[0] = jnp.zeros(state_scr.shape[1:], F32)

    @pl.when(_is_any(r, lasts))
    def _():
        state_scr[1] = jnp.zeros(state_scr.shape[1:], F32)

    shared = (dtb_ref, alog_ref, tri_ref, mask_ref, exp_ref)
    _ssd_chunk(0, g, xf_ref, bf_ref, cf_ref, sf_ref, *shared, yf_ref, state_scr, valid)
    _ssd_chunk(1, r, xb_ref, bb_ref, cb_ref, sb_ref, *shared, yb_ref, state_scr, valid)


def _ssd_scan(xbc, small, dt_bias, a_log, layout):
    rows = xbc.shape[0]
    ch = SSD_CHUNK
    d_inner = SSD_N_GROUPS * SSD_HEADS_PER_GROUP * SSD_HEAD_DIM
    gn = SSD_N_GROUPS * SSD_D_STATE
    n_heads = SSD_N_GROUPS * SSD_HEADS_PER_GROUP
    n_chunks = rows // ch
    firsts = tuple(s // ch for s, _ in layout["spans"])
    lasts = tuple((s + n) // ch - 1 for s, n in layout["spans"])

    idx = np.arange(ch)
    lower = (idx[:, None] >= idx[None, :])
    tri = np.stack([lower, lower.T]).astype(np.float32)
    mask = np.where(tri > 0, 0.0, NEG_INF).astype(np.float32)
    expand = np.zeros((2 * LANES, d_inner), np.float32)
    for h in range(n_heads):
        expand[h, h * SSD_HEAD_DIM:(h + 1) * SSD_HEAD_DIM] = 1.0
        expand[LANES + h, h * SSD_HEAD_DIM:(h + 1) * SSD_HEAD_DIM] = 1.0
    pad = ((0, 0), (0, 0), (0, LANES - n_heads))
    dtb = jnp.pad(dt_bias.reshape(2, 1, n_heads), pad)
    alog = jnp.pad(a_log.reshape(2, 1, n_heads), pad)

    def chunk_specs(cmap):
        return [
            pl.BlockSpec((ch, d_inner), lambda g: (cmap(g), 0)),
            pl.BlockSpec((ch, gn), lambda g: (cmap(g), d_inner // gn)),
            pl.BlockSpec((ch, gn), lambda g: (cmap(g), d_inner // gn + 1)),
            pl.BlockSpec((ch, LANES), lambda g: (cmap(g), 0)),
        ]

    def fwd(g):
        return g

    def rev(g):
        return n_chunks - 1 - g

    kern = functools.partial(_ssd_scan_kernel, n_chunks=n_chunks, firsts=firsts, lasts=lasts,
                             valid=layout["valid"])
    return pl.pallas_call(
        kern,
        grid=(n_chunks,),
        in_specs=chunk_specs(fwd) + chunk_specs(rev) + [
            pl.BlockSpec((2, 1, LANES), lambda g: (0, 0, 0)),
            pl.BlockSpec((2, 1, LANES), lambda g: (0, 0, 0)),
            pl.BlockSpec((2, ch, ch), lambda g: (0, 0, 0)),
            pl.BlockSpec((2, ch, ch), lambda g: (0, 0, 0)),
            pl.BlockSpec((2 * LANES, d_inner), lambda g: (0, 0)),
        ],
        out_specs=[pl.BlockSpec((ch, d_inner), lambda g: (fwd(g), 0)),
                   pl.BlockSpec((ch, d_inner), lambda g: (rev(g), 0))],
        out_shape=[jax.ShapeDtypeStruct((rows, d_inner), BF16)] * 2,
        scratch_shapes=[
            pltpu.VMEM((2, SSD_N_GROUPS, SSD_D_STATE, SSD_HEADS_PER_GROUP * SSD_HEAD_DIM), F32)],
        compiler_params=_cparams("arbitrary"),
        name="ssd_scan",
    )(xbc, xbc, xbc, small, xbc, xbc, xbc, small, dtb, alog, jnp.asarray(tri, BF16),
      jnp.asarray(mask), jnp.asarray(expand, BF16))


def _gla_levels():
    lv, m = [], GLA_CHUNK // 2
    while m >= 1:
        lv.append(m)
        m //= 2
    return tuple(lv)


def _gla_level_exponent(cs_scr, cs_t, t, m, reverse):
    ch = cs_t.shape[0]
    off = m - 1 if reverse else m

    def ref_tile(r):
        return jnp.broadcast_to(cs_scr[t, pl.ds(r, 1), :], (8, LANES))

    if m >= 8:
        pieces = []
        for p in range(ch // (2 * m)):
            lo = 2 * m * p
            ref = jnp.concatenate([ref_tile(lo + off)] * (m // 8), axis=0)
            early, late = cs_t[lo:lo + m], cs_t[lo + m:lo + 2 * m]
            pieces += [early - ref, ref - late] if reverse else [ref - early, late - ref]
        return jnp.concatenate(pieces, axis=0)

    row = lax.broadcasted_iota(jnp.int32, (ch, 1), 0)
    in_late = ((row // m) & 1) == 1
    is_target = in_late != reverse
    if m == 1:
        x3 = cs_t.reshape(ch // 8, 8, LANES)
        nb = pltpu.roll(x3, 1 if reverse else 7, 1).reshape(ch, LANES)
        return jnp.where(is_target, 0.0, nb - cs_t)
    sub = lax.broadcasted_iota(jnp.int32, (8, 1), 0)
    pieces = []
    for p in range(ch // 8):
        ref = ref_tile(8 * p + off)
        for q in range(1, 8 // (2 * m)):
            ref = jnp.where(sub >= 2 * m * q, ref_tile(8 * p + 2 * m * q + off), ref)
        pieces.append(ref)
    delta = cs_t - jnp.concatenate(pieces, axis=0)
    return jnp.where(is_target, delta, -delta)


def _gla_level_operand(q_scr, k_ref, sl, m, reverse):
    ch = q_scr.shape[0]
    if m >= 16:
        pieces = []
        for p in range(ch // (2 * m)):
            lo = 2 * m * p
            early, late = slice(lo, lo + m), slice(lo + m, lo + 2 * m)
            if reverse:
                pieces += [q_scr[early, sl], k_ref[late, sl]]
            else:
                pieces += [k_ref[early, sl], q_scr[late, sl]]
        return jnp.concatenate(pieces, axis=0)
    row = lax.broadcasted_iota(jnp.int32, (ch, 1), 0)
    is_target = (((row // m) & 1) == 1) != reverse
    return jnp.where(is_target, q_scr[:, sl], k_ref[:, sl])


def _gla_scan_kernel(q_ref, k_ref, v_ref, small_ref, gu_ref, gb_ref, tri_ref, lev_ref, o_ref,
                     state_scr, cs_scr, att_scr, diag_scr, q_scr, *, reverse, n_chunks, firsts, lasts,
                     valid, dk, dv):
    g = pl.program_id(0)
    c = n_chunks - 1 - g if reverse else g
    ch = GLA_CHUNK
    levels = _gla_levels()
    tiles_per_head = dk // LANES

    @pl.when(_is_any(c, lasts if reverse else firsts))
    def _():
        state_scr[...] = jnp.zeros_like(state_scr)

    ok = _row_valid(c * ch, ch, valid)
    u = _dot(small_ref[...].astype(BF16), gu_ref[...]) + gb_ref[...]
    gate = jnp.where(ok, -_softplus(-u) / GLA_GATE_TAU, 0.0)
    cs = _dot_split2_lhs(tri_ref[...], gate)
    n_tiles = cs.shape[1] // LANES
    for t in range(n_tiles):
        cs_scr[t] = cs[:, t * LANES:(t + 1) * LANES]
    total = jnp.sum(gate, axis=0, keepdims=True)
    lev = lev_ref[...]
    q_scr[...] = q_ref[...] * jnp.asarray(dk ** -0.5, BF16)

    half = ch // 2
    tgt, src = (slice(0, half), slice(half, ch)) if reverse else (slice(half, ch), slice(0, half))
    quads = (slice(0, half), slice(half, ch))

    @pl.when(g == 0)
    def _():
        att_scr[:, src, tgt] = jnp.zeros((GLA_N_HEADS, half, half), BF16)

    for i, m in enumerate(levels):
        e = [jnp.exp(_gla_level_exponent(cs_scr, cs[:, t * LANES:(t + 1) * LANES], t, m, reverse)
                     ).astype(BF16) for t in range(n_tiles)]
        for h in range(GLA_N_HEADS):
            sl = slice(h * dk, (h + 1) * dk)
            e_h = jnp.concatenate(e[h * tiles_per_head:(h + 1) * tiles_per_head], axis=1)
            x = _gla_level_operand(q_scr, k_ref, sl, m, reverse) * e_h
            if i == 0:
                att_scr[h, tgt, src] = _dot_nt(x[tgt], x[src]).astype(BF16)
                continue
            for qi, qs in enumerate(quads):
                p = _dot_nt(x[qs], x[qs])
                diag_scr[h, qi] = jnp.where(lev == i, p, 0.0 if i == 1 else diag_scr[h, qi])

    for h in range(GLA_N_HEADS):
        sl = slice(h * dk, (h + 1) * dk)
        vs = slice(h * dv, (h + 1) * dv)
        vh = v_ref[:, vs]
        for qi, qs in enumerate(quads):
            p = _dot_nt(q_scr[qs, sl], k_ref[qs, sl])
            att_scr[h, qs, qs] = jnp.where(lev == len(levels), p, diag_scr[h, qi]).astype(BF16)
        st = state_scr[h]
        qd = q_scr[:, sl] * jnp.exp(cs[:, sl]).astype(BF16)
        o = _dot(att_scr[h], vh) + _dot_nt(qd, st.astype(BF16))
        o_ref[:, vs] = o.astype(BF16)
        kd = k_ref[:, sl] * jnp.exp(total[:, sl] - cs[:, sl]).astype(BF16)
        state_scr[h] = st * jnp.exp(total[:, sl]) + _dot_tn(vh, kd)


def _gla_scan(big, small, gate_up, gate_b, layout, col_q, col_k, col_v, dk, dv, reverse):
    rows = big.shape[0]
    ch = GLA_CHUNK
    n_chunks = rows // ch
    firsts = tuple(s // ch for s, _ in layout["spans"])
    lasts = tuple((s + n) // ch - 1 for s, n in layout["spans"])
    hk = GLA_N_HEADS * dk
    hv = GLA_N_HEADS * dv
    levels = _gla_levels()

    idx = np.arange(ch)
    l_, s_ = idx[:, None], idx[None, :]
    seen = (l_ <= s_) if reverse else (l_ >= s_)
    x = l_ ^ s_
    top = np.floor(np.log2(np.maximum(x, 1))).astype(np.int64)
    lvl = np.where(x == 0, len(levels), (len(levels) - 1) - top)
    lev = np.where(seen, lvl, -1).astype(np.int32)[:ch // 2, :ch // 2]

    lo = 32 + GLA_GATE_RANK * int(reverse)
    gu = jnp.zeros((LANES, hk), F32).at[lo:lo + GLA_GATE_RANK].set(gate_up).astype(BF16)

    def cmap(g):
        return n_chunks - 1 - g if reverse else g

    kern = functools.partial(_gla_scan_kernel, reverse=reverse, n_chunks=n_chunks, firsts=firsts,
                             lasts=lasts, valid=layout["valid"], dk=dk, dv=dv)
    return pl.pallas_call(
        kern,
        grid=(n_chunks,),
        in_specs=[
            pl.BlockSpec((ch, hk), lambda g: (cmap(g), col_q // hk)),
            pl.BlockSpec((ch, hk), lambda g: (cmap(g), col_k // hk)),
            pl.BlockSpec((ch, hv), lambda g: (cmap(g), col_v // hv)),
            pl.BlockSpec((ch, LANES), lambda g: (cmap(g), 0)),
            pl.BlockSpec((LANES, hk), lambda g: (0, 0)),
            pl.BlockSpec((1, hk), lambda g: (0, 0)),
            pl.BlockSpec((ch, ch), lambda g: (0, 0)),
            pl.BlockSpec((ch // 2, ch // 2), lambda g: (0, 0)),
        ],
        out_specs=pl.BlockSpec((ch, hv), lambda g: (cmap(g), 0)),
        out_shape=jax.ShapeDtypeStruct((rows, hv), BF16),
        scratch_shapes=[
            pltpu.VMEM((GLA_N_HEADS, dv, dk), F32),
            pltpu.VMEM((hk // LANES, ch, LANES), F32),
            pltpu.VMEM((GLA_N_HEADS, ch, ch), BF16),
            pltpu.VMEM((GLA_N_HEADS, 2, ch // 2, ch // 2), F32),
            pltpu.VMEM((ch, hk), BF16),
        ],
        compiler_params=_cparams("arbitrary"),
        name="gla_scan_bwd" if reverse else "gla_scan_fwd",
    )(big, big, big, small, gu, gate_b.reshape(1, hk), jnp.asarray(seen.astype(np.float32), BF16),
      jnp.asarray(lev))


def _merge_kernel(ysf_ref, ysb_ref, xs_ref, z_ref, ogf_ref, ogb_ref, ogate_ref, ms_ref, mg_ref, dskip_ref,
                  snw_ref, gnw_ref, ws_ref, wg_ref, o_ref, *, dv):
    tm = o_ref.shape[0]
    for r0 in range(0, tm, PROLOGUE_ROWS):
        rs = slice(r0, r0 + PROLOGUE_ROWS)
        y = ysf_ref[rs, :].astype(F32) + ysb_ref[rs, :].astype(F32)
        y = y + xs_ref[rs, :].astype(F32) * dskip_ref[...]
        y = y * _silu(z_ref[rs, :].astype(F32))
        ms = jnp.mean(y * y, axis=-1, keepdims=True)
        a_s = (y * lax.rsqrt(ms + EPS) * snw_ref[...]).astype(BF16)
        parts = []
        for h in range(GLA_N_HEADS):
            sl = slice(h * dv, (h + 1) * dv)
            o = ogf_ref[rs, sl].astype(F32) + ogb_ref[rs, sl].astype(F32)
            ms = jnp.mean(o * o, axis=-1, keepdims=True)
            o = o * lax.rsqrt(ms + EPS) * gnw_ref[...]
            parts.append((o * _silu(ogate_ref[rs, sl].astype(F32))).astype(BF16))
        a_g = jnp.concatenate(parts, axis=1)
        merged = _sigmoid(ms_ref[rs, :].astype(F32)) * _dot(a_s, ws_ref[...])
        merged = merged + _sigmoid(mg_ref[rs, :].astype(F32)) * _dot(a_g, wg_ref[...])
        o_ref[rs, :] = merged.astype(BF16)


def _merge(y_fwd, y_bwd, xbc, big, o_fwd, o_bwd, d_skip, ssd_norm_w, gla_norm_w, w_ssd, w_gla, col_og, col_merge,
           tm=256):
    rows, d = xbc.shape[0], w_ssd.shape[1]
    di = w_ssd.shape[0]
    dvt = w_gla.shape[0]
    dv = dvt // GLA_N_HEADS
    n_tiles = rows // tm
    kern = functools.partial(_merge_kernel, dv=dv)

    def cur(s):
        return s

    prev = cur
    resident = pl.Buffered(1)
    return pl.pallas_call(
        kern,
        grid=(n_tiles,),
        in_specs=[
            pl.BlockSpec((tm, di), lambda s: (cur(s), 0)),
            pl.BlockSpec((tm, di), lambda s: (cur(s), 0)),
            pl.BlockSpec((tm, di), lambda s: (cur(s), 0)),
            pl.BlockSpec((tm, di), lambda s: (cur(s), 0)),
            pl.BlockSpec((tm, dvt), lambda s: (cur(s), 0)),
            pl.BlockSpec((tm, dvt), lambda s: (cur(s), 0)),
            pl.BlockSpec((tm, dvt), lambda s: (cur(s), col_og // dvt)),
            pl.BlockSpec((tm, d), lambda s: (prev(s), col_merge // d)),
            pl.BlockSpec((tm, d), lambda s: (prev(s), col_merge // d + 1)),
            pl.BlockSpec((1, di), lambda s: (0, 0)),
            pl.BlockSpec((1, di), lambda s: (0, 0)),
            pl.BlockSpec((1, dv), lambda s: (0, 0)),
            pl.BlockSpec((di, d), lambda s: (0, 0), pipeline_mode=resident),
            pl.BlockSpec((dvt, d), lambda s: (0, 0), pipeline_mode=resident),
        ],
        out_specs=pl.BlockSpec((tm, d), lambda s: (prev(s), 0)),
        out_shape=jax.ShapeDtypeStruct((rows, d), BF16),
        compiler_params=_cparams("arbitrary"),
        name="merge",
    )(y_fwd, y_bwd, xbc, big, o_fwd, o_bwd, big, big, big,
      jnp.repeat(d_skip, SSD_HEAD_DIM).reshape(1, di), ssd_norm_w.reshape(1, di),
      gla_norm_w.reshape(1, dv), w_ssd, w_gla)


def _mm_res_kernel(*refs):
    *aw_refs, h_ref, o_ref = refs
    n_terms = len(aw_refs) // 2
    acc = h_ref[...]
    for a_ref, w_ref in zip(aw_refs[:n_terms], aw_refs[n_terms:]):
        acc = acc + _dot(a_ref[...], w_ref[...])
    o_ref[...] = acc


def _mm_res(a_list, w_list, h, n_split, tm=512):
    rows = h.shape[0]
    n = w_list[0].shape[1]
    tn = n // n_split
    a_specs = [pl.BlockSpec((tm, a.shape[1]), lambda j, i: (i, 0)) for a in a_list]
    w_specs = [pl.BlockSpec((w.shape[0], tn), lambda j, i: (0, j)) for w in w_list]
    return pl.pallas_call(
        _mm_res_kernel,
        grid=(n_split, rows // tm),
        in_specs=a_specs + w_specs + [pl.BlockSpec((tm, tn), lambda j, i: (i, j))],
        out_specs=pl.BlockSpec((tm, tn), lambda j, i: (i, j)),
        out_shape=jax.ShapeDtypeStruct((rows, n), F32),
        compiler_params=_cparams("arbitrary", "arbitrary"),
        name="mm_res",
    )(*a_list, *w_list, h)


def _ffn_kernel(h_ref, prev_ref, next_ref, nw_ref, wg_ref, wu_ref, cwg_ref, cwu_ref, cbg_ref,
                   cbu_ref, wd_ref, o_ref, xn_scr, u_scr, *, valid):
    tm = h_ref.shape[0]

    def norm(x):
        ms = jnp.mean(x * x, axis=-1, keepdims=True)
        return (x * lax.rsqrt(ms + EPS) * nw_ref[...]).astype(BF16)

    @pl.when(pl.program_id(1) == 0)
    def _():
        xn_scr[0:HALO, :] = norm(prev_ref[...])
        for r0 in range(0, tm, PROLOGUE_ROWS):
            xn_scr[HALO + r0:HALO + r0 + PROLOGUE_ROWS, :] = norm(h_ref[r0:r0 + PROLOGUE_ROWS, :])
        xn_scr[HALO + tm:2 * HALO + tm, :] = norm(next_ref[...])
        o_ref[...] = h_ref[...]

    xn = xn_scr[...]
    n = tm + 2 * HALO

    def conv(u, w_ref, b_ref, cs):
        y = b_ref[:, cs] + pltpu.roll(u, 1, 0)[HALO:HALO + tm] * w_ref[0:1, cs]
        y = y + u[HALO:HALO + tm] * w_ref[1:2, cs]
        return y + pltpu.roll(u, n - 1, 0)[HALO:HALO + tm] * w_ref[2:3, cs]

    ok = _row_valid(pl.program_id(0) * tm, tm, valid)
    chunks = [slice(c0, c0 + FFN_COL_CHUNK) for c0 in range(0, wg_ref.shape[1], FFN_COL_CHUNK)]
    slot = pl.program_id(1) % 2
    for k, cs in enumerate(chunks):
        u_scr[slot, 2 * k] = _dot(xn, wg_ref[:, cs]).astype(BF16)
        u_scr[slot, 2 * k + 1] = _dot(xn, wu_ref[:, cs]).astype(BF16)
    for k, cs in enumerate(chunks):
        gate = conv(u_scr[slot, 2 * k].astype(F32), cwg_ref, cbg_ref, cs)
        up = conv(u_scr[slot, 2 * k + 1].astype(F32), cwu_ref, cbu_ref, cs)
        act = jnp.where(ok, _silu(gate) * up, 0.0).astype(BF16)
        o_ref[...] += _dot(act, wd_ref[cs, :])


def _ffn(h, norm_w, w_up, conv_w, conv_b, w_down, layout, tm, tf=512):
    rows, d = h.shape
    dff = w_up.shape[1] // 2
    hb = tm // HALO
    last = rows // HALO - 1
    nj = dff // tf
    kern = functools.partial(_ffn_kernel, valid=layout["valid"])
    return pl.pallas_call(
        kern,
        grid=(rows // tm, nj),
        in_specs=[
            pl.BlockSpec((tm, d), lambda i, j: (i, 0)),
            pl.BlockSpec((HALO, d), lambda i, j: (jnp.maximum(i * hb - 1, 0), 0)),
            pl.BlockSpec((HALO, d), lambda i, j: (jnp.minimum((i + 1) * hb, last), 0)),
            pl.BlockSpec((1, d), lambda i, j: (0, 0)),
            pl.BlockSpec((d, tf), lambda i, j: (0, j)),
            pl.BlockSpec((d, tf), lambda i, j: (0, nj + j)),
            pl.BlockSpec((3, tf), lambda i, j: (0, j)),
            pl.BlockSpec((3, tf), lambda i, j: (0, nj + j)),
            pl.BlockSpec((1, tf), lambda i, j: (0, j)),
            pl.BlockSpec((1, tf), lambda i, j: (0, nj + j)),
            pl.BlockSpec((tf, d), lambda i, j: (j, 0)),
        ],
        out_specs=pl.BlockSpec((tm, d), lambda i, j: (i, 0)),
        out_shape=jax.ShapeDtypeStruct((rows, d), F32),
        scratch_shapes=[pltpu.VMEM((tm + 2 * HALO, d), BF16),
                        pltpu.VMEM((2, 2 * (tf // FFN_COL_CHUNK), tm + 2 * HALO, FFN_COL_CHUNK), BF16)],
        compiler_params=_cparams("parallel", "arbitrary"),
        name="ffn",
    )(h, h, h, norm_w.reshape(1, d), w_up, w_up, conv_w, conv_w,
      conv_b.reshape(1, 2 * dff), conv_b.reshape(1, 2 * dff), w_down)


def _final_norm_kernel(h_ref, nw_ref, o_ref):
    x = h_ref[...]
    ms = jnp.mean(x * x, axis=-1, keepdims=True)
    o_ref[0] = x * lax.rsqrt(ms + EPS) * nw_ref[...]


def _final_norm(h, norm_w, first_row, batch, length, tr=ROW_ALIGN):
    d = h.shape[1]
    per_seq = (ROW_ALIGN + length) // tr
    b0 = (first_row + ROW_ALIGN) // tr
    return pl.pallas_call(
        _final_norm_kernel,
        grid=(batch, length // tr),
        in_specs=[
            pl.BlockSpec((tr, d), lambda b, t: (b0 + b * per_seq + t, 0)),
            pl.BlockSpec((1, d), lambda b, t: (0, 0)),
        ],
        out_specs=pl.BlockSpec((1, tr, d), lambda b, t: (b, t, 0)),
        out_shape=jax.ShapeDtypeStruct((batch, length, d), F32),
        compiler_params=_cparams("parallel", "parallel"),
        name="final_norm",
    )(h, norm_w.reshape(1, d))


def _make_layout(groups):
    seqs, valid, row = [], [], 0
    for batch, length in groups:
        assert length % ROW_ALIGN == 0
        for _ in range(batch):
            n = ROW_ALIGN + length
            seqs.append((row, n))
            valid.append((row + FRONT_PAD, row + n))
            row += n
    rows = (row // ROW_TILE + 1) * ROW_TILE
    spans = tuple(seqs) + ((row, rows - row),)
    return dict(seqs=tuple(seqs), spans=spans, valid=tuple(valid), used_rows=row, rows=rows)


def kernel(x_prompt, x_sample, meta_tokens, mix_norm_w, w_in, ssd_conv_w, ssd_conv_b, ssd_dt_bias,
           ssd_a_log, ssd_d, ssd_norm_w, ssd_w_out, gla_gate_up, gla_gate_b, gla_norm_w, gla_w_out,
           w_mix_out, ffn_norm_w, ffn_w_up, ffn_conv_w, ffn_conv_b, ffn_w_down, final_norm_w):
    depth, d = mix_norm_w.shape
    groups = [(x_prompt.shape[0], x_prompt.shape[1]), (x_sample.shape[0], x_sample.shape[1])]
    layout = _make_layout(groups)
    rows = layout["rows"]

    n_heads = ssd_dt_bias.shape[-1]
    d_inner = n_heads * SSD_HEAD_DIM
    conv_dim = ssd_conv_w.shape[-1]
    d_k = gla_gate_up.shape[-1]
    d_v = gla_w_out.shape[1]
    dk, dv = d_k // GLA_N_HEADS, d_v // GLA_N_HEADS
    sizes = (d_inner, conv_dim, n_heads, d_k, d_k, d_v, d_v, 2 * GLA_GATE_RANK, 2 * d)
    off = np.concatenate([[0], np.cumsum(sizes)])
    assert off[-1] == w_in.shape[-1]
    wide = [0, 1, 3, 4, 5, 6, 8]
    col, pos = {}, 0
    for s in wide:
        col[s] = pos
        pos += sizes[s]

    zero_front = jnp.zeros((FRONT_PAD, d), F32)
    pieces = []
    for x in (x_prompt, x_sample):
        for b in range(x.shape[0]):
            pieces += [zero_front, meta_tokens.astype(F32), x[b]]
    if rows > layout["used_rows"]:
        pieces.append(jnp.zeros((rows - layout["used_rows"], d), F32))
    h = jnp.concatenate(pieces, axis=0)

    for i in range(depth):
        w = w_in[i]
        w_big = jnp.concatenate([w[:, off[s]:off[s + 1]] for s in wide], axis=1).astype(BF16)
        w_small = jnp.concatenate(
            [w[:, off[2]:off[3]], w[:, off[7]:off[8]],
             jnp.zeros((d, LANES - n_heads - 2 * GLA_GATE_RANK), F32)], axis=1).astype(BF16)

        big, small = _inproj(h, mix_norm_w[i], w_big, w_small)
        xbc = _ssd_conv(big, ssd_conv_w[i], ssd_conv_b[i], col[1])
        y_fwd, y_bwd = _ssd_scan(xbc, small, ssd_dt_bias[i], ssd_a_log[i], layout)
        o_fwd, o_bwd = (_gla_scan(big, small, gla_gate_up[i, r], gla_gate_b[i, r], layout,
                                  col[3], col[4], col[5], dk, dv, reverse=bool(r)) for r in (0, 1))
        merged = _merge(y_fwd, y_bwd, xbc, big, o_fwd, o_bwd, ssd_d[i], ssd_norm_w[i], gla_norm_w[i],
                        ssd_w_out[i].astype(BF16), gla_w_out[i].astype(BF16), col[6], col[8])
        h = _mm_res([merged], [w_mix_out[i].astype(BF16)], h, n_split=1)
        h = _ffn(h, ffn_norm_w[i], ffn_w_up[i].astype(BF16), ffn_conv_w[i], ffn_conv_b[i],
                 ffn_w_down[i].astype(BF16), layout, tm=FFN_ROW_TILE)

    outs, si = [], 0
    for batch, length in groups:
        outs.append(_final_norm(h, final_norm_w, layout["seqs"][si][0], batch, length))
        si += batch
    return tuple(outs)
```

```python
import functools

import numpy as np
import jax
import jax.numpy as jnp
from jax import lax
from jax.experimental import pallas as pl
from jax.experimental.pallas import tpu as pltpu

F32 = jnp.float32
BF16 = jnp.bfloat16

N_META = 16
EPS = 1e-6
SSD_HEAD_DIM = 64
SSD_N_GROUPS = 8
SSD_HEADS_PER_GROUP = 4
SSD_D_STATE = 128
GLA_N_HEADS = 4
GLA_GATE_RANK = 16
GLA_GATE_TAU = 16.0

LANES = 128
ROW_ALIGN = 256
FRONT_PAD = ROW_ALIGN - N_META
SSD_CHUNK = 128
GLA_CHUNK = 256
ROW_TILE = 1024
PROLOGUE_ROWS = 128
FFN_COL_CHUNK = 256
HALO = 16
VMEM_LIMIT = 56 * 1024 * 1024
NEG_INF = float("-inf")


def _cparams(*sem):
    return pltpu.CompilerParams(dimension_semantics=sem, vmem_limit_bytes=VMEM_LIMIT)


def _softplus(x):
    return jnp.maximum(x, 0.0) + jnp.log(1.0 + jnp.exp(-jnp.abs(x)))


def _sigmoid(x):
    return 0.5 + 0.5 * jnp.tanh(0.5 * x)


def _silu(x):
    hx = 0.5 * x
    return hx + hx * jnp.tanh(hx)


def _split3(x):
    x1 = x.astype(BF16)
    r1 = x - x1.astype(F32)
    x2 = r1.astype(BF16)
    x3 = (r1 - x2.astype(F32)).astype(BF16)
    return x1, x2, x3


def _dot(a, b):
    return jnp.dot(a, b, preferred_element_type=F32)


def _dot_nt(a, b):
    return lax.dot_general(a, b, (((1,), (1,)), ((), ())), preferred_element_type=F32)


def _dot_tn(a, b):
    return lax.dot_general(a, b, (((0,), (0,)), ((), ())), preferred_element_type=F32)


def _dot_exact_lhs(t, x):
    x1, x2, x3 = _split3(x)
    return _dot(t, x1) + _dot(t, x2) + _dot(t, x3)


def _dot_split2_lhs(t, x):
    x1 = x.astype(BF16)
    x2 = (x - x1.astype(F32)).astype(BF16)
    return _dot(t, x1) + _dot(t, x2)


def _dot_exact_rhs(x, t):
    x1, x2, x3 = _split3(x)
    return _dot(x1, t) + _dot(x2, t) + _dot(x3, t)


def _row_valid(row0, n, ranges):
    r = row0 + lax.broadcasted_iota(jnp.int32, (n, 1), 0)
    ok = None
    for lo, hi in ranges:
        m = (r >= lo) & (r < hi)
        ok = m if ok is None else (ok | m)
    return ok


def _is_any(c, values):
    ok = None
    for v in values:
        m = c == v
        ok = m if ok is None else (ok | m)
    return ok


def _inproj_kernel(h_ref, prev_ref, next_ref, nw_ref, wb_ref, ws_ref, cw_ref, cb_ref, big_ref,
                   small_ref, xn_scr, *, conv_blocks):
    tm = h_ref.shape[0]
    n = tm + 2 * HALO
    j = pl.program_id(1)

    def norm(x):
        ms = jnp.mean(x * x, axis=-1, keepdims=True)
        return (x * lax.rsqrt(ms + EPS) * nw_ref[...]).astype(BF16)

    @pl.when(j == 0)
    def _():
        xn_scr[0:HALO, :] = norm(prev_ref[...])
        for r0 in range(0, tm, PROLOGUE_ROWS):
            xn_scr[HALO + r0:HALO + r0 + PROLOGUE_ROWS, :] = norm(h_ref[r0:r0 + PROLOGUE_ROWS, :])
        xn_scr[HALO + tm:2 * HALO + tm, :] = norm(next_ref[...])
        small_ref[...] = _dot(xn_scr[HALO:HALO + tm, :], ws_ref[...])

    is_conv = (j >= conv_blocks[0]) & (j < conv_blocks[1])

    @pl.when(is_conv)
    def _():
        xn = xn_scr[...]
        for c0 in range(0, big_ref.shape[1], FFN_COL_CHUNK):
            cs = slice(c0, c0 + FFN_COL_CHUNK)
            u = _dot(xn, wb_ref[:, cs])
            y = cb_ref[:, cs] + pltpu.roll(u, 1, 0)[HALO:HALO + tm] * cw_ref[0:1, cs]
            y = y + u[HALO:HALO + tm] * cw_ref[1:2, cs]
            y = y + pltpu.roll(u, n - 1, 0)[HALO:HALO + tm] * cw_ref[2:3, cs]
            big_ref[:, cs] = _silu(y).astype(BF16)

    @pl.when(jnp.logical_not(is_conv))
    def _():
        big_ref[...] = _dot(xn_scr[HALO:HALO + tm, :], wb_ref[...]).astype(BF16)


def _inproj(h, norm_w, w_big, w_small, conv_w, conv_b, conv_col0, tn=2048):
    rows, d = h.shape
    nb = w_big.shape[1]
    tm = ROW_TILE
    hb = tm // HALO
    last = rows // HALO - 1
    c = conv_w.shape[1]
    conv_blocks = (conv_col0 // tn, (conv_col0 + c) // tn)
    n_conv = conv_blocks[1] - conv_blocks[0]

    def conv_map(i, j):
        return (0, jnp.clip(j - conv_blocks[0], 0, n_conv - 1))

    kern = functools.partial(_inproj_kernel, conv_blocks=conv_blocks)
    return pl.pallas_call(
        kern,
        grid=(rows // tm, nb // tn),
        in_specs=[
            pl.BlockSpec((tm, d), lambda i, j: (i, 0)),
            pl.BlockSpec((HALO, d), lambda i, j: (jnp.maximum(i * hb - 1, 0), 0)),
            pl.BlockSpec((HALO, d), lambda i, j: (jnp.minimum((i + 1) * hb, last), 0)),
            pl.BlockSpec((1, d), lambda i, j: (0, 0)),
            pl.BlockSpec((d, tn), lambda i, j: (0, j)),
            pl.BlockSpec((d, LANES), lambda i, j: (0, 0)),
            pl.BlockSpec((3, tn), conv_map),
            pl.BlockSpec((1, tn), conv_map),
        ],
        out_specs=[
            pl.BlockSpec((tm, tn), lambda i, j: (i, j)),
            pl.BlockSpec((tm, LANES), lambda i, j: (i, 0)),
        ],
        out_shape=[
            jax.ShapeDtypeStruct((rows, nb), BF16),
            jax.ShapeDtypeStruct((rows, LANES), F32),
        ],
        scratch_shapes=[pltpu.VMEM((tm + 2 * HALO, d), BF16)],
        compiler_params=_cparams("parallel", "arbitrary"),
        name="inproj",
    )(h, h, h, norm_w.reshape(1, d), w_big, w_small, conv_w, conv_b.reshape(1, c))


def _ssd_chunk(d, c, x_ref, b_ref, c_ref, small_ref, dtb_ref, alog_ref, tri_ref, mask_ref, exp_ref,
               y_ref, state_scr, valid):
    ch = SSD_CHUNK
    ok = _row_valid(c * ch, ch, valid)
    dt = _softplus(small_ref[...] + dtb_ref[d])
    dt = jnp.where(ok, dt, 0.0)
    a = -jnp.exp(alog_ref[d]) * dt
    acs = _dot_exact_lhs(tri_ref[d], a)
    acs_t = _dot_exact_rhs(a.T, tri_ref[1 - d])
    dt_t = dt.T
    total = jnp.sum(a, axis=0, keepdims=True)
    w_in = jnp.exp(total - acs) * dt
    per_head = jnp.concatenate(
        [w_in, jnp.exp(acs), jnp.broadcast_to(jnp.exp(total), (8, LANES))], axis=0)
    hi = per_head.astype(BF16)
    lo = (per_head - hi.astype(F32)).astype(BF16)
    per_lane = _dot(jnp.concatenate([hi, lo], axis=1), exp_ref[...])
    w_exp = per_lane[0:ch]
    ea_exp = per_lane[ch:2 * ch]
    et_exp = per_lane[2 * ch:2 * ch + 1]
    mask = mask_ref[d]

    gw = SSD_HEADS_PER_GROUP * SSD_HEAD_DIM
    lane = lax.broadcasted_iota(jnp.int32, (1, gw), 1)
    for grp in range(SSD_N_GROUPS):
        bg = b_ref[:, grp * SSD_D_STATE:(grp + 1) * SSD_D_STATE]
        cg = c_ref[:, grp * SSD_D_STATE:(grp + 1) * SSD_D_STATE]
        xg = x_ref[:, grp * gw:(grp + 1) * gw]
        cb = _dot_nt(cg, bg)
        st = state_scr[d, grp]
        y = _dot(cg, st.astype(BF16)) * ea_exp[:, grp * gw:(grp + 1) * gw]
        ms, xs = [], []
        for hh in range(SSD_HEADS_PER_GROUP):
            h = grp * SSD_HEADS_PER_GROUP + hh
            seg = acs[:, h:h + 1] - acs_t[h:h + 1, :] + mask
            ms.append((cb * jnp.exp(seg) * dt_t[h:h + 1, :]).astype(BF16))
            in_head = (lane >= hh * SSD_HEAD_DIM) & (lane < (hh + 1) * SSD_HEAD_DIM)
            xs.append(jnp.where(in_head, xg, jnp.zeros_like(xg)))
        y = y + _dot(jnp.concatenate(ms, axis=1), jnp.concatenate(xs, axis=0))
        y_ref[:, grp * gw:(grp + 1) * gw] = y.astype(BF16)
        xw = (xg.astype(F32) * w_exp[:, grp * gw:(grp + 1) * gw]).astype(BF16)
        state_scr[d, grp] = st * et_exp[:, grp * gw:(grp + 1) * gw] + _dot_tn(bg, xw)


def _ssd_scan_kernel(xf_ref, bf_ref, cf_ref, sf_ref, xb_ref, bb_ref, cb_ref, sb_ref, dtb_ref,
                     alog_ref, tri_ref, mask_ref, exp_ref, yf_ref, yb_ref, state_scr, *, n_chunks,
                     firsts, lasts, valid):
    g = pl.program_id(0)
    r = n_chunks - 1 - g

    @pl.when(_is_any(g, firsts))
    def _():
        state_scr[0] = jnp.zeros(state_scr.shape[1:], F32)

    @pl.when(_is_any(r, lasts))
    def _():
        state_scr[1] = jnp.zeros(state_scr.shape[1:], F32)

    shared = (dtb_ref, alog_ref, tri_ref, mask_ref, exp_ref)
    _ssd_chunk(0, g, xf_ref, bf_ref, cf_ref, sf_ref, *shared, yf_ref, state_scr, valid)
    _ssd_chunk(1, r, xb_ref, bb_ref, cb_ref, sb_ref, *shared, yb_ref, state_scr, valid)


def _ssd_scan(xbc, col_x, small, dt_bias, a_log, layout):
    rows = xbc.shape[0]
    ch = SSD_CHUNK
    d_inner = SSD_N_GROUPS * SSD_HEADS_PER_GROUP * SSD_HEAD_DIM
    gn = SSD_N_GROUPS * SSD_D_STATE
    n_heads = SSD_N_GROUPS * SSD_HEADS_PER_GROUP
    n_chunks = rows // ch
    firsts = tuple(s // ch for s, _ in layout["spans"])
    lasts = tuple((s + n) // ch - 1 for s, n in layout["spans"])

    idx = np.arange(ch)
    lower = (idx[:, None] >= idx[None, :])
    tri = np.stack([lower, lower.T]).astype(np.float32)
    mask = np.where(tri > 0, 0.0, NEG_INF).astype(np.float32)
    expand = np.zeros((2 * LANES, d_inner), np.float32)
    for h in range(n_heads):
        expand[h, h * SSD_HEAD_DIM:(h + 1) * SSD_HEAD_DIM] = 1.0
        expand[LANES + h, h * SSD_HEAD_DIM:(h + 1) * SSD_HEAD_DIM] = 1.0
    pad = ((0, 0), (0, 0), (0, LANES - n_heads))
    dtb = jnp.pad(dt_bias.reshape(2, 1, n_heads), pad)
    alog = jnp.pad(a_log.reshape(2, 1, n_heads), pad)

    def chunk_specs(cmap):
        return [
            pl.BlockSpec((ch, d_inner), lambda g: (cmap(g), col_x // d_inner)),
            pl.BlockSpec((ch, gn), lambda g: (cmap(g), (col_x + d_inner) // gn)),
            pl.BlockSpec((ch, gn), lambda g: (cmap(g), (col_x + d_inner) // gn + 1)),
            pl.BlockSpec((ch, LANES), lambda g: (cmap(g), 0)),
        ]

    def fwd(g):
        return g

    def rev(g):
        return n_chunks - 1 - g

    kern = functools.partial(_ssd_scan_kernel, n_chunks=n_chunks, firsts=firsts, lasts=lasts,
                             valid=layout["valid"])
    return pl.pallas_call(
        kern,
        grid=(n_chunks,),
        in_specs=chunk_specs(fwd) + chunk_specs(rev) + [
            pl.BlockSpec((2, 1, LANES), lambda g: (0, 0, 0)),
            pl.BlockSpec((2, 1, LANES), lambda g: (0, 0, 0)),
            pl.BlockSpec((2, ch, ch), lambda g: (0, 0, 0)),
            pl.BlockSpec((2, ch, ch), lambda g: (0, 0, 0)),
            pl.BlockSpec((2 * LANES, d_inner), lambda g: (0, 0)),
        ],
        out_specs=[pl.BlockSpec((ch, d_inner), lambda g: (fwd(g), 0)),
                   pl.BlockSpec((ch, d_inner), lambda g: (rev(g), 0))],
        out_shape=[jax.ShapeDtypeStruct((rows, d_inner), BF16)] * 2,
        scratch_shapes=[
            pltpu.VMEM((2, SSD_N_GROUPS, SSD_D_STATE, SSD_HEADS_PER_GROUP * SSD_HEAD_DIM), F32)],
        compiler_params=_cparams("arbitrary"),
        name="ssd_scan",
    )(xbc, xbc, xbc, small, xbc, xbc, xbc, small, dtb, alog, jnp.asarray(tri, BF16),
      jnp.asarray(mask), jnp.asarray(expand, BF16))


def _gla_levels():
    lv, m = [], GLA_CHUNK // 2
    while m >= 1:
        lv.append(m)
        m //= 2
    return tuple(lv)


def _gla_level_exponent(cs_scr, cs_t, t, m, reverse):
    ch = cs_t.shape[0]
    off = m - 1 if reverse else m

    def ref_tile(r):
        return jnp.broadcast_to(cs_scr[t, pl.ds(r, 1), :], (8, LANES))

    if m >= 8:
        pieces = []
        for p in range(ch // (2 * m)):
            lo = 2 * m * p
            ref = jnp.concatenate([ref_tile(lo + off)] * (m // 8), axis=0)
            early, late = cs_t[lo:lo + m], cs_t[lo + m:lo + 2 * m]
            pieces += [early - ref, ref - late] if reverse else [ref - early, late - ref]
        return jnp.concatenate(pieces, axis=0)

    row = lax.broadcasted_iota(jnp.int32, (ch, 1), 0)
    in_late = ((row // m) & 1) == 1
    is_target = in_late != reverse
    if m == 1:
        x3 = cs_t.reshape(ch // 8, 8, LANES)
        nb = pltpu.roll(x3, 1 if reverse else 7, 1).reshape(ch, LANES)
        return jnp.where(is_target, 0.0, nb - cs_t)
    sub = lax.broadcasted_iota(jnp.int32, (8, 1), 0)
    pieces = []
    for p in range(ch // 8):
        ref = ref_tile(8 * p + off)
        for q in range(1, 8 // (2 * m)):
            ref = jnp.where(sub >= 2 * m * q, ref_tile(8 * p + 2 * m * q + off), ref)
        pieces.append(ref)
    delta = cs_t - jnp.concatenate(pieces, axis=0)
    return jnp.where(is_target, delta, -delta)


def _gla_level_operand(q_scr, k_ref, sl, m, reverse):
    ch = q_scr.shape[0]
    if m >= 16:
        pieces = []
        for p in range(ch // (2 * m)):
            lo = 2 * m * p
            early, late = slice(lo, lo + m), slice(lo + m, lo + 2 * m)
            if reverse:
                pieces += [q_scr[early, sl], k_ref[late, sl]]
            else:
                pieces += [k_ref[early, sl], q_scr[late, sl]]
        return jnp.concatenate(pieces, axis=0)
    row = lax.broadcasted_iota(jnp.int32, (ch, 1), 0)
    is_target = (((row // m) & 1) == 1) != reverse
    return jnp.where(is_target, q_scr[:, sl], k_ref[:, sl])


def _gla_scan_kernel(q_ref, k_ref, v_ref, small_ref, gu_ref, gb_ref, tri_ref, lev_ref, o_ref,
                     state_scr, cs_scr, att_scr, diag_scr, q_scr, *, reverse, n_chunks, firsts, lasts,
                     valid, dk, dv):
    g = pl.program_id(0)
    c = n_chunks - 1 - g if reverse else g
    ch = GLA_CHUNK
    levels = _gla_levels()
    tiles_per_head = dk // LANES

    @pl.when(_is_any(c, lasts if reverse else firsts))
    def _():
        state_scr[...] = jnp.zeros_like(state_scr)

    ok = _row_valid(c * ch, ch, valid)
    u = _dot(small_ref[...].astype(BF16), gu_ref[...]) + gb_ref[...]
    gate = jnp.where(ok, -_softplus(-u) / GLA_GATE_TAU, 0.0)
    cs = _dot_split2_lhs(tri_ref[...], gate)
    n_tiles = cs.shape[1] // LANES
    for t in range(n_tiles):
        cs_scr[t] = cs[:, t * LANES:(t + 1) * LANES]
    total = jnp.sum(gate, axis=0, keepdims=True)
    lev = lev_ref[...]
    q_scr[...] = q_ref[...] * jnp.asarray(dk ** -0.5, BF16)

    half = ch // 2
    tgt, src = (slice(0, half), slice(half, ch)) if reverse else (slice(half, ch), slice(0, half))
    quads = (slice(0, half), slice(half, ch))

    @pl.when(g == 0)
    def _():
        att_scr[:, src, tgt] = jnp.zeros((GLA_N_HEADS, half, half), BF16)

    for i, m in enumerate(levels):
        e = [jnp.exp(_gla_level_exponent(cs_scr, cs[:, t * LANES:(t + 1) * LANES], t, m, reverse)
                     ).astype(BF16) for t in range(n_tiles)]
        for h in range(GLA_N_HEADS):
            sl = slice(h * dk, (h + 1) * dk)
            e_h = jnp.concatenate(e[h * tiles_per_head:(h + 1) * tiles_per_head], axis=1)
            x = _gla_level_operand(q_scr, k_ref, sl, m, reverse) * e_h
            if i == 0:
                att_scr[h, tgt, src] = _dot_nt(x[tgt], x[src]).astype(BF16)
                continue
            for qi, qs in enumerate(quads):
                p = _dot_nt(x[qs], x[qs])
                diag_scr[h, qi] = jnp.where(lev == i, p, 0.0 if i == 1 else diag_scr[h, qi])

    for h in range(GLA_N_HEADS):
        sl = slice(h * dk, (h + 1) * dk)
        vs = slice(h * dv, (h + 1) * dv)
        vh = v_ref[:, vs]
        for qi, qs in enumerate(quads):
            p = _dot_nt(q_scr[qs, sl], k_ref[qs, sl])
            att_scr[h, qs, qs] = jnp.where(lev == len(levels), p, diag_scr[h, qi]).astype(BF16)
        st = state_scr[h]
        qd = q_scr[:, sl] * jnp.exp(cs[:, sl]).astype(BF16)
        o = _dot(att_scr[h], vh) + _dot_nt(qd, st.astype(BF16))
        o_ref[:, vs] = o.astype(BF16)
        kd = k_ref[:, sl] * jnp.exp(total[:, sl] - cs[:, sl]).astype(BF16)
        state_scr[h] = st * jnp.exp(total[:, sl]) + _dot_tn(vh, kd)


def _gla_scan(big, small, gate_up, gate_b, layout, col_q, col_k, col_v, dk, dv, reverse):
    rows = big.shape[0]
    ch = GLA_CHUNK
    n_chunks = rows // ch
    firsts = tuple(s // ch for s, _ in layout["spans"])
    lasts = tuple((s + n) // ch - 1 for s, n in layout["spans"])
    hk = GLA_N_HEADS * dk
    hv = GLA_N_HEADS * dv
    levels = _gla_levels()

    idx = np.arange(ch)
    l_, s_ = idx[:, None], idx[None, :]
    seen = (l_ <= s_) if reverse else (l_ >= s_)
    x = l_ ^ s_
    top = np.floor(np.log2(np.maximum(x, 1))).astype(np.int64)
    lvl = np.where(x == 0, len(levels), (len(levels) - 1) - top)
    lev = np.where(seen, lvl, -1).astype(np.int32)[:ch // 2, :ch // 2]

    lo = 32 + GLA_GATE_RANK * int(reverse)
    gu = jnp.zeros((LANES, hk), F32).at[lo:lo + GLA_GATE_RANK].set(gate_up).astype(BF16)

    def cmap(g):
        return n_chunks - 1 - g if reverse else g

    kern = functools.partial(_gla_scan_kernel, reverse=reverse, n_chunks=n_chunks, firsts=firsts,
                             lasts=lasts, valid=layout["valid"], dk=dk, dv=dv)
    return pl.pallas_call(
        kern,
        grid=(n_chunks,),
        in_specs=[
            pl.BlockSpec((ch, hk), lambda g: (cmap(g), col_q // hk)),
            pl.BlockSpec((ch, hk), lambda g: (cmap(g), col_k // hk)),
            pl.BlockSpec((ch, hv), lambda g: (cmap(g), col_v // hv)),
            pl.BlockSpec((ch, LANES), lambda g: (cmap(g), 0)),
            pl.BlockSpec((LANES, hk), lambda g: (0, 0)),
            pl.BlockSpec((1, hk), lambda g: (0, 0)),
            pl.BlockSpec((ch, ch), lambda g: (0, 0)),
            pl.BlockSpec((ch // 2, ch // 2), lambda g: (0, 0)),
        ],
        out_specs=pl.BlockSpec((ch, hv), lambda g: (cmap(g), 0)),
        out_shape=jax.ShapeDtypeStruct((rows, hv), BF16),
        scratch_shapes=[
            pltpu.VMEM((GLA_N_HEADS, dv, dk), F32),
            pltpu.VMEM((hk // LANES, ch, LANES), F32),
            pltpu.VMEM((GLA_N_HEADS, ch, ch), BF16),
            pltpu.VMEM((GLA_N_HEADS, 2, ch // 2, ch // 2), F32),
            pltpu.VMEM((ch, hk), BF16),
        ],
        compiler_params=_cparams("arbitrary"),
        name="gla_scan_bwd" if reverse else "gla_scan_fwd",
    )(big, big, big, small, gu, gate_b.reshape(1, hk), jnp.asarray(seen.astype(np.float32), BF16),
      jnp.asarray(lev))


def _merge_kernel(ysf_ref, ysb_ref, xs_ref, z_ref, ogf_ref, ogb_ref, ogate_ref, ms_ref, mg_ref, dskip_ref,
                  snw_ref, gnw_ref, ws_ref, wg_ref, o_ref, a_ssd, a_gla, *, dv):
    s = pl.program_id(0)
    tm = a_ssd.shape[1]

    @pl.when(s == 0)
    def _():
        a_ssd[1] = jnp.zeros(a_ssd.shape[1:], BF16)
        a_gla[1] = jnp.zeros(a_gla.shape[1:], BF16)

    slot = s % 2
    for r0 in range(0, tm, PROLOGUE_ROWS):
        rs = slice(r0, r0 + PROLOGUE_ROWS)
        y = ysf_ref[rs, :].astype(F32) + ysb_ref[rs, :].astype(F32)
        y = y + xs_ref[rs, :].astype(F32) * dskip_ref[...]
        y = y * _silu(z_ref[rs, :].astype(F32))
        ms = jnp.mean(y * y, axis=-1, keepdims=True)
        a_ssd[slot, rs, :] = (y * lax.rsqrt(ms + EPS) * snw_ref[...]).astype(BF16)
        for h in range(GLA_N_HEADS):
            sl = slice(h * dv, (h + 1) * dv)
            o = ogf_ref[rs, sl].astype(F32) + ogb_ref[rs, sl].astype(F32)
            ms = jnp.mean(o * o, axis=-1, keepdims=True)
            o = o * lax.rsqrt(ms + EPS) * gnw_ref[...]
            a_gla[slot, rs, sl] = (o * _silu(ogate_ref[rs, sl].astype(F32))).astype(BF16)

    y_ssd = _dot(a_ssd[1 - slot], ws_ref[...])
    y_gla = _dot(a_gla[1 - slot], wg_ref[...])
    merged = _sigmoid(ms_ref[...].astype(F32)) * y_ssd
    merged = merged + _sigmoid(mg_ref[...].astype(F32)) * y_gla
    o_ref[...] = merged.astype(BF16)


def _merge(y_fwd, y_bwd, big, o_fwd, o_bwd, d_skip, ssd_norm_w, gla_norm_w, w_ssd, w_gla, col_x, col_og,
           col_merge, tm=256):
    rows, d = big.shape[0], w_ssd.shape[1]
    di = w_ssd.shape[0]
    dvt = w_gla.shape[0]
    dv = dvt // GLA_N_HEADS
    n_tiles = rows // tm
    kern = functools.partial(_merge_kernel, dv=dv)

    def cur(s):
        return jnp.minimum(s, n_tiles - 1)

    def prev(s):
        return jnp.maximum(s - 1, 0)

    resident = pl.Buffered(1)
    return pl.pallas_call(
        kern,
        grid=(n_tiles + 1,),
        in_specs=[
            pl.BlockSpec((tm, di), lambda s: (cur(s), 0)),
            pl.BlockSpec((tm, di), lambda s: (cur(s), 0)),
            pl.BlockSpec((tm, di), lambda s: (cur(s), col_x // di)),
            pl.BlockSpec((tm, di), lambda s: (cur(s), 0)),
            pl.BlockSpec((tm, dvt), lambda s: (cur(s), 0)),
            pl.BlockSpec((tm, dvt), lambda s: (cur(s), 0)),
            pl.BlockSpec((tm, dvt), lambda s: (cur(s), col_og // dvt)),
            pl.BlockSpec((tm, d), lambda s: (prev(s), col_merge // d)),
            pl.BlockSpec((tm, d), lambda s: (prev(s), col_merge // d + 1)),
            pl.BlockSpec((1, di), lambda s: (0, 0)),
            pl.BlockSpec((1, di), lambda s: (0, 0)),
            pl.BlockSpec((1, dv), lambda s: (0, 0)),
            pl.BlockSpec((di, d), lambda s: (0, 0), pipeline_mode=resident),
            pl.BlockSpec((dvt, d), lambda s: (0, 0), pipeline_mode=resident),
        ],
        out_specs=pl.BlockSpec((tm, d), lambda s: (prev(s), 0)),
        out_shape=jax.ShapeDtypeStruct((rows, d), BF16),
        scratch_shapes=[pltpu.VMEM((2, tm, di), BF16), pltpu.VMEM((2, tm, dvt), BF16)],
        compiler_params=_cparams("arbitrary"),
        name="merge",
    )(y_fwd, y_bwd, big, big, o_fwd, o_bwd, big, big, big,
      jnp.repeat(d_skip, SSD_HEAD_DIM).reshape(1, di), ssd_norm_w.reshape(1, di),
      gla_norm_w.reshape(1, dv), w_ssd, w_gla)


def _mm_res_kernel(*refs):
    *aw_refs, h_ref, o_ref = refs
    n_terms = len(aw_refs) // 2
    acc = h_ref[...]
    for a_ref, w_ref in zip(aw_refs[:n_terms], aw_refs[n_terms:]):
        acc = acc + _dot(a_ref[...], w_ref[...])
    o_ref[...] = acc


def _mm_res(a_list, w_list, h, n_split, tm=512):
    rows = h.shape[0]
    n = w_list[0].shape[1]
    tn = n // n_split
    a_specs = [pl.BlockSpec((tm, a.shape[1]), lambda j, i: (i, 0)) for a in a_list]
    w_specs = [pl.BlockSpec((w.shape[0], tn), lambda j, i: (0, j)) for w in w_list]
    return pl.pallas_call(
        _mm_res_kernel,
        grid=(n_split, rows // tm),
        in_specs=a_specs + w_specs + [pl.BlockSpec((tm, tn), lambda j, i: (i, j))],
        out_specs=pl.BlockSpec((tm, tn), lambda j, i: (i, j)),
        out_shape=jax.ShapeDtypeStruct((rows, n), F32),
        compiler_params=_cparams("arbitrary", "arbitrary"),
        name="mm_res",
    )(*a_list, *w_list, h)


def _ffn_up_kernel(h_ref, prev_ref, next_ref, nw_ref, wg_ref, wu_ref, cwg_ref, cwu_ref, cbg_ref,
                   cbu_ref, o_ref, xn_scr, *, valid):
    tm = h_ref.shape[0]

    def norm(x):
        ms = jnp.mean(x * x, axis=-1, keepdims=True)
        return (x * lax.rsqrt(ms + EPS) * nw_ref[...]).astype(BF16)

    @pl.when(pl.program_id(1) == 0)
    def _():
        xn_scr[0:HALO, :] = norm(prev_ref[...])
        for r0 in range(0, tm, PROLOGUE_ROWS):
            xn_scr[HALO + r0:HALO + r0 + PROLOGUE_ROWS, :] = norm(h_ref[r0:r0 + PROLOGUE_ROWS, :])
        xn_scr[HALO + tm:2 * HALO + tm, :] = norm(next_ref[...])

    xn = xn_scr[...]
    n = tm + 2 * HALO

    def conv(u, w_ref, b_ref, cs):
        y = b_ref[:, cs] + pltpu.roll(u, 1, 0)[HALO:HALO + tm] * w_ref[0:1, cs]
        y = y + u[HALO:HALO + tm] * w_ref[1:2, cs]
        return y + pltpu.roll(u, n - 1, 0)[HALO:HALO + tm] * w_ref[2:3, cs]

    ok = _row_valid(pl.program_id(0) * tm, tm, valid)
    for c0 in range(0, o_ref.shape[1], FFN_COL_CHUNK):
        cs = slice(c0, c0 + FFN_COL_CHUNK)
        gate = conv(_dot(xn, wg_ref[:, cs]), cwg_ref, cbg_ref, cs)
        up = conv(_dot(xn, wu_ref[:, cs]), cwu_ref, cbu_ref, cs)
        o_ref[:, cs] = jnp.where(ok, _silu(gate) * up, 0.0).astype(BF16)


def _ffn_up(h, norm_w, w_up, conv_w, conv_b, layout, tf=512):
    rows, d = h.shape
    dff = w_up.shape[1] // 2
    tm = ROW_TILE
    hb = tm // HALO
    last = rows // HALO - 1
    nj = dff // tf
    kern = functools.partial(_ffn_up_kernel, valid=layout["valid"])
    return pl.pallas_call(
        kern,
        grid=(rows // tm, nj),
        in_specs=[
            pl.BlockSpec((tm, d), lambda i, j: (i, 0)),
            pl.BlockSpec((HALO, d), lambda i, j: (jnp.maximum(i * hb - 1, 0), 0)),
            pl.BlockSpec((HALO, d), lambda i, j: (jnp.minimum((i + 1) * hb, last), 0)),
            pl.BlockSpec((1, d), lambda i, j: (0, 0)),
            pl.BlockSpec((d, tf), lambda i, j: (0, j)),
            pl.BlockSpec((d, tf), lambda i, j: (0, nj + j)),
            pl.BlockSpec((3, tf), lambda i, j: (0, j)),
            pl.BlockSpec((3, tf), lambda i, j: (0, nj + j)),
            pl.BlockSpec((1, tf), lambda i, j: (0, j)),
            pl.BlockSpec((1, tf), lambda i, j: (0, nj + j)),
        ],
        out_specs=pl.BlockSpec((tm, tf), lambda i, j: (i, j)),
        out_shape=jax.ShapeDtypeStruct((rows, dff), BF16),
        scratch_shapes=[pltpu.VMEM((tm + 2 * HALO, d), BF16)],
        compiler_params=_cparams("parallel", "arbitrary"),
        name="ffn_up",
    )(h, h, h, norm_w.reshape(1, d), w_up, w_up, conv_w, conv_w,
      conv_b.reshape(1, 2 * dff), conv_b.reshape(1, 2 * dff))


def _final_norm_kernel(h_ref, nw_ref, o_ref):
    x = h_ref[...]
    ms = jnp.mean(x * x, axis=-1, keepdims=True)
    o_ref[0] = x * lax.rsqrt(ms + EPS) * nw_ref[...]


def _final_norm(h, norm_w, first_row, batch, length, tr=ROW_ALIGN):
    d = h.shape[1]
    per_seq = (ROW_ALIGN + length) // tr
    b0 = (first_row + ROW_ALIGN) // tr
    return pl.pallas_call(
        _final_norm_kernel,
        grid=(batch, length // tr),
        in_specs=[
            pl.BlockSpec((tr, d), lambda b, t: (b0 + b * per_seq + t, 0)),
            pl.BlockSpec((1, d), lambda b, t: (0, 0)),
        ],
        out_specs=pl.BlockSpec((1, tr, d), lambda b, t: (b, t, 0)),
        out_shape=jax.ShapeDtypeStruct((batch, length, d), F32),
        compiler_params=_cparams("parallel", "parallel"),
        name="final_norm",
    )(h, norm_w.reshape(1, d))


def _make_layout(groups):
    seqs, valid, row = [], [], 0
    for batch, length in groups:
        assert length % ROW_ALIGN == 0
        for _ in range(batch):
            n = ROW_ALIGN + length
            seqs.append((row, n))
            valid.append((row + FRONT_PAD, row + n))
            row += n
    rows = (row // ROW_TILE + 1) * ROW_TILE
    spans = tuple(seqs) + ((row, rows - row),)
    return dict(seqs=tuple(seqs), spans=spans, valid=tuple(valid), used_rows=row, rows=rows)


def kernel(x_prompt, x_sample, meta_tokens, mix_norm_w, w_in, ssd_conv_w, ssd_conv_b, ssd_dt_bias,
           ssd_a_log, ssd_d, ssd_norm_w, ssd_w_out, gla_gate_up, gla_gate_b, gla_norm_w, gla_w_out,
           w_mix_out, ffn_norm_w, ffn_w_up, ffn_conv_w, ffn_conv_b, ffn_w_down, final_norm_w):
    depth, d = mix_norm_w.shape
    groups = [(x_prompt.shape[0], x_prompt.shape[1]), (x_sample.shape[0], x_sample.shape[1])]
    layout = _make_layout(groups)
    rows = layout["rows"]

    n_heads = ssd_dt_bias.shape[-1]
    d_inner = n_heads * SSD_HEAD_DIM
    conv_dim = ssd_conv_w.shape[-1]
    d_k = gla_gate_up.shape[-1]
    d_v = gla_w_out.shape[1]
    dk, dv = d_k // GLA_N_HEADS, d_v // GLA_N_HEADS
    sizes = (d_inner, conv_dim, n_heads, d_k, d_k, d_v, d_v, 2 * GLA_GATE_RANK, 2 * d)
    off = np.concatenate([[0], np.cumsum(sizes)])
    assert off[-1] == w_in.shape[-1]
    wide = [0, 1, 3, 4, 5, 6, 8]
    col, pos = {}, 0
    for s in wide:
        col[s] = pos
        pos += sizes[s]

    zero_front = jnp.zeros((FRONT_PAD, d), F32)
    pieces = []
    for x in (x_prompt, x_sample):
        for b in range(x.shape[0]):
            pieces += [zero_front, meta_tokens.astype(F32), x[b]]
    if rows > layout["used_rows"]:
        pieces.append(jnp.zeros((rows - layout["used_rows"], d), F32))
    h = jnp.concatenate(pieces, axis=0)

    for i in range(depth):
        w = w_in[i]
        w_big = jnp.concatenate([w[:, off[s]:off[s + 1]] for s in wide], axis=1).astype(BF16)
        w_small = jnp.concatenate(
            [w[:, off[2]:off[3]], w[:, off[7]:off[8]],
             jnp.zeros((d, LANES - n_heads - 2 * GLA_GATE_RANK), F32)], axis=1).astype(BF16)

        big, small = _inproj(h, mix_norm_w[i], w_big, w_small, ssd_conv_w[i], ssd_conv_b[i], col[1])
        y_fwd, y_bwd = _ssd_scan(big, col[1], small, ssd_dt_bias[i], ssd_a_log[i], layout)
        o_fwd, o_bwd = (_gla_scan(big, small, gla_gate_up[i, r], gla_gate_b[i, r], layout,
                                  col[3], col[4], col[5], dk, dv, reverse=bool(r)) for r in (0, 1))
        merged = _merge(y_fwd, y_bwd, big, o_fwd, o_bwd, ssd_d[i], ssd_norm_w[i], gla_norm_w[i],
                        ssd_w_out[i].astype(BF16), gla_w_out[i].astype(BF16), col[1], col[6], col[8])
        h = _mm_res([merged], [w_mix_out[i].astype(BF16)], h, n_split=1)
        act = _ffn_up(h, ffn_norm_w[i], ffn_w_up[i].astype(BF16), ffn_conv_w[i], ffn_conv_b[i],
                      layout)
        h = _mm_res([act], [ffn_w_down[i].astype(BF16)], h, n_split=2)

    outs, si = [], 0
    for batch, length in groups:
        outs.append(_final_norm(h, final_norm_w, layout["seqs"][si][0], batch, length))
        si += batch
    return tuple(outs)
```

```python
import functools

import numpy as np
import jax
import jax.numpy as jnp
from jax import lax
from jax.experimental import pallas as pl
from jax.experimental.pallas import tpu as pltpu

F32 = jnp.float32
BF16 = jnp.bfloat16

N_META = 16
EPS = 1e-6
SSD_HEAD_DIM = 64
SSD_N_GROUPS = 8
SSD_HEADS_PER_GROUP = 4
SSD_D_STATE = 128
GLA_N_HEADS = 4
GLA_GATE_RANK = 16
GLA_GATE_TAU = 16.0

LANES = 128
ROW_ALIGN = 256
FRONT_PAD = ROW_ALIGN - N_META
SSD_CHUNK = 128
GLA_CHUNK = 256
ROW_TILE = 1024
PROLOGUE_ROWS = 128
FFN_COL_CHUNK = 256
ROW_PIECES = 3
HALO = 16
VMEM_LIMIT = 56 * 1024 * 1024
NEG_INF = float("-inf")


def _cparams(*sem):
    return pltpu.CompilerParams(dimension_semantics=sem, vmem_limit_bytes=VMEM_LIMIT)


def _softplus(x):
    return jnp.maximum(x, 0.0) + jnp.log(1.0 + jnp.exp(-jnp.abs(x)))


def _sigmoid(x):
    return 0.5 + 0.5 * jnp.tanh(0.5 * x)


def _silu(x):
    hx = 0.5 * x
    return hx + hx * jnp.tanh(hx)


def _split3(x):
    x1 = x.astype(BF16)
    r1 = x - x1.astype(F32)
    x2 = r1.astype(BF16)
    x3 = (r1 - x2.astype(F32)).astype(BF16)
    return x1, x2, x3


def _dot(a, b):
    return jnp.dot(a, b, preferred_element_type=F32)


def _dot_row_pieces(x_ref, w_ref, cs, pieces=ROW_PIECES):
    n = x_ref.shape[0]
    step = n // pieces
    return jnp.concatenate(
        [_dot(x_ref[r0:r0 + step, :], w_ref[:, cs]) for r0 in range(0, n, step)], axis=0)


def _dot_nt(a, b):
    return lax.dot_general(a, b, (((1,), (1,)), ((), ())), preferred_element_type=F32)


def _dot_tn(a, b):
    return lax.dot_general(a, b, (((0,), (0,)), ((), ())), preferred_element_type=F32)


def _dot_exact_lhs(t, x):
    x1, x2, x3 = _split3(x)
    return _dot(t, x1) + _dot(t, x2) + _dot(t, x3)


def _dot_split2_lhs(t, x):
    x1 = x.astype(BF16)
    x2 = (x - x1.astype(F32)).astype(BF16)
    return _dot(t, x1) + _dot(t, x2)


def _dot_exact_rhs(x, t):
    x1, x2, x3 = _split3(x)
    return _dot(x1, t) + _dot(x2, t) + _dot(x3, t)


def _row_valid(row0, n, ranges):
    r = row0 + lax.broadcasted_iota(jnp.int32, (n, 1), 0)
    ok = None
    for lo, hi in ranges:
        m = (r >= lo) & (r < hi)
        ok = m if ok is None else (ok | m)
    return ok


def _is_any(c, values):
    ok = None
    for v in values:
        m = c == v
        ok = m if ok is None else (ok | m)
    return ok


def _inproj_kernel(h_ref, prev_ref, next_ref, nw_ref, wb_ref, ws_ref, cw_ref, cb_ref, big_ref,
                   small_ref, xn_scr, *, conv_blocks):
    tm = h_ref.shape[0]
    n = tm + 2 * HALO
    j = pl.program_id(1)

    def norm(x):
        ms = jnp.mean(x * x, axis=-1, keepdims=True)
        return (x * lax.rsqrt(ms + EPS) * nw_ref[...]).astype(BF16)

    @pl.when(j == 0)
    def _():
        xn_scr[0:HALO, :] = norm(prev_ref[...])
        for r0 in range(0, tm, PROLOGUE_ROWS):
            xn_scr[HALO + r0:HALO + r0 + PROLOGUE_ROWS, :] = norm(h_ref[r0:r0 + PROLOGUE_ROWS, :])
        xn_scr[HALO + tm:2 * HALO + tm, :] = norm(next_ref[...])
        small_ref[...] = _dot(xn_scr[HALO:HALO + tm, :], ws_ref[...])

    is_conv = (j >= conv_blocks[0]) & (j < conv_blocks[1])

    @pl.when(is_conv)
    def _():
        for c0 in range(0, big_ref.shape[1], FFN_COL_CHUNK):
            cs = slice(c0, c0 + FFN_COL_CHUNK)
            u = _dot_row_pieces(xn_scr, wb_ref, cs)
            y = cb_ref[:, cs] + pltpu.roll(u, 1, 0)[HALO:HALO + tm] * cw_ref[0:1, cs]
            y = y + u[HALO:HALO + tm] * cw_ref[1:2, cs]
            y = y + pltpu.roll(u, n - 1, 0)[HALO:HALO + tm] * cw_ref[2:3, cs]
            big_ref[:, cs] = _silu(y).astype(BF16)

    @pl.when(jnp.logical_not(is_conv))
    def _():
        big_ref[...] = _dot(xn_scr[HALO:HALO + tm, :], wb_ref[...]).astype(BF16)


def _inproj(h, norm_w, w_big, w_small, conv_w, conv_b, conv_col0, tn=2048):
    rows, d = h.shape
    nb = w_big.shape[1]
    tm = ROW_TILE
    hb = tm // HALO
    last = rows // HALO - 1
    c = conv_w.shape[1]
    conv_blocks = (conv_col0 // tn, (conv_col0 + c) // tn)
    n_conv = conv_blocks[1] - conv_blocks[0]

    def conv_map(i, j):
        return (0, jnp.clip(j - conv_blocks[0], 0, n_conv - 1))

    kern = functools.partial(_inproj_kernel, conv_blocks=conv_blocks)
    return pl.pallas_call(
        kern,
        grid=(rows // tm, nb // tn),
        in_specs=[
            pl.BlockSpec((tm, d), lambda i, j: (i, 0)),
            pl.BlockSpec((HALO, d), lambda i, j: (jnp.maximum(i * hb - 1, 0), 0)),
            pl.BlockSpec((HALO, d), lambda i, j: (jnp.minimum((i + 1) * hb, last), 0)),
            pl.BlockSpec((1, d), lambda i, j: (0, 0)),
            pl.BlockSpec((d, tn), lambda i, j: (0, j)),
            pl.BlockSpec((d, LANES), lambda i, j: (0, 0)),
            pl.BlockSpec((3, tn), conv_map),
            pl.BlockSpec((1, tn), conv_map),
        ],
        out_specs=[
            pl.BlockSpec((tm, tn), lambda i, j: (i, j)),
            pl.BlockSpec((tm, LANES), lambda i, j: (i, 0)),
        ],
        out_shape=[
            jax.ShapeDtypeStruct((rows, nb), BF16),
            jax.ShapeDtypeStruct((rows, LANES), F32),
        ],
        scratch_shapes=[pltpu.VMEM((tm + 2 * HALO, d), BF16)],
        compiler_params=_cparams("parallel", "arbitrary"),
        name="inproj",
    )(h, h, h, norm_w.reshape(1, d), w_big, w_small, conv_w, conv_b.reshape(1, c))


def _ssd_chunk(d, c, x_ref, b_ref, c_ref, small_ref, dtb_ref, alog_ref, tri_ref, mask_ref, exp_ref,
               y_ref, state_scr, valid):
    ch = SSD_CHUNK
    ok = _row_valid(c * ch, ch, valid)
    dt = _softplus(small_ref[...] + dtb_ref[d])
    dt = jnp.where(ok, dt, 0.0)
    a = -jnp.exp(alog_ref[d]) * dt
    acs = _dot_exact_lhs(tri_ref[d], a)
    acs_t = _dot_exact_rhs(a.T, tri_ref[1 - d])
    dt_t = dt.T
    total = jnp.sum(a, axis=0, keepdims=True)
    w_in = jnp.exp(total - acs) * dt
    per_head = jnp.concatenate(
        [w_in, jnp.exp(acs), jnp.broadcast_to(jnp.exp(total), (8, LANES))], axis=0)
    hi = per_head.astype(BF16)
    lo = (per_head - hi.astype(F32)).astype(BF16)
    per_lane = _dot(jnp.concatenate([hi, lo], axis=1), exp_ref[...])
    w_exp = per_lane[0:ch]
    ea_exp = per_lane[ch:2 * ch]
    et_exp = per_lane[2 * ch:2 * ch + 1]
    mask = mask_ref[d]

    gw = SSD_HEADS_PER_GROUP * SSD_HEAD_DIM
    lane = lax.broadcasted_iota(jnp.int32, (1, gw), 1)
    for grp in range(SSD_N_GROUPS):
        bg = b_ref[:, grp * SSD_D_STATE:(grp + 1) * SSD_D_STATE]
        cg = c_ref[:, grp * SSD_D_STATE:(grp + 1) * SSD_D_STATE]
        xg = x_ref[:, grp * gw:(grp + 1) * gw]
        cb = _dot_nt(cg, bg)
        st = state_scr[d, grp]
        y = _dot(cg, st.astype(BF16)) * ea_exp[:, grp * gw:(grp + 1) * gw]
        ms, xs = [], []
        for hh in range(SSD_HEADS_PER_GROUP):
            h = grp * SSD_HEADS_PER_GROUP + hh
            seg = acs[:, h:h + 1] - acs_t[h:h + 1, :] + mask
            ms.append((cb * jnp.exp(seg) * dt_t[h:h + 1, :]).astype(BF16))
            in_head = (lane >= hh * SSD_HEAD_DIM) & (lane < (hh + 1) * SSD_HEAD_DIM)
            xs.append(jnp.where(in_head, xg, jnp.zeros_like(xg)))
        y = y + _dot(jnp.concatenate(ms, axis=1), jnp.concatenate(xs, axis=0))
        y_ref[:, grp * gw:(grp + 1) * gw] = y.astype(BF16)
        xw = (xg.astype(F32) * w_exp[:, grp * gw:(grp + 1) * gw]).astype(BF16)
        state_scr[d, grp] = st * et_exp[:, grp * gw:(grp + 1) * gw] + _dot_tn(bg, xw)


def _ssd_scan_kernel(xf_ref, bf_ref, cf_ref, sf_ref, xb_ref, bb_ref, cb_ref, sb_ref, dtb_ref,
                     alog_ref, tri_ref, mask_ref, exp_ref, yf_ref, yb_ref, state_scr, *, n_chunks,
                     firsts, lasts, valid):
    g = pl.program_id(0)
    r = n_chunks - 1 - g

    @pl.when(_is_any(g, firsts))
    def _():
        state_scr[0] = jnp.zeros(state_scr.shape[1:], F32)

    @pl.when(_is_any(r, lasts))
    def _():
        state_scr[1] = jnp.zeros(state_scr.shape[1:], F32)

    shared = (dtb_ref, alog_ref, tri_ref, mask_ref, exp_ref)
    _ssd_chunk(0, g, xf_ref, bf_ref, cf_ref, sf_ref, *shared, yf_ref, state_scr, valid)
    _ssd_chunk(1, r, xb_ref, bb_ref, cb_ref, sb_ref, *shared, yb_ref, state_scr, valid)


def _ssd_scan(xbc, col_x, small, dt_bias, a_log, layout):
    rows = xbc.shape[0]
    ch = SSD_CHUNK
    d_inner = SSD_N_GROUPS * SSD_HEADS_PER_GROUP * SSD_HEAD_DIM
    gn = SSD_N_GROUPS * SSD_D_STATE
    n_heads = SSD_N_GROUPS * SSD_HEADS_PER_GROUP
    n_chunks = rows // ch
    firsts = tuple(s // ch for s, _ in layout["spans"])
    lasts = tuple((s + n) // ch - 1 for s, n in layout["spans"])

    idx = np.arange(ch)
    lower = (idx[:, None] >= idx[None, :])
    tri = np.stack([lower, lower.T]).astype(np.float32)
    mask = np.where(tri > 0, 0.0, NEG_INF).astype(np.float32)
    expand = np.zeros((2 * LANES, d_inner), np.float32)
    for h in range(n_heads):
        expand[h, h * SSD_HEAD_DIM:(h + 1) * SSD_HEAD_DIM] = 1.0
        expand[LANES + h, h * SSD_HEAD_DIM:(h + 1) * SSD_HEAD_DIM] = 1.0
    pad = ((0, 0), (0, 0), (0, LANES - n_heads))
    dtb = jnp.pad(dt_bias.reshape(2, 1, n_heads), pad)
    alog = jnp.pad(a_log.reshape(2, 1, n_heads), pad)

    def chunk_specs(cmap):
        return [
            pl.BlockSpec((ch, d_inner), lambda g: (cmap(g), col_x // d_inner)),
            pl.BlockSpec((ch, gn), lambda g: (cmap(g), (col_x + d_inner) // gn)),
            pl.BlockSpec((ch, gn), lambda g: (cmap(g), (col_x + d_inner) // gn + 1)),
            pl.BlockSpec((ch, LANES), lambda g: (cmap(g), 0)),
        ]

    def fwd(g):
        return g

    def rev(g):
        return n_chunks - 1 - g

    kern = functools.partial(_ssd_scan_kernel, n_chunks=n_chunks, firsts=firsts, lasts=lasts,
                             valid=layout["valid"])
    return pl.pallas_call(
        kern,
        grid=(n_chunks,),
        in_specs=chunk_specs(fwd) + chunk_specs(rev) + [
            pl.BlockSpec((2, 1, LANES), lambda g: (0, 0, 0)),
            pl.BlockSpec((2, 1, LANES), lambda g: (0, 0, 0)),
            pl.BlockSpec((2, ch, ch), lambda g: (0, 0, 0)),
            pl.BlockSpec((2, ch, ch), lambda g: (0, 0, 0)),
            pl.BlockSpec((2 * LANES, d_inner), lambda g: (0, 0)),
        ],
        out_specs=[pl.BlockSpec((ch, d_inner), lambda g: (fwd(g), 0)),
                   pl.BlockSpec((ch, d_inner), lambda g: (rev(g), 0))],
        out_shape=[jax.ShapeDtypeStruct((rows, d_inner), BF16)] * 2,
        scratch_shapes=[
            pltpu.VMEM((2, SSD_N_GROUPS, SSD_D_STATE, SSD_HEADS_PER_GROUP * SSD_HEAD_DIM), F32)],
        compiler_params=_cparams("arbitrary"),
        name="ssd_scan",
    )(xbc, xbc, xbc, small, xbc, xbc, xbc, small, dtb, alog, jnp.asarray(tri, BF16),
      jnp.asarray(mask), jnp.asarray(expand, BF16))


def _gla_levels():
    lv, m = [], GLA_CHUNK // 2
    while m >= 1:
        lv.append(m)
        m //= 2
    return tuple(lv)


def _gla_level_exponent(cs_scr, cs_t, t, m, reverse):
    ch = cs_t.shape[0]
    off = m - 1 if reverse else m

    def ref_tile(r):
        return jnp.broadcast_to(cs_scr[t, pl.ds(r, 1), :], (8, LANES))

    if m >= 8:
        pieces = []
        for p in range(ch // (2 * m)):
            lo = 2 * m * p
            ref = jnp.concatenate([ref_tile(lo + off)] * (m // 8), axis=0)
            early, late = cs_t[lo:lo + m], cs_t[lo + m:lo + 2 * m]
            pieces += [early - ref, ref - late] if reverse else [ref - early, late - ref]
        return jnp.concatenate(pieces, axis=0)

    row = lax.broadcasted_iota(jnp.int32, (ch, 1), 0)
    in_late = ((row // m) & 1) == 1
    is_target = in_late != reverse
    if m == 1:
        x3 = cs_t.reshape(ch // 8, 8, LANES)
        nb = pltpu.roll(x3, 1 if reverse else 7, 1).reshape(ch, LANES)
        return jnp.where(is_target, 0.0, nb - cs_t)
    sub = lax.broadcasted_iota(jnp.int32, (8, 1), 0)
    pieces = []
    for p in range(ch // 8):
        ref = ref_tile(8 * p + off)
        for q in range(1, 8 // (2 * m)):
            ref = jnp.where(sub >= 2 * m * q, ref_tile(8 * p + 2 * m * q + off), ref)
        pieces.append(ref)
    delta = cs_t - jnp.concatenate(pieces, axis=0)
    return jnp.where(is_target, delta, -delta)


def _gla_level_operand(q_scr, k_ref, sl, m, reverse):
    ch = q_scr.shape[0]
    if m >= 16:
        pieces = []
        for p in range(ch // (2 * m)):
            lo = 2 * m * p
            early, late = slice(lo, lo + m), slice(lo + m, lo + 2 * m)
            if reverse:
                pieces += [q_scr[early, sl], k_ref[late, sl]]
            else:
                pieces += [k_ref[early, sl], q_scr[late, sl]]
        return jnp.concatenate(pieces, axis=0)
    row = lax.broadcasted_iota(jnp.int32, (ch, 1), 0)
    is_target = (((row // m) & 1) == 1) != reverse
    return jnp.where(is_target, q_scr[:, sl], k_ref[:, sl])


def _gla_scan_kernel(q_ref, k_ref, v_ref, small_ref, gu_ref, gb_ref, tri_ref, lev_ref, o_ref,
                     state_scr, cs_scr, att_scr, diag_scr, q_scr, *, reverse, n_chunks, firsts, lasts,
                     valid, dk, dv):
    g = pl.program_id(0)
    c = n_chunks - 1 - g if reverse else g
    ch = GLA_CHUNK
    levels = _gla_levels()
    tiles_per_head = dk // LANES

    @pl.when(_is_any(c, lasts if reverse else firsts))
    def _():
        state_scr[...] = jnp.zeros_like(state_scr)

    ok = _row_valid(c * ch, ch, valid)
    u = _dot(small_ref[...].astype(BF16), gu_ref[...]) + gb_ref[...]
    gate = jnp.where(ok, -_softplus(-u) / GLA_GATE_TAU, 0.0)
    cs = _dot_split2_lhs(tri_ref[...], gate)
    n_tiles = cs.shape[1] // LANES
    for t in range(n_tiles):
        cs_scr[t] = cs[:, t * LANES:(t + 1) * LANES]
    total = jnp.sum(gate, axis=0, keepdims=True)
    lev = lev_ref[...]
    q_scr[...] = q_ref[...] * jnp.asarray(dk ** -0.5, BF16)

    half = ch // 2
    tgt, src = (slice(0, half), slice(half, ch)) if reverse else (slice(half, ch), slice(0, half))
    quads = (slice(0, half), slice(half, ch))

    @pl.when(g == 0)
    def _():
        att_scr[:, src, tgt] = jnp.zeros((GLA_N_HEADS, half, half), BF16)

    for i, m in enumerate(levels):
        e = [jnp.exp(_gla_level_exponent(cs_scr, cs[:, t * LANES:(t + 1) * LANES], t, m, reverse)
                     ).astype(BF16) for t in range(n_tiles)]
        for h in range(GLA_N_HEADS):
            sl = slice(h * dk, (h + 1) * dk)
            e_h = jnp.concatenate(e[h * tiles_per_head:(h + 1) * tiles_per_head], axis=1)
            x = _gla_level_operand(q_scr, k_ref, sl, m, reverse) * e_h
            if i == 0:
                att_scr[h, tgt, src] = _dot_nt(x[tgt], x[src]).astype(BF16)
                continue
            for qi, qs in enumerate(quads):
                p = _dot_nt(x[qs], x[qs])
                diag_scr[h, qi] = jnp.where(lev == i, p, 0.0 if i == 1 else diag_scr[h, qi])

    for h in range(GLA_N_HEADS):
        sl = slice(h * dk, (h + 1) * dk)
        vs = slice(h * dv, (h + 1) * dv)
        vh = v_ref[:, vs]
        for qi, qs in enumerate(quads):
            p = _dot_nt(q_scr[qs, sl], k_ref[qs, sl])
            att_scr[h, qs, qs] = jnp.where(lev == len(levels), p, diag_scr[h, qi]).astype(BF16)
        st = state_scr[h]
        qd = q_scr[:, sl] * jnp.exp(cs[:, sl]).astype(BF16)
        o = _dot(att_scr[h], vh) + _dot_nt(qd, st.astype(BF16))
        o_ref[:, vs] = o.astype(BF16)
        kd = k_ref[:, sl] * jnp.exp(total[:, sl] - cs[:, sl]).astype(BF16)
        state_scr[h] = st * jnp.exp(total[:, sl]) + _dot_tn(vh, kd)


def _gla_scan(big, small, gate_up, gate_b, layout, col_q, col_k, col_v, dk, dv, reverse):
    rows = big.shape[0]
    ch = GLA_CHUNK
    n_chunks = rows // ch
    firsts = tuple(s // ch for s, _ in layout["spans"])
    lasts = tuple((s + n) // ch - 1 for s, n in layout["spans"])
    hk = GLA_N_HEADS * dk
    hv = GLA_N_HEADS * dv
    levels = _gla_levels()

    idx = np.arange(ch)
    l_, s_ = idx[:, None], idx[None, :]
    seen = (l_ <= s_) if reverse else (l_ >= s_)
    x = l_ ^ s_
    top = np.floor(np.log2(np.maximum(x, 1))).astype(np.int64)
    lvl = np.where(x == 0, len(levels), (len(levels) - 1) - top)
    lev = np.where(seen, lvl, -1).astype(np.int32)[:ch // 2, :ch // 2]

    lo = 32 + GLA_GATE_RANK * int(reverse)
    gu = jnp.zeros((LANES, hk), F32).at[lo:lo + GLA_GATE_RANK].set(gate_up).astype(BF16)

    def cmap(g):
        return n_chunks - 1 - g if reverse else g

    kern = functools.partial(_gla_scan_kernel, reverse=reverse, n_chunks=n_chunks, firsts=firsts,
                             lasts=lasts, valid=layout["valid"], dk=dk, dv=dv)
    return pl.pallas_call(
        kern,
        grid=(n_chunks,),
        in_specs=[
            pl.BlockSpec((ch, hk), lambda g: (cmap(g), col_q // hk)),
            pl.BlockSpec((ch, hk), lambda g: (cmap(g), col_k // hk)),
            pl.BlockSpec((ch, hv), lambda g: (cmap(g), col_v // hv)),
            pl.BlockSpec((ch, LANES), lambda g: (cmap(g), 0)),
            pl.BlockSpec((LANES, hk), lambda g: (0, 0)),
            pl.BlockSpec((1, hk), lambda g: (0, 0)),
            pl.BlockSpec((ch, ch), lambda g: (0, 0)),
            pl.BlockSpec((ch // 2, ch // 2), lambda g: (0, 0)),
        ],
        out_specs=pl.BlockSpec((ch, hv), lambda g: (cmap(g), 0)),
        out_shape=jax.ShapeDtypeStruct((rows, hv), BF16),
        scratch_shapes=[
            pltpu.VMEM((GLA_N_HEADS, dv, dk), F32),
            pltpu.VMEM((hk // LANES, ch, LANES), F32),
            pltpu.VMEM((GLA_N_HEADS, ch, ch), BF16),
            pltpu.VMEM((GLA_N_HEADS, 2, ch // 2, ch // 2), F32),
            pltpu.VMEM((ch, hk), BF16),
        ],
        compiler_params=_cparams("arbitrary"),
        name="gla_scan_bwd" if reverse else "gla_scan_fwd",
    )(big, big, big, small, gu, gate_b.reshape(1, hk), jnp.asarray(seen.astype(np.float32), BF16),
      jnp.asarray(lev))


def _merge_kernel(ysf_ref, ysb_ref, xs_ref, z_ref, ogf_ref, ogb_ref, ogate_ref, ms_ref, mg_ref, dskip_ref,
                  snw_ref, gnw_ref, ws_ref, wg_ref, o_ref, a_ssd, a_gla, *, dv):
    s = pl.program_id(0)
    tm = a_ssd.shape[1]

    @pl.when(s == 0)
    def _():
        a_ssd[1] = jnp.zeros(a_ssd.shape[1:], BF16)
        a_gla[1] = jnp.zeros(a_gla.shape[1:], BF16)

    slot = s % 2
    for r0 in range(0, tm, PROLOGUE_ROWS):
        rs = slice(r0, r0 + PROLOGUE_ROWS)
        y = ysf_ref[rs, :].astype(F32) + ysb_ref[rs, :].astype(F32)
        y = y + xs_ref[rs, :].astype(F32) * dskip_ref[...]
        y = y * _silu(z_ref[rs, :].astype(F32))
        ms = jnp.mean(y * y, axis=-1, keepdims=True)
        a_ssd[slot, rs, :] = (y * lax.rsqrt(ms + EPS) * snw_ref[...]).astype(BF16)
        for h in range(GLA_N_HEADS):
            sl = slice(h * dv, (h + 1) * dv)
            o = ogf_ref[rs, sl].astype(F32) + ogb_ref[rs, sl].astype(F32)
            ms = jnp.mean(o * o, axis=-1, keepdims=True)
            o = o * lax.rsqrt(ms + EPS) * gnw_ref[...]
            a_gla[slot, rs, sl] = (o * _silu(ogate_ref[rs, sl].astype(F32))).astype(BF16)

    y_ssd = _dot(a_ssd[1 - slot], ws_ref[...])
    y_gla = _dot(a_gla[1 - slot], wg_ref[...])
    merged = _sigmoid(ms_ref[...].astype(F32)) * y_ssd
    merged = merged + _sigmoid(mg_ref[...].astype(F32)) * y_gla
    o_ref[...] = merged.astype(BF16)


def _merge(y_fwd, y_bwd, big, o_fwd, o_bwd, d_skip, ssd_norm_w, gla_norm_w, w_ssd, w_gla, col_x, col_og,
           col_merge, tm=256):
    rows, d = big.shape[0], w_ssd.shape[1]
    di = w_ssd.shape[0]
    dvt = w_gla.shape[0]
    dv = dvt // GLA_N_HEADS
    n_tiles = rows // tm
    kern = functools.partial(_merge_kernel, dv=dv)

    def cur(s):
        return jnp.minimum(s, n_tiles - 1)

    def prev(s):
        return jnp.maximum(s - 1, 0)

    resident = pl.Buffered(1)
    return pl.pallas_call(
        kern,
        grid=(n_tiles + 1,),
        in_specs=[
            pl.BlockSpec((tm, di), lambda s: (cur(s), 0)),
            pl.BlockSpec((tm, di), lambda s: (cur(s), 0)),
            pl.BlockSpec((tm, di), lambda s: (cur(s), col_x // di)),
            pl.BlockSpec((tm, di), lambda s: (cur(s), 0)),
            pl.BlockSpec((tm, dvt), lambda s: (cur(s), 0)),
            pl.BlockSpec((tm, dvt), lambda s: (cur(s), 0)),
            pl.BlockSpec((tm, dvt), lambda s: (cur(s), col_og // dvt)),
            pl.BlockSpec((tm, d), lambda s: (prev(s), col_merge // d)),
            pl.BlockSpec((tm, d), lambda s: (prev(s), col_merge // d + 1)),
            pl.BlockSpec((1, di), lambda s: (0, 0)),
            pl.BlockSpec((1, di), lambda s: (0, 0)),
            pl.BlockSpec((1, dv), lambda s: (0, 0)),
            pl.BlockSpec((di, d), lambda s: (0, 0), pipeline_mode=resident),
            pl.BlockSpec((dvt, d), lambda s: (0, 0), pipeline_mode=resident),
        ],
        out_specs=pl.BlockSpec((tm, d), lambda s: (prev(s), 0)),
        out_shape=jax.ShapeDtypeStruct((rows, d), BF16),
        scratch_shapes=[pltpu.VMEM((2, tm, di), BF16), pltpu.VMEM((2, tm, dvt), BF16)],
        compiler_params=_cparams("arbitrary"),
        name="merge",
    )(y_fwd, y_bwd, big, big, o_fwd, o_bwd, big, big, big,
      jnp.repeat(d_skip, SSD_HEAD_DIM).reshape(1, di), ssd_norm_w.reshape(1, di),
      gla_norm_w.reshape(1, dv), w_ssd, w_gla)


def _mm_res_kernel(*refs):
    *aw_refs, h_ref, o_ref = refs
    n_terms = len(aw_refs) // 2
    acc = h_ref[...]
    for a_ref, w_ref in zip(aw_refs[:n_terms], aw_refs[n_terms:]):
        acc = acc + _dot(a_ref[...], w_ref[...])
    o_ref[...] = acc


def _mm_res(a_list, w_list, h, n_split, tm=512):
    rows = h.shape[0]
    n = w_list[0].shape[1]
    tn = n // n_split
    a_specs = [pl.BlockSpec((tm, a.shape[1]), lambda j, i: (i, 0)) for a in a_list]
    w_specs = [pl.BlockSpec((w.shape[0], tn), lambda j, i: (0, j)) for w in w_list]
    return pl.pallas_call(
        _mm_res_kernel,
        grid=(n_split, rows // tm),
        in_specs=a_specs + w_specs + [pl.BlockSpec((tm, tn), lambda j, i: (i, j))],
        out_specs=pl.BlockSpec((tm, tn), lambda j, i: (i, j)),
        out_shape=jax.ShapeDtypeStruct((rows, n), F32),
        compiler_params=_cparams("arbitrary", "arbitrary"),
        name="mm_res",
    )(*a_list, *w_list, h)


def _ffn_up_kernel(h_ref, prev_ref, next_ref, nw_ref, wg_ref, wu_ref, cwg_ref, cwu_ref, cbg_ref,
                   cbu_ref, o_ref, xn_scr, *, valid):
    tm = h_ref.shape[0]

    def norm(x):
        ms = jnp.mean(x * x, axis=-1, keepdims=True)
        return (x * lax.rsqrt(ms + EPS) * nw_ref[...]).astype(BF16)

    @pl.when(pl.program_id(1) == 0)
    def _():
        xn_scr[0:HALO, :] = norm(prev_ref[...])
        for r0 in range(0, tm, PROLOGUE_ROWS):
            xn_scr[HALO + r0:HALO + r0 + PROLOGUE_ROWS, :] = norm(h_ref[r0:r0 + PROLOGUE_ROWS, :])
        xn_scr[HALO + tm:2 * HALO + tm, :] = norm(next_ref[...])

    n = tm + 2 * HALO

    def conv(u, w_ref, b_ref, cs):
        y = b_ref[:, cs] + pltpu.roll(u, 1, 0)[HALO:HALO + tm] * w_ref[0:1, cs]
        y = y + u[HALO:HALO + tm] * w_ref[1:2, cs]
        return y + pltpu.roll(u, n - 1, 0)[HALO:HALO + tm] * w_ref[2:3, cs]

    ok = _row_valid(pl.program_id(0) * tm, tm, valid)
    for c0 in range(0, o_ref.shape[1], FFN_COL_CHUNK):
        cs = slice(c0, c0 + FFN_COL_CHUNK)
        gate = conv(_dot_row_pieces(xn_scr, wg_ref, cs), cwg_ref, cbg_ref, cs)
        up = conv(_dot_row_pieces(xn_scr, wu_ref, cs), cwu_ref, cbu_ref, cs)
        o_ref[:, cs] = jnp.where(ok, _silu(gate) * up, 0.0).astype(BF16)


def _ffn_up(h, norm_w, w_up, conv_w, conv_b, layout, tf=512):
    rows, d = h.shape
    dff = w_up.shape[1] // 2
    tm = ROW_TILE
    hb = tm // HALO
    last = rows // HALO - 1
    nj = dff // tf
    kern = functools.partial(_ffn_up_kernel, valid=layout["valid"])
    return pl.pallas_call(
        kern,
        grid=(rows // tm, nj),
        in_specs=[
            pl.BlockSpec((tm, d), lambda i, j: (i, 0)),
            pl.BlockSpec((HALO, d), lambda i, j: (jnp.maximum(i * hb - 1, 0), 0)),
            pl.BlockSpec((HALO, d), lambda i, j: (jnp.minimum((i + 1) * hb, last), 0)),
            pl.BlockSpec((1, d), lambda i, j: (0, 0)),
            pl.BlockSpec((d, tf), lambda i, j: (0, j)),
            pl.BlockSpec((d, tf), lambda i, j: (0, nj + j)),
            pl.BlockSpec((3, tf), lambda i, j: (0, j)),
            pl.BlockSpec((3, tf), lambda i, j: (0, nj + j)),
            pl.BlockSpec((1, tf), lambda i, j: (0, j)),
            pl.BlockSpec((1, tf), lambda i, j: (0, nj + j)),
        ],
        out_specs=pl.BlockSpec((tm, tf), lambda i, j: (i, j)),
        out_shape=jax.ShapeDtypeStruct((rows, dff), BF16),
        scratch_shapes=[pltpu.VMEM((tm + 2 * HALO, d), BF16)],
        compiler_params=_cparams("parallel", "arbitrary"),
        name="ffn_up",
    )(h, h, h, norm_w.reshape(1, d), w_up, w_up, conv_w, conv_w,
      conv_b.reshape(1, 2 * dff), conv_b.reshape(1, 2 * dff))


def _final_norm_kernel(h_ref, nw_ref, o_ref):
    x = h_ref[...]
    ms = jnp.mean(x * x, axis=-1, keepdims=True)
    o_ref[0] = x * lax.rsqrt(ms + EPS) * nw_ref[...]


def _final_norm(h, norm_w, first_row, batch, length, tr=ROW_ALIGN):
    d = h.shape[1]
    per_seq = (ROW_ALIGN + length) // tr
    b0 = (first_row + ROW_ALIGN) // tr
    return pl.pallas_call(
        _final_norm_kernel,
        grid=(batch, length // tr),
        in_specs=[
            pl.BlockSpec((tr, d), lambda b, t: (b0 + b * per_seq + t, 0)),
            pl.BlockSpec((1, d), lambda b, t: (0, 0)),
        ],
        out_specs=pl.BlockSpec((1, tr, d), lambda b, t: (b, t, 0)),
        out_shape=jax.ShapeDtypeStruct((batch, length, d), F32),
        compiler_params=_cparams("parallel", "parallel"),
        name="final_norm",
    )(h, norm_w.reshape(1, d))


def _make_layout(groups):
    seqs, valid, row = [], [], 0
    for batch, length in groups:
        assert length % ROW_ALIGN == 0
        for _ in range(batch):
            n = ROW_ALIGN + length
            seqs.append((row, n))
            valid.append((row + FRONT_PAD, row + n))
            row += n
    rows = (row // ROW_TILE + 1) * ROW_TILE
    spans = tuple(seqs) + ((row, rows - row),)
    return dict(seqs=tuple(seqs), spans=spans, valid=tuple(valid), used_rows=row, rows=rows)


def kernel(x_prompt, x_sample, meta_tokens, mix_norm_w, w_in, ssd_conv_w, ssd_conv_b, ssd_dt_bias,
           ssd_a_log, ssd_d, ssd_norm_w, ssd_w_out, gla_gate_up, gla_gate_b, gla_norm_w, gla_w_out,
           w_mix_out, ffn_norm_w, ffn_w_up, ffn_conv_w, ffn_conv_b, ffn_w_down, final_norm_w):
    depth, d = mix_norm_w.shape
    groups = [(x_prompt.shape[0], x_prompt.shape[1]), (x_sample.shape[0], x_sample.shape[1])]
    layout = _make_layout(groups)
    rows = layout["rows"]

    n_heads = ssd_dt_bias.shape[-1]
    d_inner = n_heads * SSD_HEAD_DIM
    conv_dim = ssd_conv_w.shape[-1]
    d_k = gla_gate_up.shape[-1]
    d_v = gla_w_out.shape[1]
    dk, dv = d_k // GLA_N_HEADS, d_v // GLA_N_HEADS
    sizes = (d_inner, conv_dim, n_heads, d_k, d_k, d_v, d_v, 2 * GLA_GATE_RANK, 2 * d)
    off = np.concatenate([[0], np.cumsum(sizes)])
    assert off[-1] == w_in.shape[-1]
    wide = [0, 1, 3, 4, 5, 6, 8]
    col, pos = {}, 0
    for s in wide:
        col[s] = pos
        pos += sizes[s]

    zero_front = jnp.zeros((FRONT_PAD, d), F32)
    pieces = []
    for x in (x_prompt, x_sample):
        for b in range(x.shape[0]):
            pieces += [zero_front, meta_tokens.astype(F32), x[b]]
    if rows > layout["used_rows"]:
        pieces.append(jnp.zeros((rows - layout["used_rows"], d), F32))
    h = jnp.concatenate(pieces, axis=0)

    for i in range(depth):
        w = w_in[i]
        w_big = jnp.concatenate([w[:, off[s]:off[s + 1]] for s in wide], axis=1).astype(BF16)
        w_small = jnp.concatenate(
            [w[:, off[2]:off[3]], w[:, off[7]:off[8]],
             jnp.zeros((d, LANES - n_heads - 2 * GLA_GATE_RANK), F32)], axis=1).astype(BF16)

        big, small = _inproj(h, mix_norm_w[i], w_big, w_small, ssd_conv_w[i], ssd_conv_b[i], col[1])
        y_fwd, y_bwd = _ssd_scan(big, col[1], small, ssd_dt_bias[i], ssd_a_log[i], layout)
        o_fwd, o_bwd = (_gla_scan(big, small, gla_gate_up[i, r], gla_gate_b[i, r], layout,
                                  col[3], col[4], col[5], dk, dv, reverse=bool(r)) for r in (0, 1))
        merged = _merge(y_fwd, y_bwd, big, o_fwd, o_bwd, ssd_d[i], ssd_norm_w[i], gla_norm_w[i],
                        ssd_w_out[i].astype(BF16), gla_w_out[i].astype(BF16), col[1], col[6], col[8])
        h = _mm_res([merged], [w_mix_out[i].astype(BF16)], h, n_split=1)
        act = _ffn_up(h, ffn_norm_w[i], ffn_w_up[i].astype(BF16), ffn_conv_w[i], ffn_conv_b[i],
                      layout)
        h = _mm_res([act], [ffn_w_down[i].astype(BF16)], h, n_split=2)

    outs, si = [], 0
    for batch, length in groups:
        outs.append(_final_norm(h, final_norm_w, layout["seqs"][si][0], batch, length))
        si += batch
    return tuple(outs)
```

```python
import functools

import numpy as np
import jax
import jax.numpy as jnp
from jax import lax
from jax.experimental import pallas as pl
from jax.experimental.pallas import tpu as pltpu

F32 = jnp.float32
BF16 = jnp.bfloat16

N_META = 16
EPS = 1e-6
SSD_HEAD_DIM = 64
SSD_N_GROUPS = 8
SSD_HEADS_PER_GROUP = 4
SSD_D_STATE = 128
GLA_N_HEADS = 4
GLA_GATE_RANK = 16
GLA_GATE_TAU = 16.0

LANES = 128
ROW_ALIGN = 256
FRONT_PAD = ROW_ALIGN - N_META
SSD_CHUNK = 128
GLA_CHUNK = 256
ROW_TILE = 1024
PROLOGUE_ROWS = 128
FFN_COL_CHUNK = 256
ROW_PIECES = 6
HALO = 16
VMEM_LIMIT = 56 * 1024 * 1024
NEG_INF = float("-inf")


def _cparams(*sem):
    return pltpu.CompilerParams(dimension_semantics=sem, vmem_limit_bytes=VMEM_LIMIT)


def _softplus(x):
    return jnp.maximum(x, 0.0) + jnp.log(1.0 + jnp.exp(-jnp.abs(x)))


def _sigmoid(x):
    return 0.5 + 0.5 * jnp.tanh(0.5 * x)


def _silu(x):
    hx = 0.5 * x
    return hx + hx * jnp.tanh(hx)


def _split3(x):
    x1 = x.astype(BF16)
    r1 = x - x1.astype(F32)
    x2 = r1.astype(BF16)
    x3 = (r1 - x2.astype(F32)).astype(BF16)
    return x1, x2, x3


def _dot(a, b):
    return jnp.dot(a, b, preferred_element_type=F32)


def _dot_row_pieces(x_ref, w_ref, cs, pieces=ROW_PIECES):
    n = x_ref.shape[0]
    step = n // pieces
    return jnp.concatenate(
        [_dot(x_ref[r0:r0 + step, :], w_ref[:, cs]) for r0 in range(0, n, step)], axis=0)


def _dot_nt(a, b):
    return lax.dot_general(a, b, (((1,), (1,)), ((), ())), preferred_element_type=F32)


def _dot_tn(a, b):
    return lax.dot_general(a, b, (((0,), (0,)), ((), ())), preferred_element_type=F32)


def _dot_exact_lhs(t, x):
    x1, x2, x3 = _split3(x)
    return _dot(t, x1) + _dot(t, x2) + _dot(t, x3)


def _dot_split2_lhs(t, x):
    x1 = x.astype(BF16)
    x2 = (x - x1.astype(F32)).astype(BF16)
    return _dot(t, x1) + _dot(t, x2)


def _dot_exact_rhs(x, t):
    x1, x2, x3 = _split3(x)
    return _dot(x1, t) + _dot(x2, t) + _dot(x3, t)


def _row_valid(row0, n, ranges):
    r = row0 + lax.broadcasted_iota(jnp.int32, (n, 1), 0)
    ok = None
    for lo, hi in ranges:
        m = (r >= lo) & (r < hi)
        ok = m if ok is None else (ok | m)
    return ok


def _is_any(c, values):
    ok = None
    for v in values:
        m = c == v
        ok = m if ok is None else (ok | m)
    return ok


def _inproj_kernel(h_ref, prev_ref, next_ref, nw_ref, wb_ref, ws_ref, cw_ref, cb_ref, big_ref,
                   small_ref, xn_scr, *, conv_blocks):
    tm = h_ref.shape[0]
    n = tm + 2 * HALO
    j = pl.program_id(1)

    def norm(x):
        ms = jnp.mean(x * x, axis=-1, keepdims=True)
        return (x * lax.rsqrt(ms + EPS) * nw_ref[...]).astype(BF16)

    @pl.when(j == 0)
    def _():
        xn_scr[0:HALO, :] = norm(prev_ref[...])
        for r0 in range(0, tm, PROLOGUE_ROWS):
            xn_scr[HALO + r0:HALO + r0 + PROLOGUE_ROWS, :] = norm(h_ref[r0:r0 + PROLOGUE_ROWS, :])
        xn_scr[HALO + tm:2 * HALO + tm, :] = norm(next_ref[...])
        small_ref[...] = _dot(xn_scr[HALO:HALO + tm, :], ws_ref[...])

    is_conv = (j >= conv_blocks[0]) & (j < conv_blocks[1])

    @pl.when(is_conv)
    def _():
        for c0 in range(0, big_ref.shape[1], FFN_COL_CHUNK):
            cs = slice(c0, c0 + FFN_COL_CHUNK)
            u = _dot_row_pieces(xn_scr, wb_ref, cs)
            y = cb_ref[:, cs] + pltpu.roll(u, 1, 0)[HALO:HALO + tm] * cw_ref[0:1, cs]
            y = y + u[HALO:HALO + tm] * cw_ref[1:2, cs]
            y = y + pltpu.roll(u, n - 1, 0)[HALO:HALO + tm] * cw_ref[2:3, cs]
            big_ref[:, cs] = _silu(y).astype(BF16)

    @pl.when(jnp.logical_not(is_conv))
    def _():
        big_ref[...] = _dot(xn_scr[HALO:HALO + tm, :], wb_ref[...]).astype(BF16)


def _inproj(h, norm_w, w_big, w_small, conv_w, conv_b, conv_col0, tn=2048):
    rows, d = h.shape
    nb = w_big.shape[1]
    tm = ROW_TILE
    hb = tm // HALO
    last = rows // HALO - 1
    c = conv_w.shape[1]
    conv_blocks = (conv_col0 // tn, (conv_col0 + c) // tn)
    n_conv = conv_blocks[1] - conv_blocks[0]

    def conv_map(i, j):
        return (0, jnp.clip(j - conv_blocks[0], 0, n_conv - 1))

    kern = functools.partial(_inproj_kernel, conv_blocks=conv_blocks)
    return pl.pallas_call(
        kern,
        grid=(rows // tm, nb // tn),
        in_specs=[
            pl.BlockSpec((tm, d), lambda i, j: (i, 0)),
            pl.BlockSpec((HALO, d), lambda i, j: (jnp.maximum(i * hb - 1, 0), 0)),
            pl.BlockSpec((HALO, d), lambda i, j: (jnp.minimum((i + 1) * hb, last), 0)),
            pl.BlockSpec((1, d), lambda i, j: (0, 0)),
            pl.BlockSpec((d, tn), lambda i, j: (0, j)),
            pl.BlockSpec((d, LANES), lambda i, j: (0, 0)),
            pl.BlockSpec((3, tn), conv_map),
            pl.BlockSpec((1, tn), conv_map),
        ],
        out_specs=[
            pl.BlockSpec((tm, tn), lambda i, j: (i, j)),
            pl.BlockSpec((tm, LANES), lambda i, j: (i, 0)),
        ],
        out_shape=[
            jax.ShapeDtypeStruct((rows, nb), BF16),
            jax.ShapeDtypeStruct((rows, LANES), F32),
        ],
        scratch_shapes=[pltpu.VMEM((tm + 2 * HALO, d), BF16)],
        compiler_params=_cparams("parallel", "arbitrary"),
        name="inproj",
    )(h, h, h, norm_w.reshape(1, d), w_big, w_small, conv_w, conv_b.reshape(1, c))


def _ssd_chunk(d, c, x_ref, b_ref, c_ref, small_ref, dtb_ref, alog_ref, tri_ref, mask_ref, exp_ref,
               y_ref, state_scr, valid):
    ch = SSD_CHUNK
    ok = _row_valid(c * ch, ch, valid)
    dt = _softplus(small_ref[...] + dtb_ref[d])
    dt = jnp.where(ok, dt, 0.0)
    a = -jnp.exp(alog_ref[d]) * dt
    acs = _dot_exact_lhs(tri_ref[d], a)
    acs_t = _dot_exact_rhs(a.T, tri_ref[1 - d])
    dt_t = dt.T
    total = jnp.sum(a, axis=0, keepdims=True)
    w_in = jnp.exp(total - acs) * dt
    per_head = jnp.concatenate(
        [w_in, jnp.exp(acs), jnp.broadcast_to(jnp.exp(total), (8, LANES))], axis=0)
    hi = per_head.astype(BF16)
    lo = (per_head - hi.astype(F32)).astype(BF16)
    per_lane = _dot(jnp.concatenate([hi, lo], axis=1), exp_ref[...])
    w_exp = per_lane[0:ch]
    ea_exp = per_lane[ch:2 * ch]
    et_exp = per_lane[2 * ch:2 * ch + 1]
    mask = mask_ref[d]

    gw = SSD_HEADS_PER_GROUP * SSD_HEAD_DIM
    lane = lax.broadcasted_iota(jnp.int32, (1, gw), 1)
    for grp in range(SSD_N_GROUPS):
        bg = b_ref[:, grp * SSD_D_STATE:(grp + 1) * SSD_D_STATE]
        cg = c_ref[:, grp * SSD_D_STATE:(grp + 1) * SSD_D_STATE]
        xg = x_ref[:, grp * gw:(grp + 1) * gw]
        cb = _dot_nt(cg, bg)
        st = state_scr[d, grp]
        y = _dot(cg, st.astype(BF16)) * ea_exp[:, grp * gw:(grp + 1) * gw]
        ms, xs = [], []
        for hh in range(SSD_HEADS_PER_GROUP):
            h = grp * SSD_HEADS_PER_GROUP + hh
            seg = acs[:, h:h + 1] - acs_t[h:h + 1, :] + mask
            ms.append((cb * jnp.exp(seg) * dt_t[h:h + 1, :]).astype(BF16))
            in_head = (lane >= hh * SSD_HEAD_DIM) & (lane < (hh + 1) * SSD_HEAD_DIM)
            xs.append(jnp.where(in_head, xg, jnp.zeros_like(xg)))
        y = y + _dot(jnp.concatenate(ms, axis=1), jnp.concatenate(xs, axis=0))
        y_ref[:, grp * gw:(grp + 1) * gw] = y.astype(BF16)
        xw = (xg.astype(F32) * w_exp[:, grp * gw:(grp + 1) * gw]).astype(BF16)
        state_scr[d, grp] = st * et_exp[:, grp * gw:(grp + 1) * gw] + _dot_tn(bg, xw)


def _ssd_scan_kernel(xf_ref, bf_ref, cf_ref, sf_ref, xb_ref, bb_ref, cb_ref, sb_ref, dtb_ref,
                     alog_ref, tri_ref, mask_ref, exp_ref, yf_ref, yb_ref, state_scr, *, n_chunks,
                     firsts, lasts, valid):
    g = pl.program_id(0)
    r = n_chunks - 1 - g

    @pl.when(_is_any(g, firsts))
    def _():
        state_scr[0] = jnp.zeros(state_scr.shape[1:], F32)

    @pl.when(_is_any(r, lasts))
    def _():
        state_scr[1] = jnp.zeros(state_scr.shape[1:], F32)

    shared = (dtb_ref, alog_ref, tri_ref, mask_ref, exp_ref)
    _ssd_chunk(0, g, xf_ref, bf_ref, cf_ref, sf_ref, *shared, yf_ref, state_scr, valid)
    _ssd_chunk(1, r, xb_ref, bb_ref, cb_ref, sb_ref, *shared, yb_ref, state_scr, valid)


def _ssd_scan(xbc, col_x, small, dt_bias, a_log, layout):
    rows = xbc.shape[0]
    ch = SSD_CHUNK
    d_inner = SSD_N_GROUPS * SSD_HEADS_PER_GROUP * SSD_HEAD_DIM
    gn = SSD_N_GROUPS * SSD_D_STATE
    n_heads = SSD_N_GROUPS * SSD_HEADS_PER_GROUP
    n_chunks = rows // ch
    firsts = tuple(s // ch for s, _ in layout["spans"])
    lasts = tuple((s + n) // ch - 1 for s, n in layout["spans"])

    idx = np.arange(ch)
    lower = (idx[:, None] >= idx[None, :])
    tri = np.stack([lower, lower.T]).astype(np.float32)
    mask = np.where(tri > 0, 0.0, NEG_INF).astype(np.float32)
    expand = np.zeros((2 * LANES, d_inner), np.float32)
    for h in range(n_heads):
        expand[h, h * SSD_HEAD_DIM:(h + 1) * SSD_HEAD_DIM] = 1.0
        expand[LANES + h, h * SSD_HEAD_DIM:(h + 1) * SSD_HEAD_DIM] = 1.0
    pad = ((0, 0), (0, 0), (0, LANES - n_heads))
    dtb = jnp.pad(dt_bias.reshape(2, 1, n_heads), pad)
    alog = jnp.pad(a_log.reshape(2, 1, n_heads), pad)

    def chunk_specs(cmap):
        return [
            pl.BlockSpec((ch, d_inner), lambda g: (cmap(g), col_x // d_inner)),
            pl.BlockSpec((ch, gn), lambda g: (cmap(g), (col_x + d_inner) // gn)),
            pl.BlockSpec((ch, gn), lambda g: (cmap(g), (col_x + d_inner) // gn + 1)),
            pl.BlockSpec((ch, LANES), lambda g: (cmap(g), 0)),
        ]

    def fwd(g):
        return g

    def rev(g):
        return n_chunks - 1 - g

    kern = functools.partial(_ssd_scan_kernel, n_chunks=n_chunks, firsts=firsts, lasts=lasts,
                             valid=layout["valid"])
    return pl.pallas_call(
        kern,
        grid=(n_chunks,),
        in_specs=chunk_specs(fwd) + chunk_specs(rev) + [
            pl.BlockSpec((2, 1, LANES), lambda g: (0, 0, 0)),
            pl.BlockSpec((2, 1, LANES), lambda g: (0, 0, 0)),
            pl.BlockSpec((2, ch, ch), lambda g: (0, 0, 0)),
            pl.BlockSpec((2, ch, ch), lambda g: (0, 0, 0)),
            pl.BlockSpec((2 * LANES, d_inner), lambda g: (0, 0)),
        ],
        out_specs=[pl.BlockSpec((ch, d_inner), lambda g: (fwd(g), 0)),
                   pl.BlockSpec((ch, d_inner), lambda g: (rev(g), 0))],
        out_shape=[jax.ShapeDtypeStruct((rows, d_inner), BF16)] * 2,
        scratch_shapes=[
            pltpu.VMEM((2, SSD_N_GROUPS, SSD_D_STATE, SSD_HEADS_PER_GROUP * SSD_HEAD_DIM), F32)],
        compiler_params=_cparams("arbitrary"),
        name="ssd_scan",
    )(xbc, xbc, xbc, small, xbc, xbc, xbc, small, dtb, alog, jnp.asarray(tri, BF16),
      jnp.asarray(mask), jnp.asarray(expand, BF16))


def _gla_levels():
    lv, m = [], GLA_CHUNK // 2
    while m >= 1:
        lv.append(m)
        m //= 2
    return tuple(lv)


def _gla_level_exponent(cs_scr, cs_t, t, m, reverse):
    ch = cs_t.shape[0]
    off = m - 1 if reverse else m

    def ref_tile(r):
        return jnp.broadcast_to(cs_scr[t, pl.ds(r, 1), :], (8, LANES))

    if m >= 8:
        pieces = []
        for p in range(ch // (2 * m)):
            lo = 2 * m * p
            ref = jnp.concatenate([ref_tile(lo + off)] * (m // 8), axis=0)
            early, late = cs_t[lo:lo + m], cs_t[lo + m:lo + 2 * m]
            pieces += [early - ref, ref - late] if reverse else [ref - early, late - ref]
        return jnp.concatenate(pieces, axis=0)

    row = lax.broadcasted_iota(jnp.int32, (ch, 1), 0)
    in_late = ((row // m) & 1) == 1
    is_target = in_late != reverse
    if m == 1:
        x3 = cs_t.reshape(ch // 8, 8, LANES)
        nb = pltpu.roll(x3, 1 if reverse else 7, 1).reshape(ch, LANES)
        return jnp.where(is_target, 0.0, nb - cs_t)
    sub = lax.broadcasted_iota(jnp.int32, (8, 1), 0)
    pieces = []
    for p in range(ch // 8):
        ref = ref_tile(8 * p + off)
        for q in range(1, 8 // (2 * m)):
            ref = jnp.where(sub >= 2 * m * q, ref_tile(8 * p + 2 * m * q + off), ref)
        pieces.append(ref)
    delta = cs_t - jnp.concatenate(pieces, axis=0)
    return jnp.where(is_target, delta, -delta)


def _gla_level_operand(q_scr, k_ref, sl, m, reverse):
    ch = q_scr.shape[0]
    if m >= 16:
        pieces = []
        for p in range(ch // (2 * m)):
            lo = 2 * m * p
            early, late = slice(lo, lo + m), slice(lo + m, lo + 2 * m)
            if reverse:
                pieces += [q_scr[early, sl], k_ref[late, sl]]
            else:
                pieces += [k_ref[early, sl], q_scr[late, sl]]
        return jnp.concatenate(pieces, axis=0)
    row = lax.broadcasted_iota(jnp.int32, (ch, 1), 0)
    is_target = (((row // m) & 1) == 1) != reverse
    return jnp.where(is_target, q_scr[:, sl], k_ref[:, sl])


def _gla_scan_kernel(q_ref, k_ref, v_ref, small_ref, gu_ref, gb_ref, tri_ref, lev_ref, o_ref,
                     state_scr, cs_scr, att_scr, diag_scr, q_scr, *, reverse, n_chunks, firsts, lasts,
                     valid, dk, dv):
    g = pl.program_id(0)
    c = n_chunks - 1 - g if reverse else g
    ch = GLA_CHUNK
    levels = _gla_levels()
    tiles_per_head = dk // LANES

    @pl.when(_is_any(c, lasts if reverse else firsts))
    def _():
        state_scr[...] = jnp.zeros_like(state_scr)

    ok = _row_valid(c * ch, ch, valid)
    u = _dot(small_ref[...].astype(BF16), gu_ref[...]) + gb_ref[...]
    gate = jnp.where(ok, -_softplus(-u) / GLA_GATE_TAU, 0.0)
    cs = _dot_split2_lhs(tri_ref[...], gate)
    n_tiles = cs.shape[1] // LANES
    for t in range(n_tiles):
        cs_scr[t] = cs[:, t * LANES:(t + 1) * LANES]
    total = jnp.sum(gate, axis=0, keepdims=True)
    lev = lev_ref[...]
    q_scr[...] = q_ref[...] * jnp.asarray(dk ** -0.5, BF16)

    half = ch // 2
    tgt, src = (slice(0, half), slice(half, ch)) if reverse else (slice(half, ch), slice(0, half))
    quads = (slice(0, half), slice(half, ch))

    @pl.when(g == 0)
    def _():
        att_scr[:, src, tgt] = jnp.zeros((GLA_N_HEADS, half, half), BF16)

    for i, m in enumerate(levels):
        e = [jnp.exp(_gla_level_exponent(cs_scr, cs[:, t * LANES:(t + 1) * LANES], t, m, reverse)
                     ).astype(BF16) for t in range(n_tiles)]
        for h in range(GLA_N_HEADS):
            sl = slice(h * dk, (h + 1) * dk)
            e_h = jnp.concatenate(e[h * tiles_per_head:(h + 1) * tiles_per_head], axis=1)
            x = _gla_level_operand(q_scr, k_ref, sl, m, reverse) * e_h
            if i == 0:
                att_scr[h, tgt, src] = _dot_nt(x[tgt], x[src]).astype(BF16)
                continue
            for qi, qs in enumerate(quads):
                p = _dot_nt(x[qs], x[qs])
                diag_scr[h, qi] = jnp.where(lev == i, p, 0.0 if i == 1 else diag_scr[h, qi])

    for h in range(GLA_N_HEADS):
        sl = slice(h * dk, (h + 1) * dk)
        vs = slice(h * dv, (h + 1) * dv)
        vh = v_ref[:, vs]
        for qi, qs in enumerate(quads):
            p = _dot_nt(q_scr[qs, sl], k_ref[qs, sl])
            att_scr[h, qs, qs] = jnp.where(lev == len(levels), p, diag_scr[h, qi]).astype(BF16)
        st = state_scr[h]
        qd = q_scr[:, sl] * jnp.exp(cs[:, sl]).astype(BF16)
        o = _dot(att_scr[h], vh) + _dot_nt(qd, st.astype(BF16))
        o_ref[:, vs] = o.astype(BF16)
        kd = k_ref[:, sl] * jnp.exp(total[:, sl] - cs[:, sl]).astype(BF16)
        state_scr[h] = st * jnp.exp(total[:, sl]) + _dot_tn(vh, kd)


def _gla_scan(big, small, gate_up, gate_b, layout, col_q, col_k, col_v, dk, dv, reverse):
    rows = big.shape[0]
    ch = GLA_CHUNK
    n_chunks = rows // ch
    firsts = tuple(s // ch for s, _ in layout["spans"])
    lasts = tuple((s + n) // ch - 1 for s, n in layout["spans"])
    hk = GLA_N_HEADS * dk
    hv = GLA_N_HEADS * dv
    levels = _gla_levels()

    idx = np.arange(ch)
    l_, s_ = idx[:, None], idx[None, :]
    seen = (l_ <= s_) if reverse else (l_ >= s_)
    x = l_ ^ s_
    top = np.floor(np.log2(np.maximum(x, 1))).astype(np.int64)
    lvl = np.where(x == 0, len(levels), (len(levels) - 1) - top)
    lev = np.where(seen, lvl, -1).astype(np.int32)[:ch // 2, :ch // 2]

    lo = 32 + GLA_GATE_RANK * int(reverse)
    gu = jnp.zeros((LANES, hk), F32).at[lo:lo + GLA_GATE_RANK].set(gate_up).astype(BF16)

    def cmap(g):
        return n_chunks - 1 - g if reverse else g

    kern = functools.partial(_gla_scan_kernel, reverse=reverse, n_chunks=n_chunks, firsts=firsts,
                             lasts=lasts, valid=layout["valid"], dk=dk, dv=dv)
    return pl.pallas_call(
        kern,
        grid=(n_chunks,),
        in_specs=[
            pl.BlockSpec((ch, hk), lambda g: (cmap(g), col_q // hk)),
            pl.BlockSpec((ch, hk), lambda g: (cmap(g), col_k // hk)),
            pl.BlockSpec((ch, hv), lambda g: (cmap(g), col_v // hv)),
            pl.BlockSpec((ch, LANES), lambda g: (cmap(g), 0)),
            pl.BlockSpec((LANES, hk), lambda g: (0, 0)),
            pl.BlockSpec((1, hk), lambda g: (0, 0)),
            pl.BlockSpec((ch, ch), lambda g: (0, 0)),
            pl.BlockSpec((ch // 2, ch // 2), lambda g: (0, 0)),
        ],
        out_specs=pl.BlockSpec((ch, hv), lambda g: (cmap(g), 0)),
        out_shape=jax.ShapeDtypeStruct((rows, hv), BF16),
        scratch_shapes=[
            pltpu.VMEM((GLA_N_HEADS, dv, dk), F32),
            pltpu.VMEM((hk // LANES, ch, LANES), F32),
            pltpu.VMEM((GLA_N_HEADS, ch, ch), BF16),
            pltpu.VMEM((GLA_N_HEADS, 2, ch // 2, ch // 2), F32),
            pltpu.VMEM((ch, hk), BF16),
        ],
        compiler_params=_cparams("arbitrary"),
        name="gla_scan_bwd" if reverse else "gla_scan_fwd",
    )(big, big, big, small, gu, gate_b.reshape(1, hk), jnp.asarray(seen.astype(np.float32), BF16),
      jnp.asarray(lev))


def _merge_kernel(ysf_ref, ysb_ref, xs_ref, z_ref, ogf_ref, ogb_ref, ogate_ref, ms_ref, mg_ref, dskip_ref,
                  snw_ref, gnw_ref, ws_ref, wg_ref, o_ref, a_ssd, a_gla, *, dv):
    s = pl.program_id(0)
    tm = a_ssd.shape[1]

    @pl.when(s == 0)
    def _():
        a_ssd[1] = jnp.zeros(a_ssd.shape[1:], BF16)
        a_gla[1] = jnp.zeros(a_gla.shape[1:], BF16)

    slot = s % 2
    for r0 in range(0, tm, PROLOGUE_ROWS):
        rs = slice(r0, r0 + PROLOGUE_ROWS)
        y = ysf_ref[rs, :].astype(F32) + ysb_ref[rs, :].astype(F32)
        y = y + xs_ref[rs, :].astype(F32) * dskip_ref[...]
        y = y * _silu(z_ref[rs, :].astype(F32))
        ms = jnp.mean(y * y, axis=-1, keepdims=True)
        a_ssd[slot, rs, :] = (y * lax.rsqrt(ms + EPS) * snw_ref[...]).astype(BF16)
        for h in range(GLA_N_HEADS):
            sl = slice(h * dv, (h + 1) * dv)
            o = ogf_ref[rs, sl].astype(F32) + ogb_ref[rs, sl].astype(F32)
            ms = jnp.mean(o * o, axis=-1, keepdims=True)
            o = o * lax.rsqrt(ms + EPS) * gnw_ref[...]
            a_gla[slot, rs, sl] = (o * _silu(ogate_ref[rs, sl].astype(F32))).astype(BF16)

    y_ssd = _dot(a_ssd[1 - slot], ws_ref[...])
    y_gla = _dot(a_gla[1 - slot], wg_ref[...])
    merged = _sigmoid(ms_ref[...].astype(F32)) * y_ssd
    merged = merged + _sigmoid(mg_ref[...].astype(F32)) * y_gla
    o_ref[...] = merged.astype(BF16)


def _merge(y_fwd, y_bwd, big, o_fwd, o_bwd, d_skip, ssd_norm_w, gla_norm_w, w_ssd, w_gla, col_x, col_og,
           col_merge, tm=256):
    rows, d = big.shape[0], w_ssd.shape[1]
    di = w_ssd.shape[0]
    dvt = w_gla.shape[0]
    dv = dvt // GLA_N_HEADS
    n_tiles = rows // tm
    kern = functools.partial(_merge_kernel, dv=dv)

    def cur(s):
        return jnp.minimum(s, n_tiles - 1)

    def prev(s):
        return jnp.maximum(s - 1, 0)

    resident = pl.Buffered(1)
    return pl.pallas_call(
        kern,
        grid=(n_tiles + 1,),
        in_specs=[
            pl.BlockSpec((tm, di), lambda s: (cur(s), 0)),
            pl.BlockSpec((tm, di), lambda s: (cur(s), 0)),
            pl.BlockSpec((tm, di), lambda s: (cur(s), col_x // di)),
            pl.BlockSpec((tm, di), lambda s: (cur(s), 0)),
            pl.BlockSpec((tm, dvt), lambda s: (cur(s), 0)),
            pl.BlockSpec((tm, dvt), lambda s: (cur(s), 0)),
            pl.BlockSpec((tm, dvt), lambda s: (cur(s), col_og // dvt)),
            pl.BlockSpec((tm, d), lambda s: (prev(s), col_merge // d)),
            pl.BlockSpec((tm, d), lambda s: (prev(s), col_merge // d + 1)),
            pl.BlockSpec((1, di), lambda s: (0, 0)),
            pl.BlockSpec((1, di), lambda s: (0, 0)),
            pl.BlockSpec((1, dv), lambda s: (0, 0)),
            pl.BlockSpec((di, d), lambda s: (0, 0), pipeline_mode=resident),
            pl.BlockSpec((dvt, d), lambda s: (0, 0), pipeline_mode=resident),
        ],
        out_specs=pl.BlockSpec((tm, d), lambda s: (prev(s), 0)),
        out_shape=jax.ShapeDtypeStruct((rows, d), BF16),
        scratch_shapes=[pltpu.VMEM((2, tm, di), BF16), pltpu.VMEM((2, tm, dvt), BF16)],
        compiler_params=_cparams("arbitrary"),
        name="merge",
    )(y_fwd, y_bwd, big, big, o_fwd, o_bwd, big, big, big,
      jnp.repeat(d_skip, SSD_HEAD_DIM).reshape(1, di), ssd_norm_w.reshape(1, di),
      gla_norm_w.reshape(1, dv), w_ssd, w_gla)


def _mm_res_kernel(*refs):
    *aw_refs, h_ref, o_ref = refs
    n_terms = len(aw_refs) // 2
    acc = h_ref[...]
    for a_ref, w_ref in zip(aw_refs[:n_terms], aw_refs[n_terms:]):
        acc = acc + _dot(a_ref[...], w_ref[...])
    o_ref[...] = acc


def _mm_res(a_list, w_list, h, n_split, tm=512):
    rows = h.shape[0]
    n = w_list[0].shape[1]
    tn = n // n_split
    a_specs = [pl.BlockSpec((tm, a.shape[1]), lambda j, i: (i, 0)) for a in a_list]
    w_specs = [pl.BlockSpec((w.shape[0], tn), lambda j, i: (0, j)) for w in w_list]
    return pl.pallas_call(
        _mm_res_kernel,
        grid=(n_split, rows // tm),
        in_specs=a_specs + w_specs + [pl.BlockSpec((tm, tn), lambda j, i: (i, j))],
        out_specs=pl.BlockSpec((tm, tn), lambda j, i: (i, j)),
        out_shape=jax.ShapeDtypeStruct((rows, n), F32),
        compiler_params=_cparams("arbitrary", "arbitrary"),
        name="mm_res",
    )(*a_list, *w_list, h)


def _ffn_up_kernel(h_ref, prev_ref, next_ref, nw_ref, wg_ref, wu_ref, cwg_ref, cwu_ref, cbg_ref,
                   cbu_ref, o_ref, xn_scr, *, valid):
    tm = h_ref.shape[0]

    def norm(x):
        ms = jnp.mean(x * x, axis=-1, keepdims=True)
        return (x * lax.rsqrt(ms + EPS) * nw_ref[...]).astype(BF16)

    @pl.when(pl.program_id(1) == 0)
    def _():
        xn_scr[0:HALO, :] = norm(prev_ref[...])
        for r0 in range(0, tm, PROLOGUE_ROWS):
            xn_scr[HALO + r0:HALO + r0 + PROLOGUE_ROWS, :] = norm(h_ref[r0:r0 + PROLOGUE_ROWS, :])
        xn_scr[HALO + tm:2 * HALO + tm, :] = norm(next_ref[...])

    n = tm + 2 * HALO

    def conv(u, w_ref, b_ref, cs):
        y = b_ref[:, cs] + pltpu.roll(u, 1, 0)[HALO:HALO + tm] * w_ref[0:1, cs]
        y = y + u[HALO:HALO + tm] * w_ref[1:2, cs]
        return y + pltpu.roll(u, n - 1, 0)[HALO:HALO + tm] * w_ref[2:3, cs]

    ok = _row_valid(pl.program_id(0) * tm, tm, valid)
    for c0 in range(0, o_ref.shape[1], FFN_COL_CHUNK):
        cs = slice(c0, c0 + FFN_COL_CHUNK)
        gate = conv(_dot_row_pieces(xn_scr, wg_ref, cs), cwg_ref, cbg_ref, cs)
        up = conv(_dot_row_pieces(xn_scr, wu_ref, cs), cwu_ref, cbu_ref, cs)
        o_ref[:, cs] = jnp.where(ok, _silu(gate) * up, 0.0).astype(BF16)


def _ffn_up(h, norm_w, w_up, conv_w, conv_b, layout, tf=512):
    rows, d = h.shape
    dff = w_up.shape[1] // 2
    tm = ROW_TILE
    hb = tm // HALO
    last = rows // HALO - 1
    nj = dff // tf
    kern = functools.partial(_ffn_up_kernel, valid=layout["valid"])
    return pl.pallas_call(
        kern,
        grid=(rows // tm, nj),
        in_specs=[
            pl.BlockSpec((tm, d), lambda i, j: (i, 0)),
            pl.BlockSpec((HALO, d), lambda i, j: (jnp.maximum(i * hb - 1, 0), 0)),
            pl.BlockSpec((HALO, d), lambda i, j: (jnp.minimum((i + 1) * hb, last), 0)),
            pl.BlockSpec((1, d), lambda i, j: (0, 0)),
            pl.BlockSpec((d, tf), lambda i, j: (0, j)),
            pl.BlockSpec((d, tf), lambda i, j: (0, nj + j)),
            pl.BlockSpec((3, tf), lambda i, j: (0, j)),
            pl.BlockSpec((3, tf), lambda i, j: (0, nj + j)),
            pl.BlockSpec((1, tf), lambda i, j: (0, j)),
            pl.BlockSpec((1, tf), lambda i, j: (0, nj + j)),
        ],
        out_specs=pl.BlockSpec((tm, tf), lambda i, j: (i, j)),
        out_shape=jax.ShapeDtypeStruct((rows, dff), BF16),
        scratch_shapes=[pltpu.VMEM((tm + 2 * HALO, d), BF16)],
        compiler_params=_cparams("parallel", "arbitrary"),
        name="ffn_up",
    )(h, h, h, norm_w.reshape(1, d), w_up, w_up, conv_w, conv_w,
      conv_b.reshape(1, 2 * dff), conv_b.reshape(1, 2 * dff))


def _final_norm_kernel(h_ref, nw_ref, o_ref):
    x = h_ref[...]
    ms = jnp.mean(x * x, axis=-1, keepdims=True)
    o_ref[0] = x * lax.rsqrt(ms + EPS) * nw_ref[...]


def _final_norm(h, norm_w, first_row, batch, length, tr=ROW_ALIGN):
    d = h.shape[1]
    per_seq = (ROW_ALIGN + length) // tr
    b0 = (first_row + ROW_ALIGN) // tr
    return pl.pallas_call(
        _final_norm_kernel,
        grid=(batch, length // tr),
        in_specs=[
            pl.BlockSpec((tr, d), lambda b, t: (b0 + b * per_seq + t, 0)),
            pl.BlockSpec((1, d), lambda b, t: (0, 0)),
        ],
        out_specs=pl.BlockSpec((1, tr, d), lambda b, t: (b, t, 0)),
        out_shape=jax.ShapeDtypeStruct((batch, length, d), F32),
        compiler_params=_cparams("parallel", "parallel"),
        name="final_norm",
    )(h, norm_w.reshape(1, d))


def _make_layout(groups):
    seqs, valid, row = [], [], 0
    for batch, length in groups:
        assert length % ROW_ALIGN == 0
        for _ in range(batch):
            n = ROW_ALIGN + length
            seqs.append((row, n))
            valid.append((row + FRONT_PAD, row + n))
            row += n
    rows = (row // ROW_TILE + 1) * ROW_TILE
    spans = tuple(seqs) + ((row, rows - row),)
    return dict(seqs=tuple(seqs), spans=spans, valid=tuple(valid), used_rows=row, rows=rows)


def kernel(x_prompt, x_sample, meta_tokens, mix_norm_w, w_in, ssd_conv_w, ssd_conv_b, ssd_dt_bias,
           ssd_a_log, ssd_d, ssd_norm_w, ssd_w_out, gla_gate_up, gla_gate_b, gla_norm_w, gla_w_out,
           w_mix_out, ffn_norm_w, ffn_w_up, ffn_conv_w, ffn_conv_b, ffn_w_down, final_norm_w):
    depth, d = mix_norm_w.shape
    groups = [(x_prompt.shape[0], x_prompt.shape[1]), (x_sample.shape[0], x_sample.shape[1])]
    layout = _make_layout(groups)
    rows = layout["rows"]

    n_heads = ssd_dt_bias.shape[-1]
    d_inner = n_heads * SSD_HEAD_DIM
    conv_dim = ssd_conv_w.shape[-1]
    d_k = gla_gate_up.shape[-1]
    d_v = gla_w_out.shape[1]
    dk, dv = d_k // GLA_N_HEADS, d_v // GLA_N_HEADS
    sizes = (d_inner, conv_dim, n_heads, d_k, d_k, d_v, d_v, 2 * GLA_GATE_RANK, 2 * d)
    off = np.concatenate([[0], np.cumsum(sizes)])
    assert off[-1] == w_in.shape[-1]
    wide = [0, 1, 3, 4, 5, 6, 8]
    col, pos = {}, 0
    for s in wide:
        col[s] = pos
        pos += sizes[s]

    zero_front = jnp.zeros((FRONT_PAD, d), F32)
    pieces = []
    for x in (x_prompt, x_sample):
        for b in range(x.shape[0]):
            pieces += [zero_front, meta_tokens.astype(F32), x[b]]
    if rows > layout["used_rows"]:
        pieces.append(jnp.zeros((rows - layout["used_rows"], d), F32))
    h = jnp.concatenate(pieces, axis=0)

    for i in range(depth):
        w = w_in[i]
        w_big = jnp.concatenate([w[:, off[s]:off[s + 1]] for s in wide], axis=1).astype(BF16)
        w_small = jnp.concatenate(
            [w[:, off[2]:off[3]], w[:, off[7]:off[8]],
             jnp.zeros((d, LANES - n_heads - 2 * GLA_GATE_RANK), F32)], axis=1).astype(BF16)

        big, small = _inproj(h, mix_norm_w[i], w_big, w_small, ssd_conv_w[i], ssd_conv_b[i], col[1])
        y_fwd, y_bwd = _ssd_scan(big, col[1], small, ssd_dt_bias[i], ssd_a_log[i], layout)
        o_fwd, o_bwd = (_gla_scan(big, small, gla_gate_up[i, r], gla_gate_b[i, r], layout,
                                  col[3], col[4], col[5], dk, dv, reverse=bool(r)) for r in (0, 1))
        merged = _merge(y_fwd, y_bwd, big, o_fwd, o_bwd, ssd_d[i], ssd_norm_w[i], gla_norm_w[i],
                        ssd_w_out[i].astype(BF16), gla_w_out[i].astype(BF16), col[1], col[6], col[8])
        h = _mm_res([merged], [w_mix_out[i].astype(BF16)], h, n_split=1)
        act = _ffn_up(h, ffn_norm_w[i], ffn_w_up[i].astype(BF16), ffn_conv_w[i], ffn_conv_b[i],
                      layout)
        h = _mm_res([act], [ffn_w_down[i].astype(BF16)], h, n_split=2)

    outs, si = [], 0
    for batch, length in groups:
        outs.append(_final_norm(h, final_norm_w, layout["seqs"][si][0], batch, length))
        si += batch
    return tuple(outs)
```

```python
import functools

import numpy as np
import jax
import jax.numpy as jnp
from jax import lax
from jax.experimental import pallas as pl
from jax.experimental.pallas import tpu as pltpu

F32 = jnp.float32
BF16 = jnp.bfloat16

N_META = 16
EPS = 1e-6
SSD_HEAD_DIM = 64
SSD_N_GROUPS = 8
SSD_HEADS_PER_GROUP = 4
SSD_D_STATE = 128
GLA_N_HEADS = 4
GLA_GATE_RANK = 16
GLA_GATE_TAU = 16.0

LANES = 128
ROW_ALIGN = 256
FRONT_PAD = ROW_ALIGN - N_META
SSD_CHUNK = 128
GLA_CHUNK = 256
ROW_TILE = 1024
PROLOGUE_ROWS = 128
FFN_COL_CHUNK = 256
ROW_PIECES = 3
HALO = 16
VMEM_LIMIT = 56 * 1024 * 1024
NEG_INF = float("-inf")


def _cparams(*sem):
    return pltpu.CompilerParams(dimension_semantics=sem, vmem_limit_bytes=VMEM_LIMIT)


def _softplus(x):
    return jnp.maximum(x, 0.0) + jnp.log(1.0 + jnp.exp(-jnp.abs(x)))


def _sigmoid(x):
    return 0.5 + 0.5 * jnp.tanh(0.5 * x)


def _silu(x):
    hx = 0.5 * x
    return hx + hx * jnp.tanh(hx)


def _split3(x):
    x1 = x.astype(BF16)
    r1 = x - x1.astype(F32)
    x2 = r1.astype(BF16)
    x3 = (r1 - x2.astype(F32)).astype(BF16)
    return x1, x2, x3


def _dot(a, b):
    return jnp.dot(a, b, preferred_element_type=F32)


def _dot_row_pieces(x_ref, w_ref, cs, pieces=ROW_PIECES):
    n = x_ref.shape[0]
    step = n // pieces
    return jnp.concatenate(
        [_dot(x_ref[r0:r0 + step, :], w_ref[:, cs]) for r0 in range(0, n, step)], axis=0)


def _dot_nt(a, b):
    return lax.dot_general(a, b, (((1,), (1,)), ((), ())), preferred_element_type=F32)


def _dot_tn(a, b):
    return lax.dot_general(a, b, (((0,), (0,)), ((), ())), preferred_element_type=F32)


def _dot_exact_lhs(t, x):
    x1, x2, x3 = _split3(x)
    return _dot(t, x1) + _dot(t, x2) + _dot(t, x3)


def _dot_split2_lhs(t, x):
    x1 = x.astype(BF16)
    x2 = (x - x1.astype(F32)).astype(BF16)
    return _dot(t, x1) + _dot(t, x2)


def _dot_exact_rhs(x, t):
    x1, x2, x3 = _split3(x)
    return _dot(x1, t) + _dot(x2, t) + _dot(x3, t)


def _row_valid(row0, n, ranges):
    r = row0 + lax.broadcasted_iota(jnp.int32, (n, 1), 0)
    ok = None
    for lo, hi in ranges:
        m = (r >= lo) & (r < hi)
        ok = m if ok is None else (ok | m)
    return ok


def _is_any(c, values):
    ok = None
    for v in values:
        m = c == v
        ok = m if ok is None else (ok | m)
    return ok


def _inproj_kernel(h_ref, prev_ref, next_ref, nw_ref, wb_ref, ws_ref, cw_ref, cb_ref, big_ref,
                   small_ref, xn_scr, *, conv_blocks):
    tm = h_ref.shape[0]
    n = tm + 2 * HALO
    j = pl.program_id(1)

    def norm(x):
        ms = jnp.mean(x * x, axis=-1, keepdims=True)
        return (x * lax.rsqrt(ms + EPS) * nw_ref[...]).astype(BF16)

    @pl.when(j == 0)
    def _():
        xn_scr[0:HALO, :] = norm(prev_ref[...])
        for r0 in range(0, tm, PROLOGUE_ROWS):
            xn_scr[HALO + r0:HALO + r0 + PROLOGUE_ROWS, :] = norm(h_ref[r0:r0 + PROLOGUE_ROWS, :])
        xn_scr[HALO + tm:2 * HALO + tm, :] = norm(next_ref[...])
        small_ref[...] = _dot(xn_scr[HALO:HALO + tm, :], ws_ref[...])

    is_conv = (j >= conv_blocks[0]) & (j < conv_blocks[1])

    @pl.when(is_conv)
    def _():
        for c0 in range(0, big_ref.shape[1], FFN_COL_CHUNK):
            cs = slice(c0, c0 + FFN_COL_CHUNK)
            u = _dot_row_pieces(xn_scr, wb_ref, cs)
            y = cb_ref[:, cs] + pltpu.roll(u, 1, 0)[HALO:HALO + tm] * cw_ref[0:1, cs]
            y = y + u[HALO:HALO + tm] * cw_ref[1:2, cs]
            y = y + pltpu.roll(u, n - 1, 0)[HALO:HALO + tm] * cw_ref[2:3, cs]
            big_ref[:, cs] = _silu(y).astype(BF16)

    @pl.when(jnp.logical_not(is_conv))
    def _():
        big_ref[...] = _dot(xn_scr[HALO:HALO + tm, :], wb_ref[...]).astype(BF16)


def _inproj(h, norm_w, w_big, w_small, conv_w, conv_b, conv_col0, tn=2048):
    rows, d = h.shape
    nb = w_big.shape[1]
    tm = ROW_TILE
    hb = tm // HALO
    last = rows // HALO - 1
    c = conv_w.shape[1]
    conv_blocks = (conv_col0 // tn, (conv_col0 + c) // tn)
    n_conv = conv_blocks[1] - conv_blocks[0]

    def conv_map(i, j):
        return (0, jnp.clip(j - conv_blocks[0], 0, n_conv - 1))

    kern = functools.partial(_inproj_kernel, conv_blocks=conv_blocks)
    return pl.pallas_call(
        kern,
        grid=(rows // tm, nb // tn),
        in_specs=[
            pl.BlockSpec((tm, d), lambda i, j: (i, 0)),
            pl.BlockSpec((HALO, d), lambda i, j: (jnp.maximum(i * hb - 1, 0), 0)),
            pl.BlockSpec((HALO, d), lambda i, j: (jnp.minimum((i + 1) * hb, last), 0)),
            pl.BlockSpec((1, d), lambda i, j: (0, 0)),
            pl.BlockSpec((d, tn), lambda i, j: (0, j)),
            pl.BlockSpec((d, LANES), lambda i, j: (0, 0)),
            pl.BlockSpec((3, tn), conv_map),
            pl.BlockSpec((1, tn), conv_map),
        ],
        out_specs=[
            pl.BlockSpec((tm, tn), lambda i, j: (i, j)),
            pl.BlockSpec((tm, LANES), lambda i, j: (i, 0)),
        ],
        out_shape=[
            jax.ShapeDtypeStruct((rows, nb), BF16),
            jax.ShapeDtypeStruct((rows, LANES), F32),
        ],
        scratch_shapes=[pltpu.VMEM((tm + 2 * HALO, d), BF16)],
        compiler_params=_cparams("parallel", "arbitrary"),
        name="inproj",
    )(h, h, h, norm_w.reshape(1, d), w_big, w_small, conv_w, conv_b.reshape(1, c))


def _ssd_chunk(d, c, x_ref, b_ref, c_ref, small_ref, dtb_ref, alog_ref, tri_ref, mask_ref, exp_ref,
               y_ref, state_scr, valid):
    ch = SSD_CHUNK
    ok = _row_valid(c * ch, ch, valid)
    dt = _softplus(small_ref[...] + dtb_ref[d])
    dt = jnp.where(ok, dt, 0.0)
    a = -jnp.exp(alog_ref[d]) * dt
    acs = _dot_exact_lhs(tri_ref[d], a)
    acs_t = _dot_exact_rhs(a.T, tri_ref[1 - d])
    dt_t = dt.T
    total = jnp.sum(a, axis=0, keepdims=True)
    w_in = jnp.exp(total - acs) * dt
    per_head = jnp.concatenate(
        [w_in, jnp.exp(acs), jnp.broadcast_to(jnp.exp(total), (8, LANES))], axis=0)
    hi = per_head.astype(BF16)
    lo = (per_head - hi.astype(F32)).astype(BF16)
    per_lane = _dot(jnp.concatenate([hi, lo], axis=1), exp_ref[...])
    w_exp = per_lane[0:ch]
    ea_exp = per_lane[ch:2 * ch]
    et_exp = per_lane[2 * ch:2 * ch + 1]
    mask = mask_ref[d]

    gw = SSD_HEADS_PER_GROUP * SSD_HEAD_DIM
    lane = lax.broadcasted_iota(jnp.int32, (1, gw), 1)
    for grp in range(SSD_N_GROUPS):
        bg = b_ref[:, grp * SSD_D_STATE:(grp + 1) * SSD_D_STATE]
        cg = c_ref[:, grp * SSD_D_STATE:(grp + 1) * SSD_D_STATE]
        xg = x_ref[:, grp * gw:(grp + 1) * gw]
        cb = _dot_nt(cg, bg)
        st = state_scr[d, grp]
        y = _dot(cg, st.astype(BF16)) * ea_exp[:, grp * gw:(grp + 1) * gw]
        ms, xs = [], []
        for hh in range(SSD_HEADS_PER_GROUP):
            h = grp * SSD_HEADS_PER_GROUP + hh
            seg = acs[:, h:h + 1] - acs_t[h:h + 1, :] + mask
            ms.append((cb * jnp.exp(seg) * dt_t[h:h + 1, :]).astype(BF16))
            in_head = (lane >= hh * SSD_HEAD_DIM) & (lane < (hh + 1) * SSD_HEAD_DIM)
            xs.append(jnp.where(in_head, xg, jnp.zeros_like(xg)))
        y = y + _dot(jnp.concatenate(ms, axis=1), jnp.concatenate(xs, axis=0))
        y_ref[:, grp * gw:(grp + 1) * gw] = y.astype(BF16)
        xw = (xg.astype(F32) * w_exp[:, grp * gw:(grp + 1) * gw]).astype(BF16)
        state_scr[d, grp] = st * et_exp[:, grp * gw:(grp + 1) * gw] + _dot_tn(bg, xw)


def _ssd_scan_kernel(xf_ref, bf_ref, cf_ref, sf_ref, xb_ref, bb_ref, cb_ref, sb_ref, dtb_ref,
                     alog_ref, tri_ref, mask_ref, exp_ref, yf_ref, yb_ref, state_scr, *, n_chunks,
                     firsts, lasts, valid):
    g = pl.program_id(0)
    r = n_chunks - 1 - g

    @pl.when(_is_any(g, firsts))
    def _():
        state_scr[0] = jnp.zeros(state_scr.shape[1:], F32)

    @pl.when(_is_any(r, lasts))
    def _():
        state_scr[1] = jnp.zeros(state_scr.shape[1:], F32)

    shared = (dtb_ref, alog_ref, tri_ref, mask_ref, exp_ref)
    _ssd_chunk(0, g, xf_ref, bf_ref, cf_ref, sf_ref, *shared, yf_ref, state_scr, valid)
    _ssd_chunk(1, r, xb_ref, bb_ref, cb_ref, sb_ref, *shared, yb_ref, state_scr, valid)


def _ssd_scan(xbc, col_x, small, dt_bias, a_log, layout):
    rows = xbc.shape[0]
    ch = SSD_CHUNK
    d_inner = SSD_N_GROUPS * SSD_HEADS_PER_GROUP * SSD_HEAD_DIM
    gn = SSD_N_GROUPS * SSD_D_STATE
    n_heads = SSD_N_GROUPS * SSD_HEADS_PER_GROUP
    n_chunks = rows // ch
    firsts = tuple(s // ch for s, _ in layout["spans"])
    lasts = tuple((s + n) // ch - 1 for s, n in layout["spans"])

    idx = np.arange(ch)
    lower = (idx[:, None] >= idx[None, :])
    tri = np.stack([lower, lower.T]).astype(np.float32)
    mask = np.where(tri > 0, 0.0, NEG_INF).astype(np.float32)
    expand = np.zeros((2 * LANES, d_inner), np.float32)
    for h in range(n_heads):
        expand[h, h * SSD_HEAD_DIM:(h + 1) * SSD_HEAD_DIM] = 1.0
        expand[LANES + h, h * SSD_HEAD_DIM:(h + 1) * SSD_HEAD_DIM] = 1.0
    pad = ((0, 0), (0, 0), (0, LANES - n_heads))
    dtb = jnp.pad(dt_bias.reshape(2, 1, n_heads), pad)
    alog = jnp.pad(a_log.reshape(2, 1, n_heads), pad)

    def chunk_specs(cmap):
        return [
            pl.BlockSpec((ch, d_inner), lambda g: (cmap(g), col_x // d_inner)),
            pl.BlockSpec((ch, gn), lambda g: (cmap(g), (col_x + d_inner) // gn)),
            pl.BlockSpec((ch, gn), lambda g: (cmap(g), (col_x + d_inner) // gn + 1)),
            pl.BlockSpec((ch, LANES), lambda g: (cmap(g), 0)),
        ]

    def fwd(g):
        return g

    def rev(g):
        return n_chunks - 1 - g

    kern = functools.partial(_ssd_scan_kernel, n_chunks=n_chunks, firsts=firsts, lasts=lasts,
                             valid=layout["valid"])
    return pl.pallas_call(
        kern,
        grid=(n_chunks,),
        in_specs=chunk_specs(fwd) + chunk_specs(rev) + [
            pl.BlockSpec((2, 1, LANES), lambda g: (0, 0, 0)),
            pl.BlockSpec((2, 1, LANES), lambda g: (0, 0, 0)),
            pl.BlockSpec((2, ch, ch), lambda g: (0, 0, 0)),
            pl.BlockSpec((2, ch, ch), lambda g: (0, 0, 0)),
            pl.BlockSpec((2 * LANES, d_inner), lambda g: (0, 0)),
        ],
        out_specs=[pl.BlockSpec((ch, d_inner), lambda g: (fwd(g), 0)),
                   pl.BlockSpec((ch, d_inner), lambda g: (rev(g), 0))],
        out_shape=[jax.ShapeDtypeStruct((rows, d_inner), BF16)] * 2,
        scratch_shapes=[
            pltpu.VMEM((2, SSD_N_GROUPS, SSD_D_STATE, SSD_HEADS_PER_GROUP * SSD_HEAD_DIM), F32)],
        compiler_params=_cparams("arbitrary"),
        name="ssd_scan",
    )(xbc, xbc, xbc, small, xbc, xbc, xbc, small, dtb, alog, jnp.asarray(tri, BF16),
      jnp.asarray(mask), jnp.asarray(expand, BF16))


def _gla_levels():
    lv, m = [], GLA_CHUNK // 2
    while m >= 1:
        lv.append(m)
        m //= 2
    return tuple(lv)


def _gla_level_exponent(cs_scr, cs_t, t, m, reverse):
    ch = cs_t.shape[0]
    off = m - 1 if reverse else m

    def ref_tile(r):
        return jnp.broadcast_to(cs_scr[t, pl.ds(r, 1), :], (8, LANES))

    if m >= 8:
        pieces = []
        for p in range(ch // (2 * m)):
            lo = 2 * m * p
            ref = jnp.concatenate([ref_tile(lo + off)] * (m // 8), axis=0)
            early, late = cs_t[lo:lo + m], cs_t[lo + m:lo + 2 * m]
            pieces += [early - ref, ref - late] if reverse else [ref - early, late - ref]
        return jnp.concatenate(pieces, axis=0)

    row = lax.broadcasted_iota(jnp.int32, (ch, 1), 0)
    in_late = ((row // m) & 1) == 1
    is_target = in_late != reverse
    if m == 1:
        x3 = cs_t.reshape(ch // 8, 8, LANES)
        nb = pltpu.roll(x3, 1 if reverse else 7, 1).reshape(ch, LANES)
        return jnp.where(is_target, 0.0, nb - cs_t)
    sub = lax.broadcasted_iota(jnp.int32, (8, 1), 0)
    pieces = []
    for p in range(ch // 8):
        ref = ref_tile(8 * p + off)
        for q in range(1, 8 // (2 * m)):
            ref = jnp.where(sub >= 2 * m * q, ref_tile(8 * p + 2 * m * q + off), ref)
        pieces.append(ref)
    delta = cs_t - jnp.concatenate(pieces, axis=0)
    return jnp.where(is_target, delta, -delta)


def _gla_level_operand(q_scr, k_ref, sl, m, reverse):
    ch = q_scr.shape[0]
    if m >= 16:
        pieces = []
        for p in range(ch // (2 * m)):
            lo = 2 * m * p
            early, late = slice(lo, lo + m), slice(lo + m, lo + 2 * m)
            if reverse:
                pieces += [q_scr[early, sl], k_ref[late, sl]]
            else:
                pieces += [k_ref[early, sl], q_scr[late, sl]]
        return jnp.concatenate(pieces, axis=0)
    row = lax.broadcasted_iota(jnp.int32, (ch, 1), 0)
    is_target = (((row // m) & 1) == 1) != reverse
    return jnp.where(is_target, q_scr[:, sl], k_ref[:, sl])


def _gla_scan_kernel(q_ref, k_ref, v_ref, small_ref, gu_ref, gb_ref, tri_ref, lev_ref, o_ref,
                     state_scr, cs_scr, att_scr, diag_scr, q_scr, *, reverse, n_chunks, firsts, lasts,
                     valid, dk, dv):
    g = pl.program_id(0)
    c = n_chunks - 1 - g if reverse else g
    ch = GLA_CHUNK
    levels = _gla_levels()
    tiles_per_head = dk // LANES

    @pl.when(_is_any(c, lasts if reverse else firsts))
    def _():
        state_scr[...] = jnp.zeros_like(state_scr)

    ok = _row_valid(c * ch, ch, valid)
    u = _dot(small_ref[...].astype(BF16), gu_ref[...]) + gb_ref[...]
    gate = jnp.where(ok, -_softplus(-u) / GLA_GATE_TAU, 0.0)
    cs = _dot_split2_lhs(tri_ref[...], gate)
    n_tiles = cs.shape[1] // LANES
    for t in range(n_tiles):
        cs_scr[t] = cs[:, t * LANES:(t + 1) * LANES]
    total = jnp.sum(gate, axis=0, keepdims=True)
    lev = lev_ref[...]
    q_scr[...] = q_ref[...] * jnp.asarray(dk ** -0.5, BF16)

    half = ch // 2
    tgt, src = (slice(0, half), slice(half, ch)) if reverse else (slice(half, ch), slice(0, half))
    quads = (slice(0, half), slice(half, ch))

    @pl.when(g == 0)
    def _():
        att_scr[:, src, tgt] = jnp.zeros((GLA_N_HEADS, half, half), BF16)

    for i, m in enumerate(levels):
        e = [jnp.exp(_gla_level_exponent(cs_scr, cs[:, t * LANES:(t + 1) * LANES], t, m, reverse)
                     ).astype(BF16) for t in range(n_tiles)]
        for h in range(GLA_N_HEADS):
            sl = slice(h * dk, (h + 1) * dk)
            e_h = jnp.concatenate(e[h * tiles_per_head:(h + 1) * tiles_per_head], axis=1)
            x = _gla_level_operand(q_scr, k_ref, sl, m, reverse) * e_h
            if i == 0:
                att_scr[h, tgt, src] = _dot_nt(x[tgt], x[src]).astype(BF16)
                continue
            for qi, qs in enumerate(quads):
                p = _dot_nt(x[qs], x[qs])
                diag_scr[h, qi] = jnp.where(lev == i, p, 0.0 if i == 1 else diag_scr[h, qi])

    for h in range(GLA_N_HEADS):
        sl = slice(h * dk, (h + 1) * dk)
        vs = slice(h * dv, (h + 1) * dv)
        vh = v_ref[:, vs]
        for qi, qs in enumerate(quads):
            p = _dot_nt(q_scr[qs, sl], k_ref[qs, sl])
            att_scr[h, qs, qs] = jnp.where(lev == len(levels), p, diag_scr[h, qi]).astype(BF16)
        st = state_scr[h]
        qd = q_scr[:, sl] * jnp.exp(cs[:, sl]).astype(BF16)
        o = _dot(att_scr[h], vh) + _dot_nt(qd, st.astype(BF16))
        o_ref[:, vs] = o.astype(BF16)
        kd = k_ref[:, sl] * jnp.exp(total[:, sl] - cs[:, sl]).astype(BF16)
        state_scr[h] = st * jnp.exp(total[:, sl]) + _dot_tn(vh, kd)


def _gla_scan(big, small, gate_up, gate_b, layout, col_q, col_k, col_v, dk, dv, reverse):
    rows = big.shape[0]
    ch = GLA_CHUNK
    n_chunks = rows // ch
    firsts = tuple(s // ch for s, _ in layout["spans"])
    lasts = tuple((s + n) // ch - 1 for s, n in layout["spans"])
    hk = GLA_N_HEADS * dk
    hv = GLA_N_HEADS * dv
    levels = _gla_levels()

    idx = np.arange(ch)
    l_, s_ = idx[:, None], idx[None, :]
    seen = (l_ <= s_) if reverse else (l_ >= s_)
    x = l_ ^ s_
    top = np.floor(np.log2(np.maximum(x, 1))).astype(np.int64)
    lvl = np.where(x == 0, len(levels), (len(levels) - 1) - top)
    lev = np.where(seen, lvl, -1).astype(np.int32)[:ch // 2, :ch // 2]

    lo = 32 + GLA_GATE_RANK * int(reverse)
    gu = jnp.zeros((LANES, hk), F32).at[lo:lo + GLA_GATE_RANK].set(gate_up).astype(BF16)

    def cmap(g):
        return n_chunks - 1 - g if reverse else g

    kern = functools.partial(_gla_scan_kernel, reverse=reverse, n_chunks=n_chunks, firsts=firsts,
                             lasts=lasts, valid=layout["valid"], dk=dk, dv=dv)
    return pl.pallas_call(
        kern,
        grid=(n_chunks,),
        in_specs=[
            pl.BlockSpec((ch, hk), lambda g: (cmap(g), col_q // hk)),
            pl.BlockSpec((ch, hk), lambda g: (cmap(g), col_k // hk)),
            pl.BlockSpec((ch, hv), lambda g: (cmap(g), col_v // hv)),
            pl.BlockSpec((ch, LANES), lambda g: (cmap(g), 0)),
            pl.BlockSpec((LANES, hk), lambda g: (0, 0)),
            pl.BlockSpec((1, hk), lambda g: (0, 0)),
            pl.BlockSpec((ch, ch), lambda g: (0, 0)),
            pl.BlockSpec((ch // 2, ch // 2), lambda g: (0, 0)),
        ],
        out_specs=pl.BlockSpec((ch, hv), lambda g: (cmap(g), 0)),
        out_shape=jax.ShapeDtypeStruct((rows, hv), BF16),
        scratch_shapes=[
            pltpu.VMEM((GLA_N_HEADS, dv, dk), F32),
            pltpu.VMEM((hk // LANES, ch, LANES), F32),
            pltpu.VMEM((GLA_N_HEADS, ch, ch), BF16),
            pltpu.VMEM((GLA_N_HEADS, 2, ch // 2, ch // 2), F32),
            pltpu.VMEM((ch, hk), BF16),
        ],
        compiler_params=_cparams("arbitrary"),
        name="gla_scan_bwd" if reverse else "gla_scan_fwd",
    )(big, big, big, small, gu, gate_b.reshape(1, hk), jnp.asarray(seen.astype(np.float32), BF16),
      jnp.asarray(lev))


def _merge_kernel(ysf_ref, ysb_ref, xs_ref, z_ref, ogf_ref, ogb_ref, ogate_ref, ms_ref, mg_ref, dskip_ref,
                  snw_ref, gnw_ref, ws_ref, wg_ref, o_ref, a_ssd, a_gla, *, dv):
    s = pl.program_id(0)
    tm = a_ssd.shape[1]

    @pl.when(s == 0)
    def _():
        a_ssd[1] = jnp.zeros(a_ssd.shape[1:], BF16)
        a_gla[1] = jnp.zeros(a_gla.shape[1:], BF16)

    slot = s % 2
    for r0 in range(0, tm, PROLOGUE_ROWS):
        rs = slice(r0, r0 + PROLOGUE_ROWS)
        y = ysf_ref[rs, :].astype(F32) + ysb_ref[rs, :].astype(F32)
        y = y + xs_ref[rs, :].astype(F32) * dskip_ref[...]
        y = y * _silu(z_ref[rs, :].astype(F32))
        ms = jnp.mean(y * y, axis=-1, keepdims=True)
        a_ssd[slot, rs, :] = (y * lax.rsqrt(ms + EPS) * snw_ref[...]).astype(BF16)
        for h in range(GLA_N_HEADS):
            sl = slice(h * dv, (h + 1) * dv)
            o = ogf_ref[rs, sl].astype(F32) + ogb_ref[rs, sl].astype(F32)
            ms = jnp.mean(o * o, axis=-1, keepdims=True)
            o = o * lax.rsqrt(ms + EPS) * gnw_ref[...]
            a_gla[slot, rs, sl] = (o * _silu(ogate_ref[rs, sl].astype(F32))).astype(BF16)

    y_ssd = _dot(a_ssd[1 - slot], ws_ref[...])
    y_gla = _dot(a_gla[1 - slot], wg_ref[...])
    merged = _sigmoid(ms_ref[...].astype(F32)) * y_ssd
    merged = merged + _sigmoid(mg_ref[...].astype(F32)) * y_gla
    o_ref[...] = merged.astype(BF16)


def _merge(y_fwd, y_bwd, big, o_fwd, o_bwd, d_skip, ssd_norm_w, gla_norm_w, w_ssd, w_gla, col_x, col_og,
           col_merge, tm=256):
    rows, d = big.shape[0], w_ssd.shape[1]
    di = w_ssd.shape[0]
    dvt = w_gla.shape[0]
    dv = dvt // GLA_N_HEADS
    n_tiles = rows // tm
    kern = functools.partial(_merge_kernel, dv=dv)

    def cur(s):
        return jnp.minimum(s, n_tiles - 1)

    def prev(s):
        return jnp.maximum(s - 1, 0)

    resident = pl.Buffered(1)
    return pl.pallas_call(
        kern,
        grid=(n_tiles + 1,),
        in_specs=[
            pl.BlockSpec((tm, di), lambda s: (cur(s), 0)),
            pl.BlockSpec((tm, di), lambda s: (cur(s), 0)),
            pl.BlockSpec((tm, di), lambda s: (cur(s), col_x // di)),
            pl.BlockSpec((tm, di), lambda s: (cur(s), 0)),
            pl.BlockSpec((tm, dvt), lambda s: (cur(s), 0)),
            pl.BlockSpec((tm, dvt), lambda s: (cur(s), 0)),
            pl.BlockSpec((tm, dvt), lambda s: (cur(s), col_og // dvt)),
            pl.BlockSpec((tm, d), lambda s: (prev(s), col_merge // d)),
            pl.BlockSpec((tm, d), lambda s: (prev(s), col_merge // d + 1)),
            pl.BlockSpec((1, di), lambda s: (0, 0)),
            pl.BlockSpec((1, di), lambda s: (0, 0)),
            pl.BlockSpec((1, dv), lambda s: (0, 0)),
            pl.BlockSpec((di, d), lambda s: (0, 0), pipeline_mode=resident),
            pl.BlockSpec((dvt, d), lambda s: (0, 0), pipeline_mode=resident),
        ],
        out_specs=pl.BlockSpec((tm, d), lambda s: (prev(s), 0)),
        out_shape=jax.ShapeDtypeStruct((rows, d), BF16),
        scratch_shapes=[pltpu.VMEM((2, tm, di), BF16), pltpu.VMEM((2, tm, dvt), BF16)],
        compiler_params=_cparams("arbitrary"),
        name="merge",
    )(y_fwd, y_bwd, big, big, o_fwd, o_bwd, big, big, big,
      jnp.repeat(d_skip, SSD_HEAD_DIM).reshape(1, di), ssd_norm_w.reshape(1, di),
      gla_norm_w.reshape(1, dv), w_ssd, w_gla)


def _mm_res_kernel(*refs):
    *aw_refs, h_ref, o_ref = refs
    n_terms = len(aw_refs) // 2
    acc = h_ref[...]
    for a_ref, w_ref in zip(aw_refs[:n_terms], aw_refs[n_terms:]):
        acc = acc + _dot(a_ref[...], w_ref[...])
    o_ref[...] = acc


def _mm_res(a_list, w_list, h, n_split, tm=512):
    rows = h.shape[0]
    n = w_list[0].shape[1]
    tn = n // n_split
    a_specs = [pl.BlockSpec((tm, a.shape[1]), lambda j, i: (i, 0)) for a in a_list]
    w_specs = [pl.BlockSpec((w.shape[0], tn), lambda j, i: (0, j)) for w in w_list]
    return pl.pallas_call(
        _mm_res_kernel,
        grid=(n_split, rows // tm),
        in_specs=a_specs + w_specs + [pl.BlockSpec((tm, tn), lambda j, i: (i, j))],
        out_specs=pl.BlockSpec((tm, tn), lambda j, i: (i, j)),
        out_shape=jax.ShapeDtypeStruct((rows, n), F32),
        compiler_params=_cparams("arbitrary", "arbitrary"),
        name="mm_res",
    )(*a_list, *w_list, h)


def _ffn_up_kernel(h_ref, prev_ref, next_ref, nw_ref, wg_ref, wu_ref, cwg_ref, cwu_ref, cbg_ref,
                   cbu_ref, o_ref, xn_scr, *, valid):
    tm = h_ref.shape[0]

    def norm(x):
        ms = jnp.mean(x * x, axis=-1, keepdims=True)
        return (x * lax.rsqrt(ms + EPS) * nw_ref[...]).astype(BF16)

    def normalise_tile():
        xn_scr[0:HALO, :] = norm(prev_ref[...])
        for r0 in range(0, tm, PROLOGUE_ROWS):
            xn_scr[HALO + r0:HALO + r0 + PROLOGUE_ROWS, :] = norm(h_ref[r0:r0 + PROLOGUE_ROWS, :])
        xn_scr[HALO + tm:2 * HALO + tm, :] = norm(next_ref[...])

    n = tm + 2 * HALO

    def conv(u, w_ref, b_ref, cs):
        y = b_ref[:, cs] + pltpu.roll(u, 1, 0)[HALO:HALO + tm] * w_ref[0:1, cs]
        y = y + u[HALO:HALO + tm] * w_ref[1:2, cs]
        return y + pltpu.roll(u, n - 1, 0)[HALO:HALO + tm] * w_ref[2:3, cs]

    def project():
        ok = _row_valid(pl.program_id(0) * tm, tm, valid)
        for c0 in range(0, o_ref.shape[1], FFN_COL_CHUNK):
            cs = slice(c0, c0 + FFN_COL_CHUNK)
            gate = conv(_dot_row_pieces(xn_scr, wg_ref, cs), cwg_ref, cbg_ref, cs)
            up = conv(_dot_row_pieces(xn_scr, wu_ref, cs), cwu_ref, cbu_ref, cs)
            o_ref[:, cs] = jnp.where(ok, _silu(gate) * up, 0.0).astype(BF16)

    @pl.when(pl.program_id(1) == 0)
    def _():
        normalise_tile()
        project()

    @pl.when(pl.program_id(1) != 0)
    def _():
        project()


def _ffn_up(h, norm_w, w_up, conv_w, conv_b, layout, tf=512):
    rows, d = h.shape
    dff = w_up.shape[1] // 2
    tm = ROW_TILE
    hb = tm // HALO
    last = rows // HALO - 1
    nj = dff // tf
    kern = functools.partial(_ffn_up_kernel, valid=layout["valid"])
    return pl.pallas_call(
        kern,
        grid=(rows // tm, nj),
        in_specs=[
            pl.BlockSpec((tm, d), lambda i, j: (i, 0)),
            pl.BlockSpec((HALO, d), lambda i, j: (jnp.maximum(i * hb - 1, 0), 0)),
            pl.BlockSpec((HALO, d), lambda i, j: (jnp.minimum((i + 1) * hb, last), 0)),
            pl.BlockSpec((1, d), lambda i, j: (0, 0)),
            pl.BlockSpec((d, tf), lambda i, j: (0, j)),
            pl.BlockSpec((d, tf), lambda i, j: (0, nj + j)),
            pl.BlockSpec((3, tf), lambda i, j: (0, j)),
            pl.BlockSpec((3, tf), lambda i, j: (0, nj + j)),
            pl.BlockSpec((1, tf), lambda i, j: (0, j)),
            pl.BlockSpec((1, tf), lambda i, j: (0, nj + j)),
        ],
        out_specs=pl.BlockSpec((tm, tf), lambda i, j: (i, j)),
        out_shape=jax.ShapeDtypeStruct((rows, dff), BF16),
        scratch_shapes=[pltpu.VMEM((tm + 2 * HALO, d), BF16)],
        compiler_params=_cparams("parallel", "arbitrary"),
        name="ffn_up",
    )(h, h, h, norm_w.reshape(1, d), w_up, w_up, conv_w, conv_w,
      conv_b.reshape(1, 2 * dff), conv_b.reshape(1, 2 * dff))


def _final_norm_kernel(h_ref, nw_ref, o_ref):
    x = h_ref[...]
    ms = jnp.mean(x * x, axis=-1, keepdims=True)
    o_ref[0] = x * lax.rsqrt(ms + EPS) * nw_ref[...]


def _final_norm(h, norm_w, first_row, batch, length, tr=ROW_ALIGN):
    d = h.shape[1]
    per_seq = (ROW_ALIGN + length) // tr
    b0 = (first_row + ROW_ALIGN) // tr
    return pl.pallas_call(
        _final_norm_kernel,
        grid=(batch, length // tr),
        in_specs=[
            pl.BlockSpec((tr, d), lambda b, t: (b0 + b * per_seq + t, 0)),
            pl.BlockSpec((1, d), lambda b, t: (0, 0)),
        ],
        out_specs=pl.BlockSpec((1, tr, d), lambda b, t: (b, t, 0)),
        out_shape=jax.ShapeDtypeStruct((batch, length, d), F32),
        compiler_params=_cparams("parallel", "parallel"),
        name="final_norm",
    )(h, norm_w.reshape(1, d))


def _make_layout(groups):
    seqs, valid, row = [], [], 0
    for batch, length in groups:
        assert length % ROW_ALIGN == 0
        for _ in range(batch):
            n = ROW_ALIGN + length
            seqs.append((row, n))
            valid.append((row + FRONT_PAD, row + n))
            row += n
    rows = (row // ROW_TILE + 1) * ROW_TILE
    spans = tuple(seqs) + ((row, rows - row),)
    return dict(seqs=tuple(seqs), spans=spans, valid=tuple(valid), used_rows=row, rows=rows)


def kernel(x_prompt, x_sample, meta_tokens, mix_norm_w, w_in, ssd_conv_w, ssd_conv_b, ssd_dt_bias,
           ssd_a_log, ssd_d, ssd_norm_w, ssd_w_out, gla_gate_up, gla_gate_b, gla_norm_w, gla_w_out,
           w_mix_out, ffn_norm_w, ffn_w_up, ffn_conv_w, ffn_conv_b, ffn_w_down, final_norm_w):
    depth, d = mix_norm_w.shape
    groups = [(x_prompt.shape[0], x_prompt.shape[1]), (x_sample.shape[0], x_sample.shape[1])]
    layout = _make_layout(groups)
    rows = layout["rows"]

    n_heads = ssd_dt_bias.shape[-1]
    d_inner = n_heads * SSD_HEAD_DIM
    conv_dim = ssd_conv_w.shape[-1]
    d_k = gla_gate_up.shape[-1]
    d_v = gla_w_out.shape[1]
    dk, dv = d_k // GLA_N_HEADS, d_v // GLA_N_HEADS
    sizes = (d_inner, conv_dim, n_heads, d_k, d_k, d_v, d_v, 2 * GLA_GATE_RANK, 2 * d)
    off = np.concatenate([[0], np.cumsum(sizes)])
    assert off[-1] == w_in.shape[-1]
    wide = [0, 1, 3, 4, 5, 6, 8]
    col, pos = {}, 0
    for s in wide:
        col[s] = pos
        pos += sizes[s]

    zero_front = jnp.zeros((FRONT_PAD, d), F32)
    pieces = []
    for x in (x_prompt, x_sample):
        for b in range(x.shape[0]):
            pieces += [zero_front, meta_tokens.astype(F32), x[b]]
    if rows > layout["used_rows"]:
        pieces.append(jnp.zeros((rows - layout["used_rows"], d), F32))
    h = jnp.concatenate(pieces, axis=0)

    for i in range(depth):
        w = w_in[i]
        w_big = jnp.concatenate([w[:, off[s]:off[s + 1]] for s in wide], axis=1).astype(BF16)
        w_small = jnp.concatenate(
            [w[:, off[2]:off[3]], w[:, off[7]:off[8]],
             jnp.zeros((d, LANES - n_heads - 2 * GLA_GATE_RANK), F32)], axis=1).astype(BF16)

        big, small = _inproj(h, mix_norm_w[i], w_big, w_small, ssd_conv_w[i], ssd_conv_b[i], col[1])
        y_fwd, y_bwd = _ssd_scan(big, col[1], small, ssd_dt_bias[i], ssd_a_log[i], layout)
        o_fwd, o_bwd = (_gla_scan(big, small, gla_gate_up[i, r], gla_gate_b[i, r], layout,
                                  col[3], col[4], col[5], dk, dv, reverse=bool(r)) for r in (0, 1))
        merged = _merge(y_fwd, y_bwd, big, o_fwd, o_bwd, ssd_d[i], ssd_norm_w[i], gla_norm_w[i],
                        ssd_w_out[i].astype(BF16), gla_w_out[i].astype(BF16), col[1], col[6], col[8])
        h = _mm_res([merged], [w_mix_out[i].astype(BF16)], h, n_split=1)
        act = _ffn_up(h, ffn_norm_w[i], ffn_w_up[i].astype(BF16), ffn_conv_w[i], ffn_conv_b[i],
                      layout)
        h = _mm_res([act], [ffn_w_down[i].astype(BF16)], h, n_split=2)

    outs, si = [], 0
    for batch, length in groups:
        outs.append(_final_norm(h, final_norm_w, layout["seqs"][si][0], batch, length))
        si += batch
    return tuple(outs)
```

```python
import functools

import numpy as np
import jax
import jax.numpy as jnp
from jax import lax
from jax.experimental import pallas as pl
from jax.experimental.pallas import tpu as pltpu

F32 = jnp.float32
BF16 = jnp.bfloat16

N_META = 16
EPS = 1e-6
SSD_HEAD_DIM = 64
SSD_N_GROUPS = 8
SSD_HEADS_PER_GROUP = 4
SSD_D_STATE = 128
GLA_N_HEADS = 4
GLA_GATE_RANK = 16
GLA_GATE_TAU = 16.0

LANES = 128
ROW_ALIGN = 256
FRONT_PAD = ROW_ALIGN - N_META
SSD_CHUNK = 128
GLA_CHUNK = 256
ROW_TILE = 1024
PROLOGUE_ROWS = 128
FFN_COL_CHUNK = 256
ROW_PIECES = 3
HALO = 16
VMEM_LIMIT = 56 * 1024 * 1024
NEG_INF = float("-inf")


def _cparams(*sem):
    return pltpu.CompilerParams(dimension_semantics=sem, vmem_limit_bytes=VMEM_LIMIT)


def _softplus(x):
    return jnp.maximum(x, 0.0) + jnp.log(1.0 + jnp.exp(-jnp.abs(x)))


def _sigmoid(x):
    return 0.5 + 0.5 * jnp.tanh(0.5 * x)


def _silu(x):
    hx = 0.5 * x
    return hx + hx * jnp.tanh(hx)


def _split3(x):
    x1 = x.astype(BF16)
    r1 = x - x1.astype(F32)
    x2 = r1.astype(BF16)
    x3 = (r1 - x2.astype(F32)).astype(BF16)
    return x1, x2, x3


def _dot(a, b):
    return jnp.dot(a, b, preferred_element_type=F32)


def _dot_row_pieces(x_ref, w_ref, cs, pieces=ROW_PIECES):
    n = x_ref.shape[0]
    step = n // pieces
    return jnp.concatenate(
        [_dot(x_ref[r0:r0 + step, :], w_ref[:, cs]) for r0 in range(0, n, step)], axis=0)


def _dot_nt(a, b):
    return lax.dot_general(a, b, (((1,), (1,)), ((), ())), preferred_element_type=F32)


def _dot_tn(a, b):
    return lax.dot_general(a, b, (((0,), (0,)), ((), ())), preferred_element_type=F32)


def _dot_exact_lhs(t, x):
    x1, x2, x3 = _split3(x)
    return _dot(t, x1) + _dot(t, x2) + _dot(t, x3)


def _dot_split2_lhs(t, x):
    x1 = x.astype(BF16)
    x2 = (x - x1.astype(F32)).astype(BF16)
    return _dot(t, x1) + _dot(t, x2)


def _dot_exact_rhs(x, t):
    x1, x2, x3 = _split3(x)
    return _dot(x1, t) + _dot(x2, t) + _dot(x3, t)


def _row_valid(row0, n, ranges):
    r = row0 + lax.broadcasted_iota(jnp.int32, (n, 1), 0)
    ok = None
    for lo, hi in ranges:
        m = (r >= lo) & (r < hi)
        ok = m if ok is None else (ok | m)
    return ok


def _is_any(c, values):
    ok = None
    for v in values:
        m = c == v
        ok = m if ok is None else (ok | m)
    return ok


def _inproj_kernel(h_ref, prev_ref, next_ref, nw_ref, wb_ref, ws_ref, cw_ref, cb_ref, big_ref,
                   small_ref, xn_scr, *, conv_blocks):
    tm = h_ref.shape[0]
    n = tm + 2 * HALO
    j = pl.program_id(1)

    def norm(x):
        ms = jnp.mean(x * x, axis=-1, keepdims=True)
        return (x * lax.rsqrt(ms + EPS) * nw_ref[...]).astype(BF16)

    @pl.when(j == 0)
    def _():
        xn_scr[0:HALO, :] = norm(prev_ref[...])
        for r0 in range(0, tm, PROLOGUE_ROWS):
            xn_scr[HALO + r0:HALO + r0 + PROLOGUE_ROWS, :] = norm(h_ref[r0:r0 + PROLOGUE_ROWS, :])
        xn_scr[HALO + tm:2 * HALO + tm, :] = norm(next_ref[...])
        small_ref[...] = _dot(xn_scr[HALO:HALO + tm, :], ws_ref[...])

    is_conv = (j >= conv_blocks[0]) & (j < conv_blocks[1])

    @pl.when(is_conv)
    def _():
        for c0 in range(0, big_ref.shape[1], FFN_COL_CHUNK):
            cs = slice(c0, c0 + FFN_COL_CHUNK)
            u = _dot_row_pieces(xn_scr, wb_ref, cs)
            y = cb_ref[:, cs] + pltpu.roll(u, 1, 0)[HALO:HALO + tm] * cw_ref[0:1, cs]
            y = y + u[HALO:HALO + tm] * cw_ref[1:2, cs]
            y = y + pltpu.roll(u, n - 1, 0)[HALO:HALO + tm] * cw_ref[2:3, cs]
            big_ref[:, cs] = _silu(y).astype(BF16)

    @pl.when(jnp.logical_not(is_conv))
    def _():
        big_ref[...] = _dot(xn_scr[HALO:HALO + tm, :], wb_ref[...]).astype(BF16)


def _inproj(h, norm_w, w_big, w_small, conv_w, conv_b, conv_col0, tn=2048):
    rows, d = h.shape
    nb = w_big.shape[1]
    tm = ROW_TILE
    hb = tm // HALO
    last = rows // HALO - 1
    c = conv_w.shape[1]
    conv_blocks = (conv_col0 // tn, (conv_col0 + c) // tn)
    n_conv = conv_blocks[1] - conv_blocks[0]

    def conv_map(i, j):
        return (0, jnp.clip(j - conv_blocks[0], 0, n_conv - 1))

    kern = functools.partial(_inproj_kernel, conv_blocks=conv_blocks)
    return pl.pallas_call(
        kern,
        grid=(rows // tm, nb // tn),
        in_specs=[
            pl.BlockSpec((tm, d), lambda i, j: (i, 0)),
            pl.BlockSpec((HALO, d), lambda i, j: (jnp.maximum(i * hb - 1, 0), 0)),
            pl.BlockSpec((HALO, d), lambda i, j: (jnp.minimum((i + 1) * hb, last), 0)),
            pl.BlockSpec((1, d), lambda i, j: (0, 0)),
            pl.BlockSpec((d, tn), lambda i, j: (0, j)),
            pl.BlockSpec((d, LANES), lambda i, j: (0, 0)),
            pl.BlockSpec((3, tn), conv_map),
            pl.BlockSpec((1, tn), conv_map),
        ],
        out_specs=[
            pl.BlockSpec((tm, tn), lambda i, j: (i, j)),
            pl.BlockSpec((tm, LANES), lambda i, j: (i, 0)),
        ],
        out_shape=[
            jax.ShapeDtypeStruct((rows, nb), BF16),
            jax.ShapeDtypeStruct((rows, LANES), F32),
        ],
        scratch_shapes=[pltpu.VMEM((tm + 2 * HALO, d), BF16)],
        compiler_params=_cparams("parallel", "arbitrary"),
        name="inproj",
    )(h, h, h, norm_w.reshape(1, d), w_big, w_small, conv_w, conv_b.reshape(1, c))


def _ssd_chunk(d, c, x_ref, b_ref, c_ref, small_ref, dtb_ref, alog_ref, tri_ref, mask_ref, exp_ref,
               y_ref, state_scr, valid):
    ch = SSD_CHUNK
    ok = _row_valid(c * ch, ch, valid)
    dt = _softplus(small_ref[...] + dtb_ref[d])
    dt = jnp.where(ok, dt, 0.0)
    a = -jnp.exp(alog_ref[d]) * dt
    acs = _dot_exact_lhs(tri_ref[d], a)
    acs_t = _dot_exact_rhs(a.T, tri_ref[1 - d])
    dt_t = dt.T
    total = jnp.sum(a, axis=0, keepdims=True)
    w_in = jnp.exp(total - acs) * dt
    per_head = jnp.concatenate(
        [w_in, jnp.exp(acs), jnp.broadcast_to(jnp.exp(total), (8, LANES))], axis=0)
    hi = per_head.astype(BF16)
    lo = (per_head - hi.astype(F32)).astype(BF16)
    per_lane = _dot(jnp.concatenate([hi, lo], axis=1), exp_ref[...])
    w_exp = per_lane[0:ch]
    ea_exp = per_lane[ch:2 * ch]
    et_exp = per_lane[2 * ch:2 * ch + 1]
    mask = mask_ref[d]

    gw = SSD_HEADS_PER_GROUP * SSD_HEAD_DIM
    lane = lax.broadcasted_iota(jnp.int32, (1, gw), 1)
    for grp in range(SSD_N_GROUPS):
        bg = b_ref[:, grp * SSD_D_STATE:(grp + 1) * SSD_D_STATE]
        cg = c_ref[:, grp * SSD_D_STATE:(grp + 1) * SSD_D_STATE]
        xg = x_ref[:, grp * gw:(grp + 1) * gw]
        cb = _dot_nt(cg, bg)
        st = state_scr[d, grp]
        y = _dot(cg, st.astype(BF16)) * ea_exp[:, grp * gw:(grp + 1) * gw]
        ms, xs = [], []
        for hh in range(SSD_HEADS_PER_GROUP):
            h = grp * SSD_HEADS_PER_GROUP + hh
            seg = acs[:, h:h + 1] - acs_t[h:h + 1, :] + mask
            ms.append((cb * jnp.exp(seg) * dt_t[h:h + 1, :]).astype(BF16))
            in_head = (lane >= hh * SSD_HEAD_DIM) & (lane < (hh + 1) * SSD_HEAD_DIM)
            xs.append(jnp.where(in_head, xg, jnp.zeros_like(xg)))
        y = y + _dot(jnp.concatenate(ms, axis=1), jnp.concatenate(xs, axis=0))
        y_ref[:, grp * gw:(grp + 1) * gw] = y.astype(BF16)
        xw = (xg.astype(F32) * w_exp[:, grp * gw:(grp + 1) * gw]).astype(BF16)
        state_scr[d, grp] = st * et_exp[:, grp * gw:(grp + 1) * gw] + _dot_tn(bg, xw)


def _ssd_scan_kernel(xf_ref, bf_ref, cf_ref, sf_ref, xb_ref, bb_ref, cb_ref, sb_ref, dtb_ref,
                     alog_ref, tri_ref, mask_ref, exp_ref, yf_ref, yb_ref, state_scr, *, n_chunks,
                     firsts, lasts, valid):
    g = pl.program_id(0)
    r = n_chunks - 1 - g

    @pl.when(_is_any(g, firsts))
    def _():
        state_scr[0] = jnp.zeros(state_scr.shape[1:], F32)

    @pl.when(_is_any(r, lasts))
    def _():
        state_scr[1] = jnp.zeros(state_scr.shape[1:], F32)

    shared = (dtb_ref, alog_ref, tri_ref, mask_ref, exp_ref)
    _ssd_chunk(0, g, xf_ref, bf_ref, cf_ref, sf_ref, *shared, yf_ref, state_scr, valid)
    _ssd_chunk(1, r, xb_ref, bb_ref, cb_ref, sb_ref, *shared, yb_ref, state_scr, valid)


def _ssd_scan(xbc, col_x, small, dt_bias, a_log, layout):
    rows = xbc.shape[0]
    ch = SSD_CHUNK
    d_inner = SSD_N_GROUPS * SSD_HEADS_PER_GROUP * SSD_HEAD_DIM
    gn = SSD_N_GROUPS * SSD_D_STATE
    n_heads = SSD_N_GROUPS * SSD_HEADS_PER_GROUP
    n_chunks = rows // ch
    firsts = tuple(s // ch for s, _ in layout["spans"])
    lasts = tuple((s + n) // ch - 1 for s, n in layout["spans"])

    idx = np.arange(ch)
    lower = (idx[:, None] >= idx[None, :])
    tri = np.stack([lower, lower.T]).astype(np.float32)
    mask = np.where(tri > 0, 0.0, NEG_INF).astype(np.float32)
    expand = np.zeros((2 * LANES, d_inner), np.float32)
    for h in range(n_heads):
        expand[h, h * SSD_HEAD_DIM:(h + 1) * SSD_HEAD_DIM] = 1.0
        expand[LANES + h, h * SSD_HEAD_DIM:(h + 1) * SSD_HEAD_DIM] = 1.0
    pad = ((0, 0), (0, 0), (0, LANES - n_heads))
    dtb = jnp.pad(dt_bias.reshape(2, 1, n_heads), pad)
    alog = jnp.pad(a_log.reshape(2, 1, n_heads), pad)

    def chunk_specs(cmap):
        return [
            pl.BlockSpec((ch, d_inner), lambda g: (cmap(g), col_x // d_inner)),
            pl.BlockSpec((ch, gn), lambda g: (cmap(g), (col_x + d_inner) // gn)),
            pl.BlockSpec((ch, gn), lambda g: (cmap(g), (col_x + d_inner) // gn + 1)),
            pl.BlockSpec((ch, LANES), lambda g: (cmap(g), 0)),
        ]

    def fwd(g):
        return g

    def rev(g):
        return n_chunks - 1 - g

    kern = functools.partial(_ssd_scan_kernel, n_chunks=n_chunks, firsts=firsts, lasts=lasts,
                             valid=layout["valid"])
    return pl.pallas_call(
        kern,
        grid=(n_chunks,),
        in_specs=chunk_specs(fwd) + chunk_specs(rev) + [
            pl.BlockSpec((2, 1, LANES), lambda g: (0, 0, 0)),
            pl.BlockSpec((2, 1, LANES), lambda g: (0, 0, 0)),
            pl.BlockSpec((2, ch, ch), lambda g: (0, 0, 0)),
            pl.BlockSpec((2, ch, ch), lambda g: (0, 0, 0)),
            pl.BlockSpec((2 * LANES, d_inner), lambda g: (0, 0)),
        ],
        out_specs=[pl.BlockSpec((ch, d_inner), lambda g: (fwd(g), 0)),
                   pl.BlockSpec((ch, d_inner), lambda g: (rev(g), 0))],
        out_shape=[jax.ShapeDtypeStruct((rows, d_inner), BF16)] * 2,
        scratch_shapes=[
            pltpu.VMEM((2, SSD_N_GROUPS, SSD_D_STATE, SSD_HEADS_PER_GROUP * SSD_HEAD_DIM), F32)],
        compiler_params=_cparams("arbitrary"),
        name="ssd_scan",
    )(xbc, xbc, xbc, small, xbc, xbc, xbc, small, dtb, alog, jnp.asarray(tri, BF16),
      jnp.asarray(mask), jnp.asarray(expand, BF16))


def _gla_levels():
    lv, m = [], GLA_CHUNK // 2
    while m >= 1:
        lv.append(m)
        m //= 2
    return tuple(lv)


def _gla_level_exponent(cs_scr, cs_t, t, m, reverse):
    ch = cs_t.shape[0]
    off = m - 1 if reverse else m

    def ref_tile(r):
        return jnp.broadcast_to(cs_scr[t, pl.ds(r, 1), :], (8, LANES))

    if m >= 8:
        pieces = []
        for p in range(ch // (2 * m)):
            lo = 2 * m * p
            ref = jnp.concatenate([ref_tile(lo + off)] * (m // 8), axis=0)
            early, late = cs_t[lo:lo + m], cs_t[lo + m:lo + 2 * m]
            pieces += [early - ref, ref - late] if reverse else [ref - early, late - ref]
        return jnp.concatenate(pieces, axis=0)

    row = lax.broadcasted_iota(jnp.int32, (ch, 1), 0)
    in_late = ((row // m) & 1) == 1
    is_target = in_late != reverse
    if m == 1:
        x3 = cs_t.reshape(ch // 8, 8, LANES)
        nb = pltpu.roll(x3, 1 if reverse else 7, 1).reshape(ch, LANES)
        return jnp.where(is_target, 0.0, nb - cs_t)
    sub = lax.broadcasted_iota(jnp.int32, (8, 1), 0)
    pieces = []
    for p in range(ch // 8):
        ref = ref_tile(8 * p + off)
        for q in range(1, 8 // (2 * m)):
            ref = jnp.where(sub >= 2 * m * q, ref_tile(8 * p + 2 * m * q + off), ref)
        pieces.append(ref)
    delta = cs_t - jnp.concatenate(pieces, axis=0)
    return jnp.where(is_target, delta, -delta)


def _gla_level_operand(q_scr, k_ref, sl, m, reverse):
    ch = q_scr.shape[0]
    if m >= 16:
        pieces = []
        for p in range(ch // (2 * m)):
            lo = 2 * m * p
            early, late = slice(lo, lo + m), slice(lo + m, lo + 2 * m)
            if reverse:
                pieces += [q_scr[early, sl], k_ref[late, sl]]
            else:
                pieces += [k_ref[early, sl], q_scr[late, sl]]
        return jnp.concatenate(pieces, axis=0)
    row = lax.broadcasted_iota(jnp.int32, (ch, 1), 0)
    is_target = (((row // m) & 1) == 1) != reverse
    return jnp.where(is_target, q_scr[:, sl], k_ref[:, sl])


def _gla_scan_kernel(q_ref, k_ref, v_ref, small_ref, gu_ref, gb_ref, tri_ref, lev_ref, o_ref,
                     state_scr, cs_scr, att_scr, diag_scr, q_scr, *, reverse, n_chunks, firsts, lasts,
                     valid, dk, dv):
    g = pl.program_id(0)
    c = n_chunks - 1 - g if reverse else g
    ch = GLA_CHUNK
    levels = _gla_levels()
    tiles_per_head = dk // LANES

    @pl.when(_is_any(c, lasts if reverse else firsts))
    def _():
        state_scr[...] = jnp.zeros_like(state_scr)

    ok = _row_valid(c * ch, ch, valid)
    u = _dot(small_ref[...].astype(BF16), gu_ref[...]) + gb_ref[...]
    gate = jnp.where(ok, -_softplus(-u) / GLA_GATE_TAU, 0.0)
    cs = _dot_split2_lhs(tri_ref[...], gate)
    n_tiles = cs.shape[1] // LANES
    for t in range(n_tiles):
        cs_scr[t] = cs[:, t * LANES:(t + 1) * LANES]
    total = jnp.sum(gate, axis=0, keepdims=True)
    lev = lev_ref[...]
    q_scr[...] = q_ref[...] * jnp.asarray(dk ** -0.5, BF16)

    half = ch // 2
    tgt, src = (slice(0, half), slice(half, ch)) if reverse else (slice(half, ch), slice(0, half))
    quads = (slice(0, half), slice(half, ch))

    @pl.when(g == 0)
    def _():
        att_scr[:, src, tgt] = jnp.zeros((GLA_N_HEADS, half, half), BF16)

    for i, m in enumerate(levels):
        e = [jnp.exp(_gla_level_exponent(cs_scr, cs[:, t * LANES:(t + 1) * LANES], t, m, reverse)
                     ).astype(BF16) for t in range(n_tiles)]
        for h in range(GLA_N_HEADS):
            sl = slice(h * dk, (h + 1) * dk)
            e_h = jnp.concatenate(e[h * tiles_per_head:(h + 1) * tiles_per_head], axis=1)
            x = _gla_level_operand(q_scr, k_ref, sl, m, reverse) * e_h
            if i == 0:
                att_scr[h, tgt, src] = _dot_nt(x[tgt], x[src]).astype(BF16)
                continue
            for qi, qs in enumerate(quads):
                p = _dot_nt(x[qs], x[qs])
                diag_scr[h, qi] = jnp.where(lev == i, p, 0.0 if i == 1 else diag_scr[h, qi])

    for h in range(GLA_N_HEADS):
        sl = slice(h * dk, (h + 1) * dk)
        vs = slice(h * dv, (h + 1) * dv)
        vh = v_ref[:, vs]
        for qi, qs in enumerate(quads):
            p = _dot_nt(q_scr[qs, sl], k_ref[qs, sl])
            att_scr[h, qs, qs] = jnp.where(lev == len(levels), p, diag_scr[h, qi]).astype(BF16)
        st = state_scr[h]
        qd = q_scr[:, sl] * jnp.exp(cs[:, sl]).astype(BF16)
        o = _dot(att_scr[h], vh) + _dot_nt(qd, st.astype(BF16))
        o_ref[:, vs] = o.astype(BF16)
        kd = k_ref[:, sl] * jnp.exp(total[:, sl] - cs[:, sl]).astype(BF16)
        state_scr[h] = st * jnp.exp(total[:, sl]) + _dot_tn(vh, kd)


def _gla_scan(big, small, gate_up, gate_b, layout, col_q, col_k, col_v, dk, dv, reverse):
    rows = big.shape[0]
    ch = GLA_CHUNK
    n_chunks = rows // ch
    firsts = tuple(s // ch for s, _ in layout["spans"])
    lasts = tuple((s + n) // ch - 1 for s, n in layout["spans"])
    hk = GLA_N_HEADS * dk
    hv = GLA_N_HEADS * dv
    levels = _gla_levels()

    idx = np.arange(ch)
    l_, s_ = idx[:, None], idx[None, :]
    seen = (l_ <= s_) if reverse else (l_ >= s_)
    x = l_ ^ s_
    top = np.floor(np.log2(np.maximum(x, 1))).astype(np.int64)
    lvl = np.where(x == 0, len(levels), (len(levels) - 1) - top)
    lev = np.where(seen, lvl, -1).astype(np.int32)[:ch // 2, :ch // 2]

    lo = SSD_N_GROUPS * SSD_HEADS_PER_GROUP + GLA_GATE_RANK * int(reverse)
    gu = jnp.zeros((LANES, hk), F32).at[lo:lo + GLA_GATE_RANK].set(gate_up).astype(BF16)

    def cmap(g):
        return n_chunks - 1 - g if reverse else g

    kern = functools.partial(_gla_scan_kernel, reverse=reverse, n_chunks=n_chunks, firsts=firsts,
                             lasts=lasts, valid=layout["valid"], dk=dk, dv=dv)
    return pl.pallas_call(
        kern,
        grid=(n_chunks,),
        in_specs=[
            pl.BlockSpec((ch, hk), lambda g: (cmap(g), col_q // hk)),
            pl.BlockSpec((ch, hk), lambda g: (cmap(g), col_k // hk)),
            pl.BlockSpec((ch, hv), lambda g: (cmap(g), col_v // hv)),
            pl.BlockSpec((ch, LANES), lambda g: (cmap(g), 0)),
            pl.BlockSpec((LANES, hk), lambda g: (0, 0)),
            pl.BlockSpec((1, hk), lambda g: (0, 0)),
            pl.BlockSpec((ch, ch), lambda g: (0, 0)),
            pl.BlockSpec((ch // 2, ch // 2), lambda g: (0, 0)),
        ],
        out_specs=pl.BlockSpec((ch, hv), lambda g: (cmap(g), 0)),
        out_shape=jax.ShapeDtypeStruct((rows, hv), BF16),
        scratch_shapes=[
            pltpu.VMEM((GLA_N_HEADS, dv, dk), F32),
            pltpu.VMEM((hk // LANES, ch, LANES), F32),
            pltpu.VMEM((GLA_N_HEADS, ch, ch), BF16),
            pltpu.VMEM((GLA_N_HEADS, 2, ch // 2, ch // 2), F32),
            pltpu.VMEM((ch, hk), BF16),
        ],
        compiler_params=_cparams("arbitrary"),
        name="gla_scan_bwd" if reverse else "gla_scan_fwd",
    )(big, big, big, small, gu, gate_b.reshape(1, hk), jnp.asarray(seen.astype(np.float32), BF16),
      jnp.asarray(lev))


def _merge_kernel(ysf_ref, ysb_ref, xs_ref, z_ref, ogf_ref, ogb_ref, ogate_ref, ms_ref, mg_ref, dskip_ref,
                  snw_ref, gnw_ref, ws_ref, wg_ref, o_ref, a_ssd, a_gla, *, dv):
    s = pl.program_id(0)
    tm = a_ssd.shape[1]

    @pl.when(s == 0)
    def _():
        a_ssd[1] = jnp.zeros(a_ssd.shape[1:], BF16)
        a_gla[1] = jnp.zeros(a_gla.shape[1:], BF16)

    slot = s % 2
    for r0 in range(0, tm, PROLOGUE_ROWS):
        rs = slice(r0, r0 + PROLOGUE_ROWS)
        y = ysf_ref[rs, :].astype(F32) + ysb_ref[rs, :].astype(F32)
        y = y + xs_ref[rs, :].astype(F32) * dskip_ref[...]
        y = y * _silu(z_ref[rs, :].astype(F32))
        ms = jnp.mean(y * y, axis=-1, keepdims=True)
        a_ssd[slot, rs, :] = (y * lax.rsqrt(ms + EPS) * snw_ref[...]).astype(BF16)
        for h in range(GLA_N_HEADS):
            sl = slice(h * dv, (h + 1) * dv)
            o = ogf_ref[rs, sl].astype(F32) + ogb_ref[rs, sl].astype(F32)
            ms = jnp.mean(o * o, axis=-1, keepdims=True)
            o = o * lax.rsqrt(ms + EPS) * gnw_ref[...]
            a_gla[slot, rs, sl] = (o * _silu(ogate_ref[rs, sl].astype(F32))).astype(BF16)

    y_ssd = _dot(a_ssd[1 - slot], ws_ref[...])
    y_gla = _dot(a_gla[1 - slot], wg_ref[...])
    merged = _sigmoid(ms_ref[...].astype(F32)) * y_ssd
    merged = merged + _sigmoid(mg_ref[...].astype(F32)) * y_gla
    o_ref[...] = merged.astype(BF16)


def _merge(y_fwd, y_bwd, big, o_fwd, o_bwd, d_skip, ssd_norm_w, gla_norm_w, w_ssd, w_gla, col_x, col_og,
           col_merge, tm=256):
    rows, d = big.shape[0], w_ssd.shape[1]
    di = w_ssd.shape[0]
    dvt = w_gla.shape[0]
    dv = dvt // GLA_N_HEADS
    n_tiles = rows // tm
    kern = functools.partial(_merge_kernel, dv=dv)

    def cur(s):
        return jnp.minimum(s, n_tiles - 1)

    def prev(s):
        return jnp.maximum(s - 1, 0)

    resident = pl.Buffered(1)
    return pl.pallas_call(
        kern,
        grid=(n_tiles + 1,),
        in_specs=[
            pl.BlockSpec((tm, di), lambda s: (cur(s), 0)),
            pl.BlockSpec((tm, di), lambda s: (cur(s), 0)),
            pl.BlockSpec((tm, di), lambda s: (cur(s), col_x // di)),
            pl.BlockSpec((tm, di), lambda s: (cur(s), 0)),
            pl.BlockSpec((tm, dvt), lambda s: (cur(s), 0)),
            pl.BlockSpec((tm, dvt), lambda s: (cur(s), 0)),
            pl.BlockSpec((tm, dvt), lambda s: (cur(s), col_og // dvt)),
            pl.BlockSpec((tm, d), lambda s: (prev(s), col_merge // d)),
            pl.BlockSpec((tm, d), lambda s: (prev(s), col_merge // d + 1)),
            pl.BlockSpec((1, di), lambda s: (0, 0)),
            pl.BlockSpec((1, di), lambda s: (0, 0)),
            pl.BlockSpec((1, dv), lambda s: (0, 0)),
            pl.BlockSpec((di, d), lambda s: (0, 0), pipeline_mode=resident),
            pl.BlockSpec((dvt, d), lambda s: (0, 0), pipeline_mode=resident),
        ],
        out_specs=pl.BlockSpec((tm, d), lambda s: (prev(s), 0)),
        out_shape=jax.ShapeDtypeStruct((rows, d), BF16),
        scratch_shapes=[pltpu.VMEM((2, tm, di), BF16), pltpu.VMEM((2, tm, dvt), BF16)],
        compiler_params=_cparams("arbitrary"),
        name="merge",
    )(y_fwd, y_bwd, big, big, o_fwd, o_bwd, big, big, big,
      jnp.repeat(d_skip, SSD_HEAD_DIM).reshape(1, di), ssd_norm_w.reshape(1, di),
      gla_norm_w.reshape(1, dv), w_ssd, w_gla)


def _mm_res_kernel(*refs):
    *aw_refs, h_ref, o_ref = refs
    n_terms = len(aw_refs) // 2
    acc = h_ref[...]
    for a_ref, w_ref in zip(aw_refs[:n_terms], aw_refs[n_terms:]):
        acc = acc + _dot(a_ref[...], w_ref[...])
    o_ref[...] = acc


def _mm_res(a_list, w_list, h, n_split, tm=512):
    rows = h.shape[0]
    n = w_list[0].shape[1]
    tn = n // n_split
    a_specs = [pl.BlockSpec((tm, a.shape[1]), lambda j, i: (i, 0)) for a in a_list]
    w_specs = [pl.BlockSpec((w.shape[0], tn), lambda j, i: (0, j)) for w in w_list]
    return pl.pallas_call(
        _mm_res_kernel,
        grid=(n_split, rows // tm),
        in_specs=a_specs + w_specs + [pl.BlockSpec((tm, tn), lambda j, i: (i, j))],
        out_specs=pl.BlockSpec((tm, tn), lambda j, i: (i, j)),
        out_shape=jax.ShapeDtypeStruct((rows, n), F32),
        compiler_params=_cparams("arbitrary", "arbitrary"),
        name="mm_res",
    )(*a_list, *w_list, h)


def _ffn_up_kernel(h_ref, prev_ref, next_ref, nw_ref, wg_ref, wu_ref, cwg_ref, cwu_ref, cbg_ref,
                   cbu_ref, o_ref, xn_scr, *, valid):
    tm = h_ref.shape[0]

    def norm(x):
        ms = jnp.mean(x * x, axis=-1, keepdims=True)
        return (x * lax.rsqrt(ms + EPS) * nw_ref[...]).astype(BF16)

    @pl.when(pl.program_id(1) == 0)
    def _():
        xn_scr[0:HALO, :] = norm(prev_ref[...])
        for r0 in range(0, tm, PROLOGUE_ROWS):
            xn_scr[HALO + r0:HALO + r0 + PROLOGUE_ROWS, :] = norm(h_ref[r0:r0 + PROLOGUE_ROWS, :])
        xn_scr[HALO + tm:2 * HALO + tm, :] = norm(next_ref[...])

    n = tm + 2 * HALO

    def conv(u, w_ref, b_ref, cs):
        y = b_ref[:, cs] + pltpu.roll(u, 1, 0)[HALO:HALO + tm] * w_ref[0:1, cs]
        y = y + u[HALO:HALO + tm] * w_ref[1:2, cs]
        return y + pltpu.roll(u, n - 1, 0)[HALO:HALO + tm] * w_ref[2:3, cs]

    ok = _row_valid(pl.program_id(0) * tm, tm, valid)
    for c0 in range(0, o_ref.shape[1], FFN_COL_CHUNK):
        cs = slice(c0, c0 + FFN_COL_CHUNK)
        gate = conv(_dot_row_pieces(xn_scr, wg_ref, cs), cwg_ref, cbg_ref, cs)
        up = conv(_dot_row_pieces(xn_scr, wu_ref, cs), cwu_ref, cbu_ref, cs)
        o_ref[:, cs] = jnp.where(ok, _silu(gate) * up, 0.0).astype(BF16)


def _ffn_up(h, norm_w, w_up, conv_w, conv_b, layout, tf=512):
    rows, d = h.shape
    dff = w_up.shape[1] // 2
    tm = ROW_TILE
    hb = tm // HALO
    last = rows // HALO - 1
    nj = dff // tf
    kern = functools.partial(_ffn_up_kernel, valid=layout["valid"])
    return pl.pallas_call(
        kern,
        grid=(rows // tm, nj),
        in_specs=[
            pl.BlockSpec((tm, d), lambda i, j: (i, 0)),
            pl.BlockSpec((HALO, d), lambda i, j: (jnp.maximum(i * hb - 1, 0), 0)),
            pl.BlockSpec((HALO, d), lambda i, j: (jnp.minimum((i + 1) * hb, last), 0)),
            pl.BlockSpec((1, d), lambda i, j: (0, 0)),
            pl.BlockSpec((d, tf), lambda i, j: (0, j)),
            pl.BlockSpec((d, tf), lambda i, j: (0, nj + j)),
            pl.BlockSpec((3, tf), lambda i, j: (0, j)),
            pl.BlockSpec((3, tf), lambda i, j: (0, nj + j)),
            pl.BlockSpec((1, tf), lambda i, j: (0, j)),
            pl.BlockSpec((1, tf), lambda i, j: (0, nj + j)),
        ],
        out_specs=pl.BlockSpec((tm, tf), lambda i, j: (i, j)),
        out_shape=jax.ShapeDtypeStruct((rows, dff), BF16),
        scratch_shapes=[pltpu.VMEM((tm + 2 * HALO, d), BF16)],
        compiler_params=_cparams("parallel", "arbitrary"),
        name="ffn_up",
    )(h, h, h, norm_w.reshape(1, d), w_up, w_up, conv_w, conv_w,
      conv_b.reshape(1, 2 * dff), conv_b.reshape(1, 2 * dff))


def _final_norm_kernel(h_ref, nw_ref, o_ref):
    x = h_ref[...]
    ms = jnp.mean(x * x, axis=-1, keepdims=True)
    o_ref[0] = x * lax.rsqrt(ms + EPS) * nw_ref[...]


def _final_norm(h, norm_w, first_row, batch, length, tr=ROW_ALIGN):
    d = h.shape[1]
    per_seq = (ROW_ALIGN + length) // tr
    b0 = (first_row + ROW_ALIGN) // tr
    return pl.pallas_call(
        _final_norm_kernel,
        grid=(batch, length // tr),
        in_specs=[
            pl.BlockSpec((tr, d), lambda b, t: (b0 + b * per_seq + t, 0)),
            pl.BlockSpec((1, d), lambda b, t: (0, 0)),
        ],
        out_specs=pl.BlockSpec((1, tr, d), lambda b, t: (b, t, 0)),
        out_shape=jax.ShapeDtypeStruct((batch, length, d), F32),
        compiler_params=_cparams("parallel", "parallel"),
        name="final_norm",
    )(h, norm_w.reshape(1, d))


def _make_layout(groups):
    seqs, valid, row = [], [], 0
    for batch, length in groups:
        assert length % ROW_ALIGN == 0
        for _ in range(batch):
            n = ROW_ALIGN + length
            seqs.append((row, n))
            valid.append((row + FRONT_PAD, row + n))
            row += n
    rows = (row // ROW_TILE + 1) * ROW_TILE
    spans = tuple(seqs) + ((row, rows - row),)
    return dict(seqs=tuple(seqs), spans=spans, valid=tuple(valid), used_rows=row, rows=rows)


def kernel(x_prompt, x_sample, meta_tokens, mix_norm_w, w_in, ssd_conv_w, ssd_conv_b, ssd_dt_bias,
           ssd_a_log, ssd_d, ssd_norm_w, ssd_w_out, gla_gate_up, gla_gate_b, gla_norm_w, gla_w_out,
           w_mix_out, ffn_norm_w, ffn_w_up, ffn_conv_w, ffn_conv_b, ffn_w_down, final_norm_w):
    depth, d = mix_norm_w.shape
    groups = [(x_prompt.shape[0], x_prompt.shape[1]), (x_sample.shape[0], x_sample.shape[1])]
    layout = _make_layout(groups)
    rows = layout["rows"]

    n_heads = ssd_dt_bias.shape[-1]
    d_inner = n_heads * SSD_HEAD_DIM
    conv_dim = ssd_conv_w.shape[-1]
    d_k = gla_gate_up.shape[-1]
    d_v = gla_w_out.shape[1]
    dk, dv = d_k // GLA_N_HEADS, d_v // GLA_N_HEADS
    sizes = (d_inner, conv_dim, n_heads, d_k, d_k, d_v, d_v, 2 * GLA_GATE_RANK, 2 * d)
    off = np.concatenate([[0], np.cumsum(sizes)])
    assert off[-1] == w_in.shape[-1]
    wide = [0, 1, 3, 4, 5, 6, 8]
    col, pos = {}, 0
    for s in wide:
        col[s] = pos
        pos += sizes[s]

    zero_front = jnp.zeros((FRONT_PAD, d), F32)
    pieces = []
    for x in (x_prompt, x_sample):
        for b in range(x.shape[0]):
            pieces += [zero_front, meta_tokens.astype(F32), x[b]]
    if rows > layout["used_rows"]:
        pieces.append(jnp.zeros((rows - layout["used_rows"], d), F32))
    h = jnp.concatenate(pieces, axis=0)

    for i in range(depth):
        w = w_in[i]
        w_big = jnp.concatenate([w[:, off[s]:off[s + 1]] for s in wide], axis=1).astype(BF16)
        w_small = jnp.concatenate(
            [w[:, off[2]:off[3]], w[:, off[7]:off[8]],
             jnp.zeros((d, LANES - n_heads - 2 * GLA_GATE_RANK), F32)], axis=1).astype(BF16)

        big, small = _inproj(h, mix_norm_w[i], w_big, w_small, ssd_conv_w[i], ssd_conv_b[i], col[1])
        y_fwd, y_bwd = _ssd_scan(big, col[1], small, ssd_dt_bias[i], ssd_a_log[i], layout)
        o_fwd, o_bwd = (_gla_scan(big, small, gla_gate_up[i, r], gla_gate_b[i, r], layout,
                                  col[3], col[4], col[5], dk, dv, reverse=bool(r)) for r in (0, 1))
        merged = _merge(y_fwd, y_bwd, big, o_fwd, o_bwd, ssd_d[i], ssd_norm_w[i], gla_norm_w[i],
                        ssd_w_out[i].astype(BF16), gla_w_out[i].astype(BF16), col[1], col[6], col[8])
        h = _mm_res([merged], [w_mix_out[i].astype(BF16)], h, n_split=1)
        act = _ffn_up(h, ffn_norm_w[i], ffn_w_up[i].astype(BF16), ffn_conv_w[i], ffn_conv_b[i],
                      layout)
        h = _mm_res([act], [ffn_w_down[i].astype(BF16)], h, n_split=2)

    outs, si = [], 0
    for batch, length in groups:
        outs.append(_final_norm(h, final_norm_w, layout["seqs"][si][0], batch, length))
        si += batch
    return tuple(outs)
```

```python
import functools

import numpy as np
import jax
import jax.numpy as jnp
from jax import lax
from jax.experimental import pallas as pl
from jax.experimental.pallas import tpu as pltpu

F32 = jnp.float32
BF16 = jnp.bfloat16

N_META = 16
EPS = 1e-6
SSD_HEAD_DIM = 64
SSD_N_GROUPS = 8
SSD_HEADS_PER_GROUP = 4
SSD_D_STATE = 128
GLA_N_HEADS = 4
GLA_GATE_RANK = 16
GLA_GATE_TAU = 16.0

LANES = 128
ROW_ALIGN = 256
FRONT_PAD = ROW_ALIGN - N_META
SSD_CHUNK = 128
GLA_CHUNK = 256
ROW_TILE = 1024
PROLOGUE_ROWS = 128
FFN_COL_CHUNK = 256
ROW_PIECES = 3
HALO = 16
VMEM_LIMIT = 56 * 1024 * 1024
NEG_INF = float("-inf")


def _cparams(*sem):
    return pltpu.CompilerParams(dimension_semantics=sem, vmem_limit_bytes=VMEM_LIMIT)


def _softplus(x):
    return jnp.maximum(x, 0.0) + jnp.log(1.0 + jnp.exp(-jnp.abs(x)))


def _sigmoid(x):
    return 0.5 + 0.5 * jnp.tanh(0.5 * x)


def _silu(x):
    hx = 0.5 * x
    return hx + hx * jnp.tanh(hx)


def _split3(x):
    x1 = x.astype(BF16)
    r1 = x - x1.astype(F32)
    x2 = r1.astype(BF16)
    x3 = (r1 - x2.astype(F32)).astype(BF16)
    return x1, x2, x3


def _dot(a, b):
    return jnp.dot(a, b, preferred_element_type=F32)


def _dot_row_pieces(x_ref, w_ref, cs, pieces=ROW_PIECES):
    n = x_ref.shape[0]
    step = n // pieces
    return jnp.concatenate(
        [_dot(x_ref[r0:r0 + step, :], w_ref[:, cs]) for r0 in range(0, n, step)], axis=0)


def _dot_nt(a, b):
    return lax.dot_general(a, b, (((1,), (1,)), ((), ())), preferred_element_type=F32)


def _dot_tn(a, b):
    return lax.dot_general(a, b, (((0,), (0,)), ((), ())), preferred_element_type=F32)


def _dot_exact_lhs(t, x):
    x1, x2, x3 = _split3(x)
    return _dot(t, x1) + _dot(t, x2) + _dot(t, x3)


def _dot_split2_lhs(t, x):
    x1 = x.astype(BF16)
    x2 = (x - x1.astype(F32)).astype(BF16)
    return _dot(t, x1) + _dot(t, x2)


def _dot_exact_rhs(x, t):
    x1, x2, x3 = _split3(x)
    return _dot(x1, t) + _dot(x2, t) + _dot(x3, t)


def _row_valid(row0, n, ranges):
    r = row0 + lax.broadcasted_iota(jnp.int32, (n, 1), 0)
    ok = None
    for lo, hi in ranges:
        m = (r >= lo) & (r < hi)
        ok = m if ok is None else (ok | m)
    return ok


def _is_any(c, values):
    ok = None
    for v in values:
        m = c == v
        ok = m if ok is None else (ok | m)
    return ok


def _inproj_kernel(h_ref, prev_ref, next_ref, nw_ref, wb_ref, ws_ref, cw_ref, cb_ref, big_ref,
                   small_ref, xn_scr, *, conv_blocks):
    tm = h_ref.shape[0]
    n = tm + 2 * HALO
    j = pl.program_id(1)

    def norm(x):
        ms = jnp.mean(x * x, axis=-1, keepdims=True)
        return (x * lax.rsqrt(ms + EPS) * nw_ref[...]).astype(BF16)

    @pl.when(j == 0)
    def _():
        xn_scr[0:HALO, :] = norm(prev_ref[...])
        for r0 in range(0, tm, PROLOGUE_ROWS):
            xn_scr[HALO + r0:HALO + r0 + PROLOGUE_ROWS, :] = norm(h_ref[r0:r0 + PROLOGUE_ROWS, :])
        xn_scr[HALO + tm:2 * HALO + tm, :] = norm(next_ref[...])
        small_ref[...] = _dot(xn_scr[HALO:HALO + tm, :], ws_ref[...])

    is_conv = (j >= conv_blocks[0]) & (j < conv_blocks[1])

    @pl.when(is_conv)
    def _():
        for c0 in range(0, big_ref.shape[1], FFN_COL_CHUNK):
            cs = slice(c0, c0 + FFN_COL_CHUNK)
            u = _dot_row_pieces(xn_scr, wb_ref, cs)
            y = cb_ref[:, cs] + pltpu.roll(u, 1, 0)[HALO:HALO + tm] * cw_ref[0:1, cs]
            y = y + u[HALO:HALO + tm] * cw_ref[1:2, cs]
            y = y + pltpu.roll(u, n - 1, 0)[HALO:HALO + tm] * cw_ref[2:3, cs]
            big_ref[:, cs] = _silu(y).astype(BF16)

    @pl.when(jnp.logical_not(is_conv))
    def _():
        big_ref[...] = _dot(xn_scr[HALO:HALO + tm, :], wb_ref[...]).astype(BF16)


def _inproj(h, norm_w, w_big, w_small, conv_w, conv_b, conv_col0, tn=2048):
    rows, d = h.shape
    nb = w_big.shape[1]
    tm = ROW_TILE
    hb = tm // HALO
    last = rows // HALO - 1
    c = conv_w.shape[1]
    conv_blocks = (conv_col0 // tn, (conv_col0 + c) // tn)
    n_conv = conv_blocks[1] - conv_blocks[0]

    def conv_map(i, j):
        return (0, jnp.clip(j - conv_blocks[0], 0, n_conv - 1))

    kern = functools.partial(_inproj_kernel, conv_blocks=conv_blocks)
    return pl.pallas_call(
        kern,
        grid=(rows // tm, nb // tn),
        in_specs=[
            pl.BlockSpec((tm, d), lambda i, j: (i, 0)),
            pl.BlockSpec((HALO, d), lambda i, j: (jnp.maximum(i * hb - 1, 0), 0)),
            pl.BlockSpec((HALO, d), lambda i, j: (jnp.minimum((i + 1) * hb, last), 0)),
            pl.BlockSpec((1, d), lambda i, j: (0, 0)),
            pl.BlockSpec((d, tn), lambda i, j: (0, j)),
            pl.BlockSpec((d, LANES), lambda i, j: (0, 0)),
            pl.BlockSpec((3, tn), conv_map),
            pl.BlockSpec((1, tn), conv_map),
        ],
        out_specs=[
            pl.BlockSpec((tm, tn), lambda i, j: (i, j)),
            pl.BlockSpec((tm, LANES), lambda i, j: (i, 0)),
        ],
        out_shape=[
            jax.ShapeDtypeStruct((rows, nb), BF16),
            jax.ShapeDtypeStruct((rows, LANES), F32),
        ],
        scratch_shapes=[pltpu.VMEM((tm + 2 * HALO, d), BF16)],
        compiler_params=_cparams("parallel", "arbitrary"),
        name="inproj",
    )(h, h, h, norm_w.reshape(1, d), w_big, w_small, conv_w, conv_b.reshape(1, c))


def _ssd_chunk(d, c, x_ref, b_ref, c_ref, small_ref, dtb_ref, alog_ref, tri_ref, mask_ref, exp_ref,
               y_ref, state_scr, valid):
    ch = SSD_CHUNK
    ok = _row_valid(c * ch, ch, valid)
    dt = _softplus(small_ref[...] + dtb_ref[d])
    dt = jnp.where(ok, dt, 0.0)
    a = -jnp.exp(alog_ref[d]) * dt
    acs = _dot_exact_lhs(tri_ref[d], a)
    acs_t = _dot_exact_rhs(a.T, tri_ref[1 - d])
    dt_t = dt.T
    total = jnp.sum(a, axis=0, keepdims=True)
    w_in = jnp.exp(total - acs) * dt
    per_head = jnp.concatenate(
        [w_in, jnp.exp(acs), jnp.broadcast_to(jnp.exp(total), (8, LANES))], axis=0)
    hi = per_head.astype(BF16)
    lo = (per_head - hi.astype(F32)).astype(BF16)
    per_lane = _dot(jnp.concatenate([hi, lo], axis=1), exp_ref[...])
    w_exp = per_lane[0:ch]
    ea_exp = per_lane[ch:2 * ch]
    et_exp = per_lane[2 * ch:2 * ch + 1]
    mask = mask_ref[d]

    gw = SSD_HEADS_PER_GROUP * SSD_HEAD_DIM
    lane = lax.broadcasted_iota(jnp.int32, (1, gw), 1)
    for grp in range(SSD_N_GROUPS):
        bg = b_ref[:, grp * SSD_D_STATE:(grp + 1) * SSD_D_STATE]
        cg = c_ref[:, grp * SSD_D_STATE:(grp + 1) * SSD_D_STATE]
        xg = x_ref[:, grp * gw:(grp + 1) * gw]
        cb = _dot_nt(cg, bg)
        st = state_scr[d, grp]
        y = _dot(cg, st.astype(BF16)) * ea_exp[:, grp * gw:(grp + 1) * gw]
        ms, xs = [], []
        for hh in range(SSD_HEADS_PER_GROUP):
            h = grp * SSD_HEADS_PER_GROUP + hh
            seg = acs[:, h:h + 1] - acs_t[h:h + 1, :] + mask
            ms.append((cb * jnp.exp(seg) * dt_t[h:h + 1, :]).astype(BF16))
            in_head = (lane >= hh * SSD_HEAD_DIM) & (lane < (hh + 1) * SSD_HEAD_DIM)
            xs.append(jnp.where(in_head, xg, jnp.zeros_like(xg)))
        y = y + _dot(jnp.concatenate(ms, axis=1), jnp.concatenate(xs, axis=0))
        y_ref[:, grp * gw:(grp + 1) * gw] = y.astype(BF16)
        xw = (xg.astype(F32) * w_exp[:, grp * gw:(grp + 1) * gw]).astype(BF16)
        state_scr[d, grp] = st * et_exp[:, grp * gw:(grp + 1) * gw] + _dot_tn(bg, xw)


def _ssd_scan_kernel(xf_ref, bf_ref, cf_ref, sf_ref, xb_ref, bb_ref, cb_ref, sb_ref, dtb_ref,
                     alog_ref, tri_ref, mask_ref, exp_ref, yf_ref, yb_ref, state_scr, *, n_chunks,
                     firsts, lasts, valid):
    g = pl.program_id(0)
    r = n_chunks - 1 - g

    @pl.when(_is_any(g, firsts))
    def _():
        state_scr[0] = jnp.zeros(state_scr.shape[1:], F32)

    @pl.when(_is_any(r, lasts))
    def _():
        state_scr[1] = jnp.zeros(state_scr.shape[1:], F32)

    shared = (dtb_ref, alog_ref, tri_ref, mask_ref, exp_ref)
    _ssd_chunk(0, g, xf_ref, bf_ref, cf_ref, sf_ref, *shared, yf_ref, state_scr, valid)
    _ssd_chunk(1, r, xb_ref, bb_ref, cb_ref, sb_ref, *shared, yb_ref, state_scr, valid)


def _ssd_scan(xbc, col_x, small, dt_bias, a_log, layout):
    rows = xbc.shape[0]
    ch = SSD_CHUNK
    d_inner = SSD_N_GROUPS * SSD_HEADS_PER_GROUP * SSD_HEAD_DIM
    gn = SSD_N_GROUPS * SSD_D_STATE
    n_heads = SSD_N_GROUPS * SSD_HEADS_PER_GROUP
    n_chunks = rows // ch
    firsts = tuple(s // ch for s, _ in layout["spans"])
    lasts = tuple((s + n) // ch - 1 for s, n in layout["spans"])

    idx = np.arange(ch)
    lower = (idx[:, None] >= idx[None, :])
    tri = np.stack([lower, lower.T]).astype(np.float32)
    mask = np.where(tri > 0, 0.0, NEG_INF).astype(np.float32)
    expand = np.zeros((2 * LANES, d_inner), np.float32)
    for h in range(n_heads):
        expand[h, h * SSD_HEAD_DIM:(h + 1) * SSD_HEAD_DIM] = 1.0
        expand[LANES + h, h * SSD_HEAD_DIM:(h + 1) * SSD_HEAD_DIM] = 1.0
    pad = ((0, 0), (0, 0), (0, LANES - n_heads))
    dtb = jnp.pad(dt_bias.reshape(2, 1, n_heads), pad)
    alog = jnp.pad(a_log.reshape(2, 1, n_heads), pad)

    def chunk_specs(cmap):
        return [
            pl.BlockSpec((ch, d_inner), lambda g: (cmap(g), col_x // d_inner)),
            pl.BlockSpec((ch, gn), lambda g: (cmap(g), (col_x + d_inner) // gn)),
            pl.BlockSpec((ch, gn), lambda g: (cmap(g), (col_x + d_inner) // gn + 1)),
            pl.BlockSpec((ch, LANES), lambda g: (cmap(g), 0)),
        ]

    def fwd(g):
        return g

    def rev(g):
        return n_chunks - 1 - g

    kern = functools.partial(_ssd_scan_kernel, n_chunks=n_chunks, firsts=firsts, lasts=lasts,
                             valid=layout["valid"])
    return pl.pallas_call(
        kern,
        grid=(n_chunks,),
        in_specs=chunk_specs(fwd) + chunk_specs(rev) + [
            pl.BlockSpec((2, 1, LANES), lambda g: (0, 0, 0)),
            pl.BlockSpec((2, 1, LANES), lambda g: (0, 0, 0)),
            pl.BlockSpec((2, ch, ch), lambda g: (0, 0, 0)),
            pl.BlockSpec((2, ch, ch), lambda g: (0, 0, 0)),
            pl.BlockSpec((2 * LANES, d_inner), lambda g: (0, 0)),
        ],
        out_specs=[pl.BlockSpec((ch, d_inner), lambda g: (fwd(g), 0)),
                   pl.BlockSpec((ch, d_inner), lambda g: (rev(g), 0))],
        out_shape=[jax.ShapeDtypeStruct((rows, d_inner), BF16)] * 2,
        scratch_shapes=[
            pltpu.VMEM((2, SSD_N_GROUPS, SSD_D_STATE, SSD_HEADS_PER_GROUP * SSD_HEAD_DIM), F32)],
        compiler_params=_cparams("arbitrary"),
        name="ssd_scan",
    )(xbc, xbc, xbc, small, xbc, xbc, xbc, small, dtb, alog, jnp.asarray(tri, BF16),
      jnp.asarray(mask), jnp.asarray(expand, BF16))


def _gla_levels():
    lv, m = [], GLA_CHUNK // 2
    while m >= 1:
        lv.append(m)
        m //= 2
    return tuple(lv)


def _gla_level_exponent(cs_scr, cs_t, t, m, reverse):
    ch = cs_t.shape[0]
    off = m - 1 if reverse else m

    def ref_tile(r):
        return jnp.broadcast_to(cs_scr[t, pl.ds(r, 1), :], (8, LANES))

    if m >= 8:
        pieces = []
        for p in range(ch // (2 * m)):
            lo = 2 * m * p
            ref = jnp.concatenate([ref_tile(lo + off)] * (m // 8), axis=0)
            early, late = cs_t[lo:lo + m], cs_t[lo + m:lo + 2 * m]
            pieces += [early - ref, ref - late] if reverse else [ref - early, late - ref]
        return jnp.concatenate(pieces, axis=0)

    row = lax.broadcasted_iota(jnp.int32, (ch, 1), 0)
    in_late = ((row // m) & 1) == 1
    is_target = in_late != reverse
    if m == 1:
        x3 = cs_t.reshape(ch // 8, 8, LANES)
        nb = pltpu.roll(x3, 1 if reverse else 7, 1).reshape(ch, LANES)
        return jnp.where(is_target, 0.0, nb - cs_t)
    sub = lax.broadcasted_iota(jnp.int32, (8, 1), 0)
    pieces = []
    for p in range(ch // 8):
        ref = ref_tile(8 * p + off)
        for q in range(1, 8 // (2 * m)):
            ref = jnp.where(sub >= 2 * m * q, ref_tile(8 * p + 2 * m * q + off), ref)
        pieces.append(ref)
    delta = cs_t - jnp.concatenate(pieces, axis=0)
    return jnp.where(is_target, delta, -delta)


def _gla_level_operand(q_scr, k_ref, sl, m, reverse):
    ch = q_scr.shape[0]
    if m >= 16:
        pieces = []
        for p in range(ch // (2 * m)):
            lo = 2 * m * p
            early, late = slice(lo, lo + m), slice(lo + m, lo + 2 * m)
            if reverse:
                pieces += [q_scr[early, sl], k_ref[late, sl]]
            else:
                pieces += [k_ref[early, sl], q_scr[late, sl]]
        return jnp.concatenate(pieces, axis=0)
    row = lax.broadcasted_iota(jnp.int32, (ch, 1), 0)
    is_target = (((row // m) & 1) == 1) != reverse
    return jnp.where(is_target, q_scr[:, sl], k_ref[:, sl])


def _gla_scan_kernel(q_ref, k_ref, v_ref, small_ref, gu_ref, gb_ref, tri_ref, lev_ref, o_ref,
                     state_scr, cs_scr, att_scr, diag_scr, q_scr, *, reverse, n_chunks, firsts, lasts,
                     valid, dk, dv):
    g = pl.program_id(0)
    c = n_chunks - 1 - g if reverse else g
    ch = GLA_CHUNK
    levels = _gla_levels()
    tiles_per_head = dk // LANES

    @pl.when(_is_any(c, lasts if reverse else firsts))
    def _():
        state_scr[...] = jnp.zeros_like(state_scr)

    ok = _row_valid(c * ch, ch, valid)
    u = _dot(small_ref[...].astype(BF16), gu_ref[...]) + gb_ref[...]
    gate = jnp.where(ok, -_softplus(-u) / GLA_GATE_TAU, 0.0)
    cs = _dot_split2_lhs(tri_ref[...], gate)
    n_tiles = cs.shape[1] // LANES
    for t in range(n_tiles):
        cs_scr[t] = cs[:, t * LANES:(t + 1) * LANES]
    total = jnp.sum(gate, axis=0, keepdims=True)
    lev = lev_ref[...]
    q_scr[...] = q_ref[...] * jnp.asarray(dk ** -0.5, BF16)

    half = ch // 2
    tgt, src = (slice(0, half), slice(half, ch)) if reverse else (slice(half, ch), slice(0, half))
    quads = (slice(0, half), slice(half, ch))

    @pl.when(g == 0)
    def _():
        att_scr[:, src, tgt] = jnp.zeros((GLA_N_HEADS, half, half), BF16)

    for i, m in enumerate(levels):
        e = [jnp.exp(_gla_level_exponent(cs_scr, cs[:, t * LANES:(t + 1) * LANES], t, m, reverse)
                     ).astype(BF16) for t in range(n_tiles)]
        for h in range(GLA_N_HEADS):
            sl = slice(h * dk, (h + 1) * dk)
            e_h = jnp.concatenate(e[h * tiles_per_head:(h + 1) * tiles_per_head], axis=1)
            x = _gla_level_operand(q_scr, k_ref, sl, m, reverse) * e_h
            if i == 0:
                att_scr[h, tgt, src] = _dot_nt(x[tgt], x[src]).astype(BF16)
                continue
            for qi, qs in enumerate(quads):
                p = _dot_nt(x[qs], x[qs])
                diag_scr[h, qi] = jnp.where(lev == i, p, 0.0 if i == 1 else diag_scr[h, qi])

    for h in range(GLA_N_HEADS):
        sl = slice(h * dk, (h + 1) * dk)
        vs = slice(h * dv, (h + 1) * dv)
        vh = v_ref[:, vs]
        for qi, qs in enumerate(quads):
            p = _dot_nt(q_scr[qs, sl], k_ref[qs, sl])
            att_scr[h, qs, qs] = jnp.where(lev == len(levels), p, diag_scr[h, qi]).astype(BF16)
        st = state_scr[h]
        qd = q_scr[:, sl] * jnp.exp(cs[:, sl]).astype(BF16)
        o = _dot(att_scr[h], vh) + _dot_nt(qd, st.astype(BF16))
        o_ref[:, vs] = o.astype(BF16)
        kd = k_ref[:, sl] * jnp.exp(total[:, sl] - cs[:, sl]).astype(BF16)
        state_scr[h] = st * jnp.exp(total[:, sl]) + _dot_tn(vh, kd)


N_GLA_IN = 8
N_GLA_SCRATCH = 5


def _gla_pair_kernel(*refs, **static):
    ins, outs, scr = refs[:2 * N_GLA_IN], refs[2 * N_GLA_IN:2 * N_GLA_IN + 2], refs[2 * N_GLA_IN + 2:]
    for r in (0, 1):
        _gla_scan_kernel(*ins[r * N_GLA_IN:(r + 1) * N_GLA_IN], outs[r],
                         *scr[r * N_GLA_SCRATCH:(r + 1) * N_GLA_SCRATCH], reverse=bool(r), **static)


def _gla_scan(big, small, gate_up, gate_b, layout, col_q, col_k, col_v, dk, dv):
    rows = big.shape[0]
    ch = GLA_CHUNK
    n_chunks = rows // ch
    hv = GLA_N_HEADS * dv
    parts = [_gla_direction(big, small, gate_up[r], gate_b[r], layout, col_q, col_k, col_v, dk, dv,
                            bool(r)) for r in (0, 1)]
    static = parts[0][4]
    return pl.pallas_call(
        functools.partial(_gla_pair_kernel, **static),
        grid=(n_chunks,),
        in_specs=parts[0][1] + parts[1][1],
        out_specs=[parts[0][2], parts[1][2]],
        out_shape=[jax.ShapeDtypeStruct((rows, hv), BF16)] * 2,
        scratch_shapes=parts[0][3] + parts[1][3],
        compiler_params=_cparams("arbitrary"),
        name="gla_scan",
    )(*parts[0][0], *parts[1][0])


def _gla_direction(big, small, gate_up, gate_b, layout, col_q, col_k, col_v, dk, dv, reverse):
    rows = big.shape[0]
    ch = GLA_CHUNK
    n_chunks = rows // ch
    firsts = tuple(s // ch for s, _ in layout["spans"])
    lasts = tuple((s + n) // ch - 1 for s, n in layout["spans"])
    hk = GLA_N_HEADS * dk
    hv = GLA_N_HEADS * dv
    levels = _gla_levels()

    idx = np.arange(ch)
    l_, s_ = idx[:, None], idx[None, :]
    seen = (l_ <= s_) if reverse else (l_ >= s_)
    x = l_ ^ s_
    top = np.floor(np.log2(np.maximum(x, 1))).astype(np.int64)
    lvl = np.where(x == 0, len(levels), (len(levels) - 1) - top)
    lev = np.where(seen, lvl, -1).astype(np.int32)[:ch // 2, :ch // 2]

    lo = SSD_N_GROUPS * SSD_HEADS_PER_GROUP + GLA_GATE_RANK * int(reverse)
    gu = jnp.zeros((LANES, hk), F32).at[lo:lo + GLA_GATE_RANK].set(gate_up).astype(BF16)

    def cmap(g):
        return n_chunks - 1 - g if reverse else g

    static = dict(n_chunks=n_chunks, firsts=firsts, lasts=lasts, valid=layout["valid"], dk=dk, dv=dv)
    in_specs = [
        pl.BlockSpec((ch, hk), lambda g: (cmap(g), col_q // hk)),
        pl.BlockSpec((ch, hk), lambda g: (cmap(g), col_k // hk)),
        pl.BlockSpec((ch, hv), lambda g: (cmap(g), col_v // hv)),
        pl.BlockSpec((ch, LANES), lambda g: (cmap(g), 0)),
        pl.BlockSpec((LANES, hk), lambda g: (0, 0)),
        pl.BlockSpec((1, hk), lambda g: (0, 0)),
        pl.BlockSpec((ch, ch), lambda g: (0, 0)),
        pl.BlockSpec((ch // 2, ch // 2), lambda g: (0, 0)),
    ]
    out_spec = pl.BlockSpec((ch, hv), lambda g: (cmap(g), 0))
    scratch = [
        pltpu.VMEM((GLA_N_HEADS, dv, dk), F32),
        pltpu.VMEM((hk // LANES, ch, LANES), F32),
        pltpu.VMEM((GLA_N_HEADS, ch, ch), BF16),
        pltpu.VMEM((GLA_N_HEADS, 2, ch // 2, ch // 2), F32),
        pltpu.VMEM((ch, hk), BF16),
    ]
    operands = (big, big, big, small, gu, gate_b.reshape(1, hk),
                jnp.asarray(seen.astype(np.float32), BF16), jnp.asarray(lev))
    return operands, in_specs, out_spec, scratch, static


def _merge_kernel(ysf_ref, ysb_ref, xs_ref, z_ref, ogf_ref, ogb_ref, ogate_ref, ms_ref, mg_ref, dskip_ref,
                  snw_ref, gnw_ref, ws_ref, wg_ref, o_ref, a_ssd, a_gla, *, dv):
    s = pl.program_id(0)
    tm = a_ssd.shape[1]

    @pl.when(s == 0)
    def _():
        a_ssd[1] = jnp.zeros(a_ssd.shape[1:], BF16)
        a_gla[1] = jnp.zeros(a_gla.shape[1:], BF16)

    slot = s % 2
    for r0 in range(0, tm, PROLOGUE_ROWS):
        rs = slice(r0, r0 + PROLOGUE_ROWS)
        y = ysf_ref[rs, :].astype(F32) + ysb_ref[rs, :].astype(F32)
        y = y + xs_ref[rs, :].astype(F32) * dskip_ref[...]
        y = y * _silu(z_ref[rs, :].astype(F32))
        ms = jnp.mean(y * y, axis=-1, keepdims=True)
        a_ssd[slot, rs, :] = (y * lax.rsqrt(ms + EPS) * snw_ref[...]).astype(BF16)
        for h in range(GLA_N_HEADS):
            sl = slice(h * dv, (h + 1) * dv)
            o = ogf_ref[rs, sl].astype(F32) + ogb_ref[rs, sl].astype(F32)
            ms = jnp.mean(o * o, axis=-1, keepdims=True)
            o = o * lax.rsqrt(ms + EPS) * gnw_ref[...]
            a_gla[slot, rs, sl] = (o * _silu(ogate_ref[rs, sl].astype(F32))).astype(BF16)

    y_ssd = _dot(a_ssd[1 - slot], ws_ref[...])
    y_gla = _dot(a_gla[1 - slot], wg_ref[...])
    merged = _sigmoid(ms_ref[...].astype(F32)) * y_ssd
    merged = merged + _sigmoid(mg_ref[...].astype(F32)) * y_gla
    o_ref[...] = merged.astype(BF16)


def _merge(y_fwd, y_bwd, big, o_fwd, o_bwd, d_skip, ssd_norm_w, gla_norm_w, w_ssd, w_gla, col_x, col_og,
           col_merge, tm=256):
    rows, d = big.shape[0], w_ssd.shape[1]
    di = w_ssd.shape[0]
    dvt = w_gla.shape[0]
    dv = dvt // GLA_N_HEADS
    n_tiles = rows // tm
    kern = functools.partial(_merge_kernel, dv=dv)

    def cur(s):
        return jnp.minimum(s, n_tiles - 1)

    def prev(s):
        return jnp.maximum(s - 1, 0)

    resident = pl.Buffered(1)
    return pl.pallas_call(
        kern,
        grid=(n_tiles + 1,),
        in_specs=[
            pl.BlockSpec((tm, di), lambda s: (cur(s), 0)),
            pl.BlockSpec((tm, di), lambda s: (cur(s), 0)),
            pl.BlockSpec((tm, di), lambda s: (cur(s), col_x // di)),
            pl.BlockSpec((tm, di), lambda s: (cur(s), 0)),
            pl.BlockSpec((tm, dvt), lambda s: (cur(s), 0)),
            pl.BlockSpec((tm, dvt), lambda s: (cur(s), 0)),
            pl.BlockSpec((tm, dvt), lambda s: (cur(s), col_og // dvt)),
            pl.BlockSpec((tm, d), lambda s: (prev(s), col_merge // d)),
            pl.BlockSpec((tm, d), lambda s: (prev(s), col_merge // d + 1)),
            pl.BlockSpec((1, di), lambda s: (0, 0)),
            pl.BlockSpec((1, di), lambda s: (0, 0)),
            pl.BlockSpec((1, dv), lambda s: (0, 0)),
            pl.BlockSpec((di, d), lambda s: (0, 0), pipeline_mode=resident),
            pl.BlockSpec((dvt, d), lambda s: (0, 0), pipeline_mode=resident),
        ],
        out_specs=pl.BlockSpec((tm, d), lambda s: (prev(s), 0)),
        out_shape=jax.ShapeDtypeStruct((rows, d), BF16),
        scratch_shapes=[pltpu.VMEM((2, tm, di), BF16), pltpu.VMEM((2, tm, dvt), BF16)],
        compiler_params=_cparams("arbitrary"),
        name="merge",
    )(y_fwd, y_bwd, big, big, o_fwd, o_bwd, big, big, big,
      jnp.repeat(d_skip, SSD_HEAD_DIM).reshape(1, di), ssd_norm_w.reshape(1, di),
      gla_norm_w.reshape(1, dv), w_ssd, w_gla)


def _mm_res_kernel(*refs):
    *aw_refs, h_ref, o_ref = refs
    n_terms = len(aw_refs) // 2
    acc = h_ref[...]
    for a_ref, w_ref in zip(aw_refs[:n_terms], aw_refs[n_terms:]):
        acc = acc + _dot(a_ref[...], w_ref[...])
    o_ref[...] = acc


def _mm_res(a_list, w_list, h, n_split, tm=512):
    rows = h.shape[0]
    n = w_list[0].shape[1]
    tn = n // n_split
    a_specs = [pl.BlockSpec((tm, a.shape[1]), lambda j, i: (i, 0)) for a in a_list]
    w_specs = [pl.BlockSpec((w.shape[0], tn), lambda j, i: (0, j)) for w in w_list]
    return pl.pallas_call(
        _mm_res_kernel,
        grid=(n_split, rows // tm),
        in_specs=a_specs + w_specs + [pl.BlockSpec((tm, tn), lambda j, i: (i, j))],
        out_specs=pl.BlockSpec((tm, tn), lambda j, i: (i, j)),
        out_shape=jax.ShapeDtypeStruct((rows, n), F32),
        compiler_params=_cparams("arbitrary", "arbitrary"),
        name="mm_res",
    )(*a_list, *w_list, h)


def _ffn_up_kernel(h_ref, prev_ref, next_ref, nw_ref, wg_ref, wu_ref, cwg_ref, cwu_ref, cbg_ref,
                   cbu_ref, o_ref, xn_scr, *, valid):
    tm = h_ref.shape[0]

    def norm(x):
        ms = jnp.mean(x * x, axis=-1, keepdims=True)
        return (x * lax.rsqrt(ms + EPS) * nw_ref[...]).astype(BF16)

    @pl.when(pl.program_id(1) == 0)
    def _():
        xn_scr[0:HALO, :] = norm(prev_ref[...])
        for r0 in range(0, tm, PROLOGUE_ROWS):
            xn_scr[HALO + r0:HALO + r0 + PROLOGUE_ROWS, :] = norm(h_ref[r0:r0 + PROLOGUE_ROWS, :])
        xn_scr[HALO + tm:2 * HALO + tm, :] = norm(next_ref[...])

    n = tm + 2 * HALO

    def conv(u, w_ref, b_ref, cs):
        y = b_ref[:, cs] + pltpu.roll(u, 1, 0)[HALO:HALO + tm] * w_ref[0:1, cs]
        y = y + u[HALO:HALO + tm] * w_ref[1:2, cs]
        return y + pltpu.roll(u, n - 1, 0)[HALO:HALO + tm] * w_ref[2:3, cs]

    ok = _row_valid(pl.program_id(0) * tm, tm, valid)
    for c0 in range(0, o_ref.shape[1], FFN_COL_CHUNK):
        cs = slice(c0, c0 + FFN_COL_CHUNK)
        gate = conv(_dot_row_pieces(xn_scr, wg_ref, cs), cwg_ref, cbg_ref, cs)
        up = conv(_dot_row_pieces(xn_scr, wu_ref, cs), cwu_ref, cbu_ref, cs)
        o_ref[:, cs] = jnp.where(ok, _silu(gate) * up, 0.0).astype(BF16)


def _ffn_up(h, norm_w, w_up, conv_w, conv_b, layout, tf=512):
    rows, d = h.shape
    dff = w_up.shape[1] // 2
    tm = ROW_TILE
    hb = tm // HALO
    last = rows // HALO - 1
    nj = dff // tf
    kern = functools.partial(_ffn_up_kernel, valid=layout["valid"])
    return pl.pallas_call(
        kern,
        grid=(rows // tm, nj),
        in_specs=[
            pl.BlockSpec((tm, d), lambda i, j: (i, 0)),
            pl.BlockSpec((HALO, d), lambda i, j: (jnp.maximum(i * hb - 1, 0), 0)),
            pl.BlockSpec((HALO, d), lambda i, j: (jnp.minimum((i + 1) * hb, last), 0)),
            pl.BlockSpec((1, d), lambda i, j: (0, 0)),
            pl.BlockSpec((d, tf), lambda i, j: (0, j)),
            pl.BlockSpec((d, tf), lambda i, j: (0, nj + j)),
            pl.BlockSpec((3, tf), lambda i, j: (0, j)),
            pl.BlockSpec((3, tf), lambda i, j: (0, nj + j)),
            pl.BlockSpec((1, tf), lambda i, j: (0, j)),
            pl.BlockSpec((1, tf), lambda i, j: (0, nj + j)),
        ],
        out_specs=pl.BlockSpec((tm, tf), lambda i, j: (i, j)),
        out_shape=jax.ShapeDtypeStruct((rows, dff), BF16),
        scratch_shapes=[pltpu.VMEM((tm + 2 * HALO, d), BF16)],
        compiler_params=_cparams("parallel", "arbitrary"),
        name="ffn_up",
    )(h, h, h, norm_w.reshape(1, d), w_up, w_up, conv_w, conv_w,
      conv_b.reshape(1, 2 * dff), conv_b.reshape(1, 2 * dff))


def _final_norm_kernel(h_ref, nw_ref, o_ref):
    x = h_ref[...]
    ms = jnp.mean(x * x, axis=-1, keepdims=True)
    o_ref[0] = x * lax.rsqrt(ms + EPS) * nw_ref[...]


def _final_norm(h, norm_w, first_row, batch, length, tr=ROW_ALIGN):
    d = h.shape[1]
    per_seq = (ROW_ALIGN + length) // tr
    b0 = (first_row + ROW_ALIGN) // tr
    return pl.pallas_call(
        _final_norm_kernel,
        grid=(batch, length // tr),
        in_specs=[
            pl.BlockSpec((tr, d), lambda b, t: (b0 + b * per_seq + t, 0)),
            pl.BlockSpec((1, d), lambda b, t: (0, 0)),
        ],
        out_specs=pl.BlockSpec((1, tr, d), lambda b, t: (b, t, 0)),
        out_shape=jax.ShapeDtypeStruct((batch, length, d), F32),
        compiler_params=_cparams("parallel", "parallel"),
        name="final_norm",
    )(h, norm_w.reshape(1, d))


def _make_layout(groups):
    seqs, valid, row = [], [], 0
    for batch, length in groups:
        assert length % ROW_ALIGN == 0
        for _ in range(batch):
            n = ROW_ALIGN + length
            seqs.append((row, n))
            valid.append((row + FRONT_PAD, row + n))
            row += n
    rows = (row // ROW_TILE + 1) * ROW_TILE
    spans = tuple(seqs) + ((row, rows - row),)
    return dict(seqs=tuple(seqs), spans=spans, valid=tuple(valid), used_rows=row, rows=rows)


def kernel(x_prompt, x_sample, meta_tokens, mix_norm_w, w_in, ssd_conv_w, ssd_conv_b, ssd_dt_bias,
           ssd_a_log, ssd_d, ssd_norm_w, ssd_w_out, gla_gate_up, gla_gate_b, gla_norm_w, gla_w_out,
           w_mix_out, ffn_norm_w, ffn_w_up, ffn_conv_w, ffn_conv_b, ffn_w_down, final_norm_w):
    depth, d = mix_norm_w.shape
    groups = [(x_prompt.shape[0], x_prompt.shape[1]), (x_sample.shape[0], x_sample.shape[1])]
    layout = _make_layout(groups)
    rows = layout["rows"]

    n_heads = ssd_dt_bias.shape[-1]
    d_inner = n_heads * SSD_HEAD_DIM
    conv_dim = ssd_conv_w.shape[-1]
    d_k = gla_gate_up.shape[-1]
    d_v = gla_w_out.shape[1]
    dk, dv = d_k // GLA_N_HEADS, d_v // GLA_N_HEADS
    sizes = (d_inner, conv_dim, n_heads, d_k, d_k, d_v, d_v, 2 * GLA_GATE_RANK, 2 * d)
    off = np.concatenate([[0], np.cumsum(sizes)])
    assert off[-1] == w_in.shape[-1]
    wide = [0, 1, 3, 4, 5, 6, 8]
    col, pos = {}, 0
    for s in wide:
        col[s] = pos
        pos += sizes[s]

    zero_front = jnp.zeros((FRONT_PAD, d), F32)
    pieces = []
    for x in (x_prompt, x_sample):
        for b in range(x.shape[0]):
            pieces += [zero_front, meta_tokens.astype(F32), x[b]]
    if rows > layout["used_rows"]:
        pieces.append(jnp.zeros((rows - layout["used_rows"], d), F32))
    h = jnp.concatenate(pieces, axis=0)

    for i in range(depth):
        w = w_in[i]
        w_big = jnp.concatenate([w[:, off[s]:off[s + 1]] for s in wide], axis=1).astype(BF16)
        w_small = jnp.concatenate(
            [w[:, off[2]:off[3]], w[:, off[7]:off[8]],
             jnp.zeros((d, LANES - n_heads - 2 * GLA_GATE_RANK), F32)], axis=1).astype(BF16)

        big, small = _inproj(h, mix_norm_w[i], w_big, w_small, ssd_conv_w[i], ssd_conv_b[i], col[1])
        y_fwd, y_bwd = _ssd_scan(big, col[1], small, ssd_dt_bias[i], ssd_a_log[i], layout)
        o_fwd, o_bwd = _gla_scan(big, small, gla_gate_up[i], gla_gate_b[i], layout,
                                 col[3], col[4], col[5], dk, dv)
        merged = _merge(y_fwd, y_bwd, big, o_fwd, o_bwd, ssd_d[i], ssd_norm_w[i], gla_norm_w[i],
                        ssd_w_out[i].astype(BF16), gla_w_out[i].astype(BF16), col[1], col[6], col[8])
        h = _mm_res([merged], [w_mix_out[i].astype(BF16)], h, n_split=1)
        act = _ffn_up(h, ffn_norm_w[i], ffn_w_up[i].astype(BF16), ffn_conv_w[i], ffn_conv_b[i],
                      layout)
        h = _mm_res([act], [ffn_w_down[i].astype(BF16)], h, n_split=2)

    outs, si = [], 0
    for batch, length in groups:
        outs.append(_final_norm(h, final_norm_w, layout["seqs"][si][0], batch, length))
        si += batch
    return tuple(outs)
```
